```python
import jax
import jax.numpy as jnp
from jax import lax
import numpy as np

D_MODEL = 1024
BATCH = 8
SEQ = 2048
DEPTH = 1
DEC_BATCH = 128
DEC_SEQ = 8
PAST_LEN = 16384
PAGE_SIZE = 128

N_MEM = 256
CONV_WIDTH = 3
D_CONV = D_MODEL
RWKV_HEAD = 64
RWKV_HEADS = D_MODEL // RWKV_HEAD
D_RWKV = RWKV_HEADS * RWKV_HEAD
DECAY_LORA = 64
ICL_LORA = 64
MEM_HEADS = 4
MEM_HEAD_DIM = D_MODEL // MEM_HEADS
D_MEM = MEM_HEADS * MEM_HEAD_DIM
N_BRANCH = 3
D_SHIFT = 3 * D_RWKV + DECAY_LORA + ICL_LORA
D_IN = 4 * D_CONV + D_SHIFT + D_RWKV + 2 * D_MEM + N_BRANCH * D_MODEL
NORM_EPS = 1e-6
GN_EPS = RWKV_HEAD * 1e-5

kernel_name = 'hybrid_shortconv_rwkv7_memxattn_step'


def _in_split_points():
    sizes = [D_CONV, D_CONV, D_CONV, D_CONV,
             D_SHIFT, D_RWKV,
             D_MEM, D_MEM,
             D_MODEL, D_MODEL, D_MODEL]
    pts = []
    acc = 0
    for s in sizes[:-1]:
        acc += s
        pts.append(acc)
    return pts


def _rmsnorm(x, g):
    xf = x.astype(jnp.float32)
    y = xf * lax.rsqrt(jnp.mean(xf * xf, axis=-1, keepdims=True) + NORM_EPS)
    return (y * g.astype(jnp.float32)).astype(x.dtype)


def _short_conv(u, conv_w, conv_b, conv_state):
    t = u.shape[1]
    ext = jnp.concatenate([conv_state.astype(u.dtype), u], axis=1)
    y = conv_b
    for j in range(CONV_WIDTH):
        y = y + ext[:, j:j + t] * conv_w[j]
    return y, ext[:, t:]


def _wkv_step(S, inp):
    r, w, k, v, kk, a = inp
    sa = jnp.einsum('bhvk,bhk->bhv', S, kk)
    S = (S * w[:, :, None, :] - sa[..., None] * (kk * a)[:, :, None, :]
         + v[..., None] * k[:, :, None, :])
    o = jnp.einsum('bhvk,bhk->bhv', S, r)
    return S, o


def _rwkv7(p, prev_row, wkv_state, mu, w0, w_up, a0, a_up, k_k, k_a, r_k, ln_g, ln_b):
    b, t, _ = p.shape
    pf = p.astype(jnp.float32)
    prev = jnp.concatenate([prev_row[:, None].astype(jnp.float32), pf[:, :-1]], axis=1)
    ps = pf + mu.astype(jnp.float32) * (prev - pf)
    r, k, v, wd, ad = jnp.split(ps, [D_RWKV, 2 * D_RWKV, 3 * D_RWKV, 3 * D_RWKV + DECAY_LORA], axis=-1)
    w_log = -jax.nn.softplus(-(w0 + jnp.tanh(wd) @ w_up)) - 0.5
    decay = jnp.exp(-jnp.exp(w_log))
    a = jax.nn.sigmoid(a0 + ad @ a_up)
    heads = lambda z: z.reshape(b, t, RWKV_HEADS, RWKV_HEAD)
    r, k, v, decay, a = heads(r), heads(k), heads(v), heads(decay), heads(a)
    kk = k * k_k.reshape(RWKV_HEADS, RWKV_HEAD)
    kk = kk * lax.rsqrt(jnp.sum(kk * kk, axis=-1, keepdims=True) + 1e-12)
    k = k * (1.0 + (a - 1.0) * k_a.reshape(RWKV_HEADS, RWKV_HEAD))
    seq = tuple(jnp.moveaxis(z, 1, 0) for z in (r, decay, k, v, kk, a))
    s_final, o = lax.scan(_wkv_step, wkv_state.astype(jnp.float32), seq)
    o = jnp.moveaxis(o, 0, 1)
    mean = jnp.mean(o, axis=-1, keepdims=True)
    var = jnp.mean(jnp.square(o - mean), axis=-1, keepdims=True)
    o = ((o - mean) * lax.rsqrt(var + GN_EPS) * ln_g.reshape(RWKV_HEADS, RWKV_HEAD)
         + ln_b.reshape(RWKV_HEADS, RWKV_HEAD))
    o = o + jnp.sum(r * k * r_k, axis=-1, keepdims=True) * v
    return o.reshape(b, t, D_RWKV), p[:, -1], s_final


def _mem_kv(mem, norm_g, w_kv):
    b, m, _ = mem.shape
    kv = _rmsnorm(mem, norm_g) @ w_kv
    mk, mv = jnp.split(kv, 2, axis=-1)
    return (mk.reshape(b, m, MEM_HEADS, MEM_HEAD_DIM), mv.reshape(b, m, MEM_HEADS, MEM_HEAD_DIM))


def _mem_attention(q, mem_k, mem_v):
    b, t, _ = q.shape
    qh = q.reshape(b, t, MEM_HEADS, MEM_HEAD_DIM).astype(jnp.float32)
    s = jnp.einsum('bthd,bmhd->bhtm', qh, mem_k.astype(jnp.float32)) * (MEM_HEAD_DIM ** -0.5)
    pr = jax.nn.softmax(s, axis=-1)
    o = jnp.einsum('bhtm,bmhd->bthd', pr, mem_v.astype(jnp.float32))
    return o.reshape(b, t, D_MEM)


def _layer(x, conv_state, shift_state, wkv_state, mem_k, mem_v,
           norm_g, w_in, conv_w, conv_b, mu, w0, w_up, a0, a_up, k_k, k_a, r_k,
           ln_g, ln_b, w_branch, w_out):
    xn = _rmsnorm(x, norm_g)
    proj = xn @ w_in
    (h, b_gate, c_gate, g_conv, p_shift, g_rwkv, q_mem, g_mem,
     m_conv, m_rwkv, m_mem) = jnp.split(proj, _in_split_points(), axis=-1)
    y_conv, conv_new = _short_conv(c_gate * h, conv_w, conv_b, conv_state)
    y_conv = b_gate * y_conv * jax.nn.silu(g_conv)
    y_rwkv, shift_new, wkv_new = _rwkv7(p_shift, shift_state, wkv_state, mu, w0, w_up, a0, a_up,
                                        k_k, k_a, r_k, ln_g, ln_b)
    y_rwkv = y_rwkv.astype(x.dtype) * jax.nn.silu(g_rwkv)
    y_mem = _mem_attention(q_mem, mem_k, mem_v).astype(x.dtype) * jax.nn.silu(g_mem)
    merged = (jax.nn.sigmoid(m_conv) * (y_conv @ w_branch[0])
              + jax.nn.sigmoid(m_rwkv) * (y_rwkv @ w_branch[1])
              + jax.nn.sigmoid(m_mem) * (y_mem @ w_branch[2]))
    return x + merged @ w_out, conv_new, shift_new, wkv_new


def setup_inputs(seed: int = 0) -> dict:
    key = jax.random.key(seed)
    ks = jax.random.split(key, 32)
    nrm = lambda k, shape, s: jax.random.normal(k, shape, jnp.float32) * s
    H, N = RWKV_HEADS, RWKV_HEAD
    return {
        'x_prompt': nrm(ks[0], (BATCH, SEQ, D_MODEL), 1.0),
        'x_sample': nrm(ks[1], (DEC_BATCH, DEC_SEQ, D_MODEL), 1.0),
        'mem_prompt': nrm(ks[2], (BATCH, N_MEM, D_MODEL), 1.0),
        'cache_mem_k': nrm(ks[3], (DEPTH, DEC_BATCH, N_MEM, MEM_HEADS, MEM_HEAD_DIM), 1.0),
        'cache_mem_v': nrm(ks[4], (DEPTH, DEC_BATCH, N_MEM, MEM_HEADS, MEM_HEAD_DIM), 1.0),
        'state_conv': nrm(ks[5], (DEPTH, DEC_BATCH, CONV_WIDTH - 1, D_CONV), 1.0),
        'state_shift': nrm(ks[6], (DEPTH, DEC_BATCH, D_SHIFT), 1.0),
        'state_wkv': nrm(ks[7], (DEPTH, DEC_BATCH, H, N, N), 0.3),
        'norm_in': 1.0 + nrm(ks[8], (DEPTH, D_MODEL), 0.02),
        'w_in': nrm(ks[9], (DEPTH, D_MODEL, D_IN), D_MODEL ** -0.5),
        'conv_w': nrm(ks[10], (DEPTH, CONV_WIDTH, D_CONV), CONV_WIDTH ** -0.5),
        'conv_b': nrm(ks[11], (DEPTH, D_CONV), 0.02),
        'shift_mu': jax.random.uniform(ks[12], (DEPTH, D_SHIFT), jnp.float32),
        'decay_w0': jax.random.uniform(ks[13], (DEPTH, D_RWKV), jnp.float32, -4.0, 1.0),
        'decay_up': nrm(ks[14], (DEPTH, DECAY_LORA, D_RWKV), 0.1),
        'icl_a0': nrm(ks[15], (DEPTH, D_RWKV), 0.1),
        'icl_up': nrm(ks[16], (DEPTH, ICL_LORA, D_RWKV), 0.5 * ICL_LORA ** -0.5),
        'k_k': 0.85 + nrm(ks[17], (DEPTH, D_RWKV), 0.05),
        'k_a': 1.0 + nrm(ks[18], (DEPTH, D_RWKV), 0.05),
        'r_k': nrm(ks[19], (DEPTH, H, N), 0.1),
        'ln_x_g': 1.0 + nrm(ks[20], (DEPTH, D_RWKV), 0.02),
        'ln_x_b': nrm(ks[21], (DEPTH, D_RWKV), 0.02),
        'norm_mem': 1.0 + nrm(ks[22], (DEPTH, D_MODEL), 0.02),
        'w_mem_kv': nrm(ks[23], (DEPTH, D_MODEL, 2 * D_MEM), D_MODEL ** -0.5),
        'w_branch': nrm(ks[24], (DEPTH, N_BRANCH, D_CONV, D_MODEL), D_CONV ** -0.5),
        'w_out': nrm(ks[25], (DEPTH, D_MODEL, D_MODEL), D_MODEL ** -0.5),
        'norm_final': 1.0 + nrm(ks[26], (D_MODEL,), 0.02),
    }


def reference(x_prompt, x_sample, mem_prompt, cache_mem_k, cache_mem_v, state_conv, state_shift,
              state_wkv, norm_in, w_in, conv_w, conv_b, shift_mu, decay_w0, decay_up, icl_a0,
              icl_up, k_k, k_a, r_k, ln_x_g, ln_x_b, norm_mem, w_mem_kv, w_branch, w_out,
              norm_final):
    bp = x_prompt.shape[0]
    hp, hs = x_prompt, x_sample
    mk_p, mv_p, conv_p, shift_p, wkv_p = [], [], [], [], []
    conv_s, shift_s, wkv_s = [], [], []
    for l in range(DEPTH):
        lw = (norm_in[l], w_in[l], conv_w[l], conv_b[l], shift_mu[l], decay_w0[l], decay_up[l],
              icl_a0[l], icl_up[l], k_k[l], k_a[l], r_k[l], ln_x_g[l], ln_x_b[l],
              w_branch[l], w_out[l])
        mk, mv = _mem_kv(mem_prompt, norm_mem[l], w_mem_kv[l])
        zc = jnp.zeros((bp, CONV_WIDTH - 1, D_CONV), x_prompt.dtype)
        zs = jnp.zeros((bp, D_SHIFT), x_prompt.dtype)
        zw = jnp.zeros((bp, RWKV_HEADS, RWKV_HEAD, RWKV_HEAD), jnp.float32)
        hp, cp, sp, wp = _layer(hp, zc, zs, zw, mk, mv, *lw)
        hs, cs, ss, ws = _layer(hs, state_conv[l], state_shift[l], state_wkv[l],
                                cache_mem_k[l], cache_mem_v[l], *lw)
        mk_p.append(mk); mv_p.append(mv)
        conv_p.append(cp); shift_p.append(sp); wkv_p.append(wp)
        conv_s.append(cs); shift_s.append(ss); wkv_s.append(ws)
    y_prompt = _rmsnorm(hp, norm_final)
    y_sample = _rmsnorm(hs, norm_final)
    return (y_prompt, y_sample, jnp.stack(mk_p), jnp.stack(mv_p), jnp.stack(conv_p),
            jnp.stack(shift_p), jnp.stack(wkv_p), jnp.stack(conv_s), jnp.stack(shift_s),
            jnp.stack(wkv_s))
```

```python
import functools

import jax
import jax.numpy as jnp
from jax import lax
from jax.experimental import pallas as pl
from jax.experimental.pallas import tpu as pltpu

D_MODEL = 1024
N_MEM = 256
RWKV_HEAD = 64
RWKV_HEADS = D_MODEL // RWKV_HEAD
LORA = 64
MEM_HEADS = 4
MEM_HEAD_DIM = D_MODEL // MEM_HEADS
D_SHIFT = 3 * D_MODEL + 2 * LORA
NORM_EPS = 1e-6
GN_EPS = RWKV_HEAD * 1e-5
KK_EPS = 1e-12

_OFF_CONV = 0
_OFF_SHIFT = 4 * D_MODEL
_OFF_GRWKV = _OFF_SHIFT + D_SHIFT
_OFF_QMEM = _OFF_GRWKV + D_MODEL
_OFF_GMEM = _OFF_QMEM + D_MODEL
_OFF_MCONV = _OFF_GMEM + D_MODEL
_OFF_MRWKV = _OFF_MCONV + D_MODEL
_OFF_MMEM = _OFF_MRWKV + D_MODEL

V7X_SUBLANES = 8
V7X_LANES = 128
SEG_TILE = 256
TOKEN_TILE = 256
SAMPLE_MEM_BATCH_TILE = 8
SCAN_TIME_TILE = 64
VMEM_LIMIT = 56 * 1024 * 1024

_F32 = jnp.float32
_BF16 = jnp.bfloat16


def _dot(a, b):
    return jnp.dot(a, b, preferred_element_type=_F32)


def _rmsnorm(x, g):
    ms = jnp.mean(x * x, axis=-1, keepdims=True)
    return x * lax.rsqrt(ms + NORM_EPS) * g


def _sigmoid(x):
    return 1.0 / (1.0 + jnp.exp(-x))


def _silu(x):
    return x * _sigmoid(x)


def _seg_sum(x, ones_bd):
    outs = []
    for c in range(D_MODEL // SEG_TILE):
        xc = x[:, c * SEG_TILE:(c + 1) * SEG_TILE]
        hi = xc.astype(_BF16)
        lo = (xc - hi.astype(_F32)).astype(_BF16)
        outs.append(_dot(hi, ones_bd) + _dot(lo, ones_bd))
    return jnp.concatenate(outs, axis=-1)


def _prev_rows_carry(u, carry, n):
    r = pltpu.roll(u, n, axis=0)
    c = pltpu.roll(carry, n, axis=0)
    sub = lax.broadcasted_iota(jnp.int32, c.shape, 0)
    top = jnp.where(sub < n, c, r[0:V7X_SUBLANES])
    return jnp.concatenate([top, r[V7X_SUBLANES:]], axis=0)


def _bcast_state(st, rows):
    tb, _, c = st.shape
    return jnp.broadcast_to(st, (tb, V7X_SUBLANES, c)).reshape(rows, c)


def _mem_kv_kernel(mem_ref, g_ref, w_ref, k_ref, v_ref):
    xn = _rmsnorm(mem_ref[...], g_ref[...]).astype(_BF16)
    k_ref[...] = _dot(xn, w_ref[:, 0:D_MODEL])
    v_ref[...] = _dot(xn, w_ref[:, D_MODEL:2 * D_MODEL])


def _mem_kv(mem2d, g, w_bf):
    n = mem2d.shape[0]
    tm = TOKEN_TILE
    return pl.pallas_call(
        _mem_kv_kernel,
        grid=(n // tm,),
        in_specs=[
            pl.BlockSpec((tm, D_MODEL), lambda i: (i, 0)),
            pl.BlockSpec((1, D_MODEL), lambda i: (0, 0)),
            pl.BlockSpec((D_MODEL, 2 * D_MODEL), lambda i: (0, 0)),
        ],
        out_specs=[
            pl.BlockSpec((tm, D_MODEL), lambda i: (i, 0)),
            pl.BlockSpec((tm, D_MODEL), lambda i: (i, 0)),
        ],
        out_shape=[jax.ShapeDtypeStruct((n, D_MODEL), _F32)] * 2,
        compiler_params=pltpu.CompilerParams(
            dimension_semantics=("arbitrary",), vmem_limit_bytes=VMEM_LIMIT),
        name="mem_kv",
    )(mem2d, g, w_bf)


def _conv_body(x, prev1_fn, prev2_fn, g_ref, w_ref, cw_ref, cb_ref, wb_ref):
    xn = _rmsnorm(x, g_ref[...]).astype(_BF16)
    h = _dot(xn, w_ref[:, 0:D_MODEL])
    cg = _dot(xn, w_ref[:, 2 * D_MODEL:3 * D_MODEL])
    u = cg * h
    conv = (cb_ref[...] + prev2_fn(u) * cw_ref[0:1, :] + prev1_fn(u) * cw_ref[1:2, :]
            + u * cw_ref[2:3, :])
    bg = _dot(xn, w_ref[:, D_MODEL:2 * D_MODEL])
    gc = _dot(xn, w_ref[:, 3 * D_MODEL:4 * D_MODEL])
    y = bg * conv * _silu(gc)
    mc = _dot(xn, w_ref[:, 4 * D_MODEL:5 * D_MODEL])
    acc = _sigmoid(mc) * _dot(y.astype(_BF16), wb_ref[...])
    return u, acc


def _conv_prompt_kernel(x_ref, g_ref, w_ref, cw_ref, cb_ref, wb_ref, acc_ref, tail_ref, carry_ref):
    @pl.when(pl.program_id(1) == 0)
    def _():
        carry_ref[...] = jnp.zeros_like(carry_ref)

    carry = carry_ref[...]
    u, acc = _conv_body(
        x_ref[...],
        lambda u: _prev_rows_carry(u, carry, 1),
        lambda u: _prev_rows_carry(u, carry, 2),
        g_ref, w_ref, cw_ref, cb_ref, wb_ref)
    acc_ref[...] = acc
    last = u[u.shape[0] - V7X_SUBLANES:]
    carry_ref[...] = last
    tail_ref[...] = last


def _conv_sample_kernel(x_ref, st_ref, g_ref, w_ref, cw_ref, cb_ref, wb_ref, acc_ref, u_ref):
    tb = x_ref.shape[0]
    rows = tb * V7X_SUBLANES
    x = x_ref[...].reshape(rows, D_MODEL)
    s0 = _bcast_state(st_ref[:, 0:1, :], rows)
    s1 = _bcast_state(st_ref[:, 1:2, :], rows)
    t = lax.broadcasted_iota(jnp.int32, (rows, D_MODEL), 0) % V7X_SUBLANES

    def prev1(u):
        return jnp.where(t == 0, s1, pltpu.roll(u, 1, axis=0))

    def prev2(u):
        return jnp.where(t == 0, s0, jnp.where(t == 1, s1, pltpu.roll(u, 2, axis=0)))

    u, acc = _conv_body(x, prev1, prev2, g_ref, w_ref, cw_ref, cb_ref, wb_ref)
    acc_ref[...] = acc
    u_ref[...] = u


def _const_spec(shape, ngrid):
    zeros = (0,) * len(shape)
    if ngrid == 1:
        return pl.BlockSpec(shape, lambda i: zeros, pipeline_mode=pl.Buffered(1))
    return pl.BlockSpec(shape, lambda i, j: zeros, pipeline_mode=pl.Buffered(1))


def _conv_prompt(x, g, w_bf, cw, cb, wb_bf):
    b, t, _ = x.shape
    tm = TOKEN_TILE
    nt = t // tm
    return pl.pallas_call(
        _conv_prompt_kernel,
        grid=(b, nt),
        in_specs=[
            pl.BlockSpec((None, tm, D_MODEL), lambda i, j: (i, j, 0)),
            _const_spec((1, D_MODEL), 2),
            _const_spec((D_MODEL, 5 * D_MODEL), 2),
            _const_spec((3, D_MODEL), 2),
            _const_spec((1, D_MODEL), 2),
            _const_spec((D_MODEL, D_MODEL), 2),
        ],
        out_specs=[
            pl.BlockSpec((None, tm, D_MODEL), lambda i, j: (i, j, 0)),
            pl.BlockSpec((None, None, V7X_SUBLANES, D_MODEL), lambda i, j: (i, j, 0, 0)),
        ],
        out_shape=[
            jax.ShapeDtypeStruct((b, t, D_MODEL), _F32),
            jax.ShapeDtypeStruct((b, nt, V7X_SUBLANES, D_MODEL), _F32),
        ],
        scratch_shapes=[pltpu.VMEM((V7X_SUBLANES, D_MODEL), _F32)],
        compiler_params=pltpu.CompilerParams(
            dimension_semantics=("arbitrary", "arbitrary"), vmem_limit_bytes=VMEM_LIMIT),
        name="conv_prompt",
    )(x, g, w_bf, cw, cb, wb_bf)


def _conv_sample(x, st, g, w_bf, cw, cb, wb_bf):
    b, t, _ = x.shape
    tb = TOKEN_TILE // t
    return pl.pallas_call(
        _conv_sample_kernel,
        grid=(b // tb,),
        in_specs=[
            pl.BlockSpec((tb, t, D_MODEL), lambda i: (i, 0, 0)),
            pl.BlockSpec((tb, 2, D_MODEL), lambda i: (i, 0, 0)),
            _const_spec((1, D_MODEL), 1),
            _const_spec((D_MODEL, 5 * D_MODEL), 1),
            _const_spec((3, D_MODEL), 1),
            _const_spec((1, D_MODEL), 1),
            _const_spec((D_MODEL, D_MODEL), 1),
        ],
        out_specs=[
            pl.BlockSpec((tb * t, D_MODEL), lambda i: (i, 0)),
            pl.BlockSpec((tb * t, D_MODEL), lambda i: (i, 0)),
        ],
        out_shape=[jax.ShapeDtypeStruct((b * t, D_MODEL), _F32)] * 2,
        compiler_params=pltpu.CompilerParams(
            dimension_semantics=("arbitrary",), vmem_limit_bytes=VMEM_LIMIT),
        name="conv_sample",
    )(x, st, g, w_bf, cw, cb, wb_bf)


def _mem_gate(xn, attn, acc_in, w_ref, wb_ref):
    gm = _dot(xn, w_ref[:, D_MODEL:2 * D_MODEL])
    y = attn * _silu(gm)
    mm = _dot(xn, w_ref[:, 2 * D_MODEL:3 * D_MODEL])
    return acc_in + _sigmoid(mm) * _dot(y.astype(_BF16), wb_ref[...])


def _mem_prompt_kernel(x_ref, k_ref, v_ref, acc_in_ref, g_ref, w_ref, wb_ref, acc_ref):
    xn = _rmsnorm(x_ref[...], g_ref[...]).astype(_BF16)
    q = _dot(xn, w_ref[:, 0:D_MODEL]) * (MEM_HEAD_DIM ** -0.5)
    heads = []
    for hh in range(MEM_HEADS):
        sl = slice(hh * MEM_HEAD_DIM, (hh + 1) * MEM_HEAD_DIM)
        qh = q[:, sl].astype(_BF16)
        kh = k_ref[:, sl].astype(_BF16)
        vh = v_ref[:, sl].astype(_BF16)
        s = lax.dot_general(qh, kh, (((1,), (1,)), ((), ())), preferred_element_type=_F32)
        e = jnp.exp(s - jnp.max(s, axis=-1, keepdims=True))
        l = jnp.sum(e, axis=-1, keepdims=True)
        heads.append(_dot(e.astype(_BF16), vh) / l)
    attn = jnp.concatenate(heads, axis=-1)
    acc_ref[...] = _mem_gate(xn, attn, acc_in_ref[...], w_ref, wb_ref)


def _mem_sample_kernel(x_ref, k_ref, v_ref, acc_in_ref, g_ref, w_ref, wb_ref, acc_ref):
    tb = x_ref.shape[0]
    rows = tb * V7X_SUBLANES
    xn = _rmsnorm(x_ref[...].reshape(rows, D_MODEL), g_ref[...]).astype(_BF16)
    q = _dot(xn, w_ref[:, 0:D_MODEL]) * (MEM_HEAD_DIM ** -0.5)
    heads = []
    for hh in range(MEM_HEADS):
        sl = slice(hh * MEM_HEAD_DIM, (hh + 1) * MEM_HEAD_DIM)
        qh = q[:, sl].astype(_BF16).reshape(tb, V7X_SUBLANES, MEM_HEAD_DIM)
        kh = k_ref[:, :, sl].astype(_BF16)
        vh = v_ref[:, :, sl].astype(_BF16)
        s = jnp.einsum("bqd,bkd->bqk", qh, kh, preferred_element_type=_F32)
        e = jnp.exp(s - jnp.max(s, axis=-1, keepdims=True))
        l = jnp.sum(e, axis=-1, keepdims=True)
        o = jnp.einsum("bqk,bkd->bqd", e.astype(_BF16), vh, preferred_element_type=_F32) / l
        heads.append(o.reshape(rows, MEM_HEAD_DIM))
    attn = jnp.concatenate(heads, axis=-1)
    acc_ref[...] = _mem_gate(xn, attn, acc_in_ref[...], w_ref, wb_ref)


def _mem_prompt(x, mk, mv, acc_in, g, w_bf, wb_bf):
    b, t, _ = x.shape
    tm = TOKEN_TILE
    return pl.pallas_call(
        _mem_prompt_kernel,
        grid=(b, t // tm),
        in_specs=[
            pl.BlockSpec((None, tm, D_MODEL), lambda i, j: (i, j, 0)),
            pl.BlockSpec((None, N_MEM, D_MODEL), lambda i, j: (i, 0, 0)),
            pl.BlockSpec((None, N_MEM, D_MODEL), lambda i, j: (i, 0, 0)),
            pl.BlockSpec((None, tm, D_MODEL), lambda i, j: (i, j, 0)),
            _const_spec((1, D_MODEL), 2),
            _const_spec((D_MODEL, 3 * D_MODEL), 2),
            _const_spec((D_MODEL, D_MODEL), 2),
        ],
        out_specs=pl.BlockSpec((None, tm, D_MODEL), lambda i, j: (i, j, 0)),
        out_shape=jax.ShapeDtypeStruct((b, t, D_MODEL), _F32),
        compiler_params=pltpu.CompilerParams(
            dimension_semantics=("arbitrary", "arbitrary"), vmem_limit_bytes=VMEM_LIMIT),
        name="mem_prompt",
    )(x, mk, mv, acc_in, g, w_bf, wb_bf)


def _mem_sample(x, mk, mv, acc_in, g, w_bf, wb_bf):
    b, t, _ = x.shape
    tb = SAMPLE_MEM_BATCH_TILE
    return pl.pallas_call(
        _mem_sample_kernel,
        grid=(b // tb,),
        in_specs=[
            pl.BlockSpec((tb, t, D_MODEL), lambda i: (i, 0, 0)),
            pl.BlockSpec((tb, N_MEM, D_MODEL), lambda i: (i, 0, 0)),
            pl.BlockSpec((tb, N_MEM, D_MODEL), lambda i: (i, 0, 0)),
            pl.BlockSpec((tb * t, D_MODEL), lambda i: (i, 0)),
            _const_spec((1, D_MODEL), 1),
            _const_spec((D_MODEL, 3 * D_MODEL), 1),
            _const_spec((D_MODEL, D_MODEL), 1),
        ],
        out_specs=pl.BlockSpec((tb * t, D_MODEL), lambda i: (i, 0)),
        out_shape=jax.ShapeDtypeStruct((b * t, D_MODEL), _F32),
        compiler_params=pltpu.CompilerParams(
            dimension_semantics=("arbitrary",), vmem_limit_bytes=VMEM_LIMIT),
        name="mem_sample",
    )(x, mk, mv, acc_in, g, w_bf, wb_bf)


def _prep_body(x, prev_fn, g_ref, w_ref, mu_ref, w0_ref, wup_ref, a0_ref, aup_ref, kk_ref, ka_ref,
               ones_ref, outs):
    r_ref, w_out_ref, kx_ref, v_ref, kkn_ref, b_ref, sg_ref, sm_ref = outs
    xn = _rmsnorm(x, g_ref[...]).astype(_BF16)
    p = _dot(xn, w_ref[:, 0:D_SHIFT])
    ps = p + mu_ref[...] * (prev_fn(p) - p)
    k = ps[:, D_MODEL:2 * D_MODEL]
    lora_in = ps[:, 3 * D_MODEL:D_SHIFT]
    z = w0_ref[...] + _dot(jnp.tanh(lora_in).astype(_BF16), wup_ref[...])
    softplus = jnp.maximum(-z, 0.0) + jnp.log(1.0 + jnp.exp(-jnp.abs(z)))
    w_out_ref[...] = jnp.exp(-jnp.exp(-softplus - 0.5))
    a = _sigmoid(a0_ref[...] + _dot(lora_in.astype(_BF16), aup_ref[...]))
    kk = k * kk_ref[...]
    kk = kk * lax.rsqrt(_seg_sum(kk * kk, ones_ref[...]) + KK_EPS)
    r_ref[...] = ps[:, 0:D_MODEL]
    kx_ref[...] = k * (1.0 + (a - 1.0) * ka_ref[...])
    v_ref[...] = ps[:, 2 * D_MODEL:3 * D_MODEL]
    kkn_ref[...] = kk
    b_ref[...] = kk * a
    sg_ref[...] = _silu(_dot(xn, w_ref[:, D_SHIFT:D_SHIFT + D_MODEL]))
    sm_ref[...] = _sigmoid(_dot(xn, w_ref[:, D_SHIFT + D_MODEL:D_SHIFT + 2 * D_MODEL]))
    return p


def _prep_prompt_kernel(x_ref, g_ref, w_ref, mu_ref, w0_ref, wup_ref, a0_ref, aup_ref, kk_ref, ka_ref,
                        ones_ref, *rest):
    outs, tail_ref, carry_ref = rest[:8], rest[8], rest[9]

    @pl.when(pl.program_id(1) == 0)
    def _():
        carry_ref[...] = jnp.zeros_like(carry_ref)

    carry = carry_ref[...]
    p = _prep_body(x_ref[...], lambda p: _prev_rows_carry(p, carry, 1), g_ref, w_ref, mu_ref, w0_ref,
                   wup_ref, a0_ref, aup_ref, kk_ref, ka_ref, ones_ref, outs)
    last = p[p.shape[0] - V7X_SUBLANES:]
    carry_ref[...] = last
    tail_ref[...] = last


def _prep_sample_kernel(x_ref, st_ref, g_ref, w_ref, mu_ref, w0_ref, wup_ref, a0_ref, aup_ref, kk_ref,
                        ka_ref, ones_ref, *rest):
    outs, p_ref = rest[:8], rest[8]
    tb = x_ref.shape[0]
    rows = tb * V7X_SUBLANES
    x = x_ref[...].reshape(rows, D_MODEL)
    st = _bcast_state(st_ref[...], rows)
    t = lax.broadcasted_iota(jnp.int32, (rows, D_SHIFT), 0) % V7X_SUBLANES
    p = _prep_body(x, lambda p: jnp.where(t == 0, st, pltpu.roll(p, 1, axis=0)), g_ref, w_ref, mu_ref,
                   w0_ref, wup_ref, a0_ref, aup_ref, kk_ref, ka_ref, ones_ref, outs)
    p_ref[...] = p


def _prep_weight_specs(ngrid):
    return [
        _const_spec((1, D_MODEL), ngrid),
        _const_spec((D_MODEL, D_SHIFT + 2 * D_MODEL), ngrid),
        _const_spec((1, D_SHIFT), ngrid),
        _const_spec((1, D_MODEL), ngrid),
        _const_spec((2 * LORA, D_MODEL), ngrid),
        _const_spec((1, D_MODEL), ngrid),
        _const_spec((2 * LORA, D_MODEL), ngrid),
        _const_spec((1, D_MODEL), ngrid),
        _const_spec((1, D_MODEL), ngrid),
        _const_spec((SEG_TILE, SEG_TILE), ngrid),
    ]


def _prep_prompt(x, weights):
    b, t, _ = x.shape
    tm = TOKEN_TILE
    nt = t // tm
    tok = pl.BlockSpec((None, tm, D_MODEL), lambda i, j: (i, j, 0))
    return pl.pallas_call(
        _prep_prompt_kernel,
        grid=(b, nt),
        in_specs=[tok] + _prep_weight_specs(2),
        out_specs=[tok] * 8 + [
            pl.BlockSpec((None, None, V7X_SUBLANES, D_SHIFT), lambda i, j: (i, j, 0, 0))],
        out_shape=[jax.ShapeDtypeStruct((b, t, D_MODEL), _F32)] * 8 + [
            jax.ShapeDtypeStruct((b, nt, V7X_SUBLANES, D_SHIFT), _F32)],
        scratch_shapes=[pltpu.VMEM((V7X_SUBLANES, D_SHIFT), _F32)],
        compiler_params=pltpu.CompilerParams(
            dimension_semantics=("arbitrary", "arbitrary"), vmem_limit_bytes=VMEM_LIMIT),
        name="prep_prompt",
    )(x, *weights)


def _prep_sample(x, st, weights):
    b, t, _ = x.shape
    tb = TOKEN_TILE // t
    tok = pl.BlockSpec((tb * t, D_MODEL), lambda i: (i, 0))
    return pl.pallas_call(
        _prep_sample_kernel,
        grid=(b // tb,),
        in_specs=[
            pl.BlockSpec((tb, t, D_MODEL), lambda i: (i, 0, 0)),
            pl.BlockSpec((tb, 1, D_SHIFT), lambda i: (i, 0, 0)),
        ] + _prep_weight_specs(1),
        out_specs=[tok] * 8 + [pl.BlockSpec((tb * t, D_SHIFT), lambda i: (i, 0))],
        out_shape=[jax.ShapeDtypeStruct((b * t, D_MODEL), _F32)] * 8 + [
            jax.ShapeDtypeStruct((b * t, D_SHIFT), _F32)],
        compiler_params=pltpu.CompilerParams(
            dimension_semantics=("arbitrary",), vmem_limit_bytes=VMEM_LIMIT),
        name="prep_sample",
    )(x, st, *weights)


def _scan_kernel(r_ref, w_ref, k_ref, v_ref, kk_ref, b_ref, *rest, zero_init):
    if zero_init:
        o_ref, sT_ref, s_ref = rest
    else:
        s0_ref, o_ref, sT_ref, s_ref = rest
    nk = RWKV_HEAD

    @pl.when(pl.program_id(1) == 0)
    def _():
        if zero_init:
            s_ref[...] = jnp.zeros_like(s_ref)
        else:
            s_ref[...] = s0_ref[...]

    def row(ref, t, k):
        return ref[t, pl.ds(k, 1), :]

    def step(t, carry):
        def p1(k, acc):
            return acc + s_ref[k] * row(kk_ref, t, k)

        sa = lax.fori_loop(0, nk, p1, jnp.zeros((RWKV_HEAD, V7X_LANES), _F32), unroll=8)
        vv = v_ref[t]

        def p2(k, o):
            s = s_ref[k] * row(w_ref, t, k) - sa * row(b_ref, t, k) + vv * row(k_ref, t, k)
            s_ref[k] = s
            return o + s * row(r_ref, t, k)

        o_ref[t] = lax.fori_loop(0, nk, p2, jnp.zeros((RWKV_HEAD, V7X_LANES), _F32), unroll=8)
        return carry

    lax.fori_loop(0, r_ref.shape[0], step, 0)

    @pl.when(pl.program_id(1) == pl.num_programs(1) - 1)
    def _():
        sT_ref[...] = s_ref[...]


def _scan(ops, s0):
    t, _, lanes = ops[0].shape
    tt = min(SCAN_TIME_TILE, t)
    op_spec = pl.BlockSpec((tt, RWKV_HEAD, V7X_LANES), lambda g, j: (j, 0, g))
    st_spec = pl.BlockSpec((RWKV_HEAD, RWKV_HEAD, V7X_LANES), lambda g, j: (0, 0, g))
    zero_init = s0 is None
    args = list(ops) + ([] if zero_init else [s0])
    return pl.pallas_call(
        functools.partial(_scan_kernel, zero_init=zero_init),
        grid=(lanes // V7X_LANES, t // tt),
        in_specs=[op_spec] * 6 + ([] if zero_init else [st_spec]),
        out_specs=[op_spec, st_spec],
        out_shape=[
            jax.ShapeDtypeStruct((t, RWKV_HEAD, lanes), _F32),
            jax.ShapeDtypeStruct((RWKV_HEAD, RWKV_HEAD, lanes), _F32),
        ],
        scratch_shapes=[pltpu.VMEM((RWKV_HEAD, RWKV_HEAD, V7X_LANES), _F32)],
        compiler_params=pltpu.CompilerParams(
            dimension_semantics=("arbitrary", "arbitrary"), vmem_limit_bytes=VMEM_LIMIT),
        name="wkv_scan",
    )(*args)


def _to_scan_layout(x, b, t):
    return x.reshape(b, t, RWKV_HEADS, RWKV_HEAD).transpose(1, 3, 0, 2).reshape(t, RWKV_HEAD, b * RWKV_HEADS)


def _from_scan_layout(o, b, t):
    return o.reshape(t, RWKV_HEAD, b, RWKV_HEADS).transpose(2, 0, 3, 1).reshape(b * t, D_MODEL)


def _out_kernel(o_ref, r_ref, kx_ref, v_ref, sg_ref, sm_ref, acc_ref, x_ref, lng_ref, lnb_ref, rk_ref,
                wb_ref, wo_ref, nf_ref, ones_ref, y_ref):
    ones = ones_ref[...]
    o = o_ref[...]
    inv_n = 1.0 / RWKV_HEAD
    d = o - _seg_sum(o, ones) * inv_n
    var = _seg_sum(d * d, ones) * inv_n
    on = d * lax.rsqrt(var + GN_EPS) * lng_ref[...] + lnb_ref[...]
    bonus = _seg_sum(r_ref[...] * kx_ref[...] * rk_ref[...], ones) * v_ref[...]
    y = (on + bonus) * sg_ref[...]
    merged = acc_ref[...] + sm_ref[...] * _dot(y.astype(_BF16), wb_ref[...])
    h = x_ref[...] + _dot(merged.astype(_BF16), wo_ref[...])
    y_ref[...] = _rmsnorm(h, nf_ref[...])


def _out(o, r, kx, v, sg, sm, acc, x2d, lng, lnb, rk, wb_bf, wo_bf, nf, ones_bd):
    n = x2d.shape[0]
    tm = TOKEN_TILE
    tok = pl.BlockSpec((tm, D_MODEL), lambda i: (i, 0))
    vec = _const_spec((1, D_MODEL), 1)
    mat = _const_spec((D_MODEL, D_MODEL), 1)
    return pl.pallas_call(
        _out_kernel,
        grid=(n // tm,),
        in_specs=[tok] * 8 + [vec, vec, vec, mat, mat, vec, _const_spec((SEG_TILE, SEG_TILE), 1)],
        out_specs=tok,
        out_shape=jax.ShapeDtypeStruct((n, D_MODEL), _F32),
        compiler_params=pltpu.CompilerParams(
            dimension_semantics=("arbitrary",), vmem_limit_bytes=VMEM_LIMIT),
        name="merge_out",
    )(o, r, kx, v, sg, sm, acc, x2d, lng, lnb, rk, wb_bf, wo_bf, nf, ones_bd)


def kernel(x_prompt, x_sample, mem_prompt, cache_mem_k, cache_mem_v, state_conv, state_shift, state_wkv, norm_in, w_in, conv_w, conv_b, shift_mu, decay_w0, decay_up, icl_a0, icl_up, k_k, k_a, r_k, ln_x_g, ln_x_b, norm_mem, w_mem_kv, w_branch, w_out, norm_final):
    assert norm_in.shape[0] == 1, "single-layer step"
    bp, tp, _ = x_prompt.shape
    bs, ts, _ = x_sample.shape
    assert ts == V7X_SUBLANES and tp % TOKEN_TILE == 0 and (bs * ts) % TOKEN_TILE == 0

    row = lambda a: a.reshape(1, -1)
    w = w_in[0]
    cols = lambda lo, n: w[:, lo:lo + n]
    w_conv = jnp.concatenate([cols(_OFF_CONV, 4 * D_MODEL), cols(_OFF_MCONV, D_MODEL)], axis=1).astype(_BF16)
    w_mem = jnp.concatenate([cols(_OFF_QMEM, 2 * D_MODEL), cols(_OFF_MMEM, D_MODEL)], axis=1).astype(_BF16)
    w_rwkv = jnp.concatenate([cols(_OFF_SHIFT, D_SHIFT + D_MODEL), cols(_OFF_MRWKV, D_MODEL)], axis=1).astype(_BF16)
    wb = w_branch[0].astype(_BF16)
    wo = w_out[0].astype(_BF16)
    zpad = jnp.zeros((LORA, D_MODEL), _F32)
    wup = jnp.concatenate([decay_up[0], zpad], axis=0).astype(_BF16)
    aup = jnp.concatenate([zpad, icl_up[0]], axis=0).astype(_BF16)
    ones_bd = jnp.kron(jnp.eye(SEG_TILE // RWKV_HEAD, dtype=_F32),
                       jnp.ones((RWKV_HEAD, RWKV_HEAD), _F32)).astype(_BF16)
    g_in = row(norm_in[0])
    prep_w = (g_in, w_rwkv, row(shift_mu[0]), row(decay_w0[0]), wup, row(icl_a0[0]), aup, row(k_k[0]),
              row(k_a[0]), ones_bd)
    out_w = (row(ln_x_g[0]), row(ln_x_b[0]), row(r_k[0]), wb[1], wo, row(norm_final), ones_bd)

    mk, mv = _mem_kv(mem_prompt.reshape(bp * N_MEM, D_MODEL), row(norm_mem[0]), w_mem_kv[0].astype(_BF16))
    acc, conv_tail = _conv_prompt(x_prompt, g_in, w_conv, conv_w[0], row(conv_b[0]), wb[0])
    acc = _mem_prompt(x_prompt, mk.reshape(bp, N_MEM, D_MODEL), mv.reshape(bp, N_MEM, D_MODEL), acc, g_in,
                      w_mem, wb[2])
    *ops, sg, sm, shift_tail = _prep_prompt(x_prompt, prep_w)
    flat = lambda a: a.reshape(bp * tp, D_MODEL)
    o, s_fin = _scan([_to_scan_layout(flat(a), bp, tp) for a in ops], None)
    y_prompt = _out(_from_scan_layout(o, bp, tp), flat(ops[0]), flat(ops[2]), flat(ops[3]), flat(sg), flat(sm),
                    flat(acc), flat(x_prompt), *out_w).reshape(bp, tp, D_MODEL)
    kv_shape = (1, bp, N_MEM, MEM_HEADS, MEM_HEAD_DIM)
    conv_p = conv_tail[:, -1, V7X_SUBLANES - 2:, :][None]
    shift_p = shift_tail[:, -1, V7X_SUBLANES - 1, :][None]
    wkv_p = s_fin.reshape(RWKV_HEAD, RWKV_HEAD, bp, RWKV_HEADS).transpose(2, 3, 1, 0)[None]

    acc_s, u_s = _conv_sample(x_sample, state_conv[0], g_in, w_conv, conv_w[0], row(conv_b[0]), wb[0])
    acc_s = _mem_sample(x_sample, cache_mem_k[0].reshape(bs, N_MEM, D_MODEL),
                        cache_mem_v[0].reshape(bs, N_MEM, D_MODEL), acc_s, g_in, w_mem, wb[2])
    *ops_s, sg_s, sm_s, p_s = _prep_sample(x_sample, state_shift[0].reshape(bs, 1, D_SHIFT), prep_w)
    s0 = state_wkv[0].transpose(3, 2, 0, 1).reshape(RWKV_HEAD, RWKV_HEAD, bs * RWKV_HEADS)
    o_s, s_fin_s = _scan([_to_scan_layout(a, bs, ts) for a in ops_s], s0)
    y_sample = _out(_from_scan_layout(o_s, bs, ts), ops_s[0], ops_s[2], ops_s[3], sg_s, sm_s, acc_s,
                    x_sample.reshape(bs * ts, D_MODEL), *out_w).reshape(bs, ts, D_MODEL)
    conv_s = u_s.reshape(bs, ts, D_MODEL)[:, ts - 2:, :][None]
    shift_s = p_s.reshape(bs, ts, D_SHIFT)[:, ts - 1, :][None]
    wkv_s = s_fin_s.reshape(RWKV_HEAD, RWKV_HEAD, bs, RWKV_HEADS).transpose(2, 3, 1, 0)[None]

    return (y_prompt, y_sample, mk.reshape(kv_shape), mv.reshape(kv_shape), conv_p, shift_p, wkv_p,
            conv_s, shift_s, wkv_s)
```

```python
import functools

import jax
import jax.numpy as jnp
from jax import lax
from jax.experimental import pallas as pl
from jax.experimental.pallas import tpu as pltpu

D_MODEL = 1024
N_MEM = 256
RWKV_HEAD = 64
RWKV_HEADS = D_MODEL // RWKV_HEAD
LORA = 64
MEM_HEADS = 4
MEM_HEAD_DIM = D_MODEL // MEM_HEADS
D_SHIFT = 3 * D_MODEL + 2 * LORA
CONV_TAPS = 3
NORM_EPS = 1e-6
GN_EPS = RWKV_HEAD * 1e-5
KK_EPS = 1e-12

_OFF_CONV = 0
_OFF_SHIFT = 4 * D_MODEL
_OFF_GRWKV = _OFF_SHIFT + D_SHIFT
_OFF_QMEM = _OFF_GRWKV + D_MODEL
_OFF_GMEM = _OFF_QMEM + D_MODEL
_OFF_MCONV = _OFF_GMEM + D_MODEL
_OFF_MRWKV = _OFF_MCONV + D_MODEL
_OFF_MMEM = _OFF_MRWKV + D_MODEL

V7X_SUBLANES = 8
V7X_LANES = 128
GROUP = V7X_SUBLANES
HEAD_PAIRS = D_MODEL // V7X_LANES
SEG_TILE = 256
TOKEN_TILE = 256
PROMPT_TIME_TILE = TOKEN_TILE // GROUP
SAMPLE_MEM_BATCH_TILE = 8
SCAN_UNROLL = 32
VMEM_LIMIT = 56 * 1024 * 1024

_F32 = jnp.float32
_BF16 = jnp.bfloat16


def _dot(a, b):
    return jnp.dot(a, b, preferred_element_type=_F32)


def _rmsnorm(x, g):
    ms = jnp.mean(x * x, axis=-1, keepdims=True)
    return x * lax.rsqrt(ms + NORM_EPS) * g


def _sigmoid(x):
    return 1.0 / (1.0 + jnp.exp(-x))


def _silu(x):
    return x * _sigmoid(x)


def _seg_sum(x, ones_bd):
    outs = []
    for c in range(D_MODEL // SEG_TILE):
        xc = x[:, c * SEG_TILE:(c + 1) * SEG_TILE]
        hi = xc.astype(_BF16)
        lo = (xc - hi.astype(_F32)).astype(_BF16)
        outs.append(_dot(hi, ones_bd) + _dot(lo, ones_bd))
    return jnp.concatenate(outs, axis=-1)


def _bcast_rows(st, nt):
    nb, _, c = st.shape
    return jnp.broadcast_to(st, (nb, nt, c)).reshape(nb * nt, c)


def _time_index(nb, nt, c):
    return lax.broadcasted_iota(jnp.int32, (nb * nt, c), 0) % nt


def _slab_start(bl, nt):
    return (bl // GROUP) * (nt * GROUP) + bl % GROUP


def _store_slab(ref, val, nb, nt):
    for bl in range(nb):
        rows = val[bl * nt:(bl + 1) * nt]
        for hp in range(HEAD_PAIRS):
            ref[hp, pl.ds(_slab_start(bl, nt), nt, stride=GROUP), :] = (
                rows[:, hp * V7X_LANES:(hp + 1) * V7X_LANES])


def _load_slab(ref, nb, nt):
    per_batch = []
    for bl in range(nb):
        per_batch.append(jnp.concatenate(
            [ref[hp, pl.ds(_slab_start(bl, nt), nt, stride=GROUP), :] for hp in range(HEAD_PAIRS)],
            axis=-1))
    return jnp.concatenate(per_batch, axis=0)


def _const_spec(shape, ngrid):
    zeros = (0,) * len(shape)
    if ngrid == 1:
        return pl.BlockSpec(shape, lambda i: zeros, pipeline_mode=pl.Buffered(1))
    return pl.BlockSpec(shape, lambda i, j: zeros, pipeline_mode=pl.Buffered(1))


def _tile_dims(b, t):
    if t >= PROMPT_TIME_TILE:
        nb, nt = GROUP, PROMPT_TIME_TILE
    else:
        nb, nt = TOKEN_TILE // t, t
    assert b % nb == 0 and t % nt == 0 and (nb == GROUP or nt == t)
    return nb, nt, (b // nb, t // nt)


def _params():
    return pltpu.CompilerParams(dimension_semantics=("arbitrary", "arbitrary"), vmem_limit_bytes=VMEM_LIMIT)


def _mem_kv_kernel(mem_ref, g_ref, w_ref, k_ref, v_ref):
    xn = _rmsnorm(mem_ref[...], g_ref[...]).astype(_BF16)
    k_ref[...] = _dot(xn, w_ref[:, 0:D_MODEL])
    v_ref[...] = _dot(xn, w_ref[:, D_MODEL:2 * D_MODEL])


def _mem_kv(mem2d, g, w_bf):
    n = mem2d.shape[0]
    tm = TOKEN_TILE
    return pl.pallas_call(
        _mem_kv_kernel,
        grid=(n // tm,),
        in_specs=[
            pl.BlockSpec((tm, D_MODEL), lambda i: (i, 0)),
            pl.BlockSpec((1, D_MODEL), lambda i: (0, 0)),
            pl.BlockSpec((D_MODEL, 2 * D_MODEL), lambda i: (0, 0)),
        ],
        out_specs=[
            pl.BlockSpec((tm, D_MODEL), lambda i: (i, 0)),
            pl.BlockSpec((tm, D_MODEL), lambda i: (i, 0)),
        ],
        out_shape=[jax.ShapeDtypeStruct((n, D_MODEL), _F32)] * 2,
        compiler_params=pltpu.CompilerParams(
            dimension_semantics=("arbitrary",), vmem_limit_bytes=VMEM_LIMIT),
        name="mem_kv",
    )(mem2d, g, w_bf)


def _conv_kernel(x_ref, *rest, has_state):
    if has_state:
        st_ref, g_ref, w_ref, cw_ref, cb_ref, wb_ref, acc_ref, new_ref = rest
    else:
        g_ref, w_ref, cw_ref, cb_ref, wb_ref, acc_ref, new_ref, st_ref = rest

        @pl.when(pl.program_id(1) == 0)
        def _():
            st_ref[...] = jnp.zeros_like(st_ref)

    nb, nt, _ = x_ref.shape
    rows = nb * nt
    xn = _rmsnorm(x_ref[...].reshape(rows, D_MODEL), g_ref[...]).astype(_BF16)
    h = _dot(xn, w_ref[:, 0:D_MODEL])
    cg = _dot(xn, w_ref[:, 2 * D_MODEL:3 * D_MODEL])
    u = cg * h
    s0 = _bcast_rows(st_ref[:, 0:1, :], nt)
    s1 = _bcast_rows(st_ref[:, 1:2, :], nt)
    t = _time_index(nb, nt, D_MODEL)
    prev1 = jnp.where(t == 0, s1, pltpu.roll(u, 1, axis=0))
    prev2 = jnp.where(t == 0, s0, jnp.where(t == 1, s1, pltpu.roll(u, 2, axis=0)))
    conv = cb_ref[...] + prev2 * cw_ref[0:1, :] + prev1 * cw_ref[1:2, :] + u * cw_ref[2:3, :]
    bg = _dot(xn, w_ref[:, D_MODEL:2 * D_MODEL])
    gc = _dot(xn, w_ref[:, 3 * D_MODEL:4 * D_MODEL])
    y = bg * conv * _silu(gc)
    mc = _dot(xn, w_ref[:, 4 * D_MODEL:5 * D_MODEL])
    acc_ref[...] = (_sigmoid(mc) * _dot(y.astype(_BF16), wb_ref[...])).reshape(nb, nt, D_MODEL)
    new = u.reshape(nb, nt, D_MODEL)[:, nt - (CONV_TAPS - 1):, :]
    new_ref[...] = new
    if not has_state:
        st_ref[...] = new


def _conv(x, st, g, w_bf, cw, cb, wb_bf):
    b, t, _ = x.shape
    nb, nt, grid = _tile_dims(b, t)
    has_state = st is not None
    tok = pl.BlockSpec((nb, nt, D_MODEL), lambda i, j: (i, j, 0))
    st_spec = pl.BlockSpec((nb, CONV_TAPS - 1, D_MODEL), lambda i, j: (i, 0, 0))
    return pl.pallas_call(
        functools.partial(_conv_kernel, has_state=has_state),
        grid=grid,
        in_specs=[tok] + ([st_spec] if has_state else []) + [
            _const_spec((1, D_MODEL), 2),
            _const_spec((D_MODEL, 5 * D_MODEL), 2),
            _const_spec((CONV_TAPS, D_MODEL), 2),
            _const_spec((1, D_MODEL), 2),
            _const_spec((D_MODEL, D_MODEL), 2),
        ],
        out_specs=[tok, st_spec],
        out_shape=[
            jax.ShapeDtypeStruct((b, t, D_MODEL), _F32),
            jax.ShapeDtypeStruct((b, CONV_TAPS - 1, D_MODEL), _F32),
        ],
        scratch_shapes=[] if has_state else [pltpu.VMEM((nb, CONV_TAPS - 1, D_MODEL), _F32)],
        compiler_params=_params(),
        name="conv_branch",
    )(*([x] + ([st] if has_state else []) + [g, w_bf, cw, cb, wb_bf]))


def _mem_gate(xn, attn, acc_in, w_ref, wb_ref):
    gm = _dot(xn, w_ref[:, D_MODEL:2 * D_MODEL])
    y = attn * _silu(gm)
    mm = _dot(xn, w_ref[:, 2 * D_MODEL:3 * D_MODEL])
    return acc_in + _sigmoid(mm) * _dot(y.astype(_BF16), wb_ref[...])


def _mem_prompt_kernel(x_ref, k_ref, v_ref, acc_in_ref, g_ref, w_ref, wb_ref, acc_ref):
    xn = _rmsnorm(x_ref[...], g_ref[...]).astype(_BF16)
    q = _dot(xn, w_ref[:, 0:D_MODEL]) * (MEM_HEAD_DIM ** -0.5)
    heads = []
    for hh in range(MEM_HEADS):
        sl = slice(hh * MEM_HEAD_DIM, (hh + 1) * MEM_HEAD_DIM)
        qh = q[:, sl].astype(_BF16)
        kh = k_ref[:, sl].astype(_BF16)
        vh = v_ref[:, sl].astype(_BF16)
        s = lax.dot_general(qh, kh, (((1,), (1,)), ((), ())), preferred_element_type=_F32)
        e = jnp.exp(s - jnp.max(s, axis=-1, keepdims=True))
        l = jnp.sum(e, axis=-1, keepdims=True)
        heads.append(_dot(e.astype(_BF16), vh) / l)
    attn = jnp.concatenate(heads, axis=-1)
    acc_ref[...] = _mem_gate(xn, attn, acc_in_ref[...], w_ref, wb_ref)


def _mem_sample_kernel(x_ref, k_ref, v_ref, acc_in_ref, g_ref, w_ref, wb_ref, acc_ref):
    tb, nt, _ = x_ref.shape
    rows = tb * nt
    xn = _rmsnorm(x_ref[...].reshape(rows, D_MODEL), g_ref[...]).astype(_BF16)
    q = (_dot(xn, w_ref[:, 0:D_MODEL]) * (MEM_HEAD_DIM ** -0.5)).reshape(tb, nt, D_MODEL)
    heads = []
    for hh in range(MEM_HEADS):
        sl = slice(hh * MEM_HEAD_DIM, (hh + 1) * MEM_HEAD_DIM)
        qh = q[:, :, sl].astype(_BF16)
        kh = k_ref[:, :, sl].astype(_BF16)
        vh = v_ref[:, :, sl].astype(_BF16)
        s = jnp.einsum("bqd,bkd->bqk", qh, kh, preferred_element_type=_F32)
        e = jnp.exp(s - jnp.max(s, axis=-1, keepdims=True))
        l = jnp.sum(e, axis=-1, keepdims=True)
        o = jnp.einsum("bqk,bkd->bqd", e.astype(_BF16), vh, preferred_element_type=_F32) / l
        heads.append(o.reshape(rows, MEM_HEAD_DIM))
    attn = jnp.concatenate(heads, axis=-1)
    acc = _mem_gate(xn, attn, acc_in_ref[...].reshape(rows, D_MODEL), w_ref, wb_ref)
    acc_ref[...] = acc.reshape(tb, nt, D_MODEL)


def _mem_prompt(x, mk, mv, acc_in, g, w_bf, wb_bf):
    b, t, _ = x.shape
    tm = TOKEN_TILE
    tok = pl.BlockSpec((None, tm, D_MODEL), lambda i, j: (i, j, 0))
    kv = pl.BlockSpec((None, N_MEM, D_MODEL), lambda i, j: (i, 0, 0))
    return pl.pallas_call(
        _mem_prompt_kernel,
        grid=(b, t // tm),
        in_specs=[tok, kv, kv, tok, _const_spec((1, D_MODEL), 2), _const_spec((D_MODEL, 3 * D_MODEL), 2),
                  _const_spec((D_MODEL, D_MODEL), 2)],
        out_specs=tok,
        out_shape=jax.ShapeDtypeStruct((b, t, D_MODEL), _F32),
        compiler_params=_params(),
        name="mem_prompt",
    )(x, mk, mv, acc_in, g, w_bf, wb_bf)


def _mem_sample(x, mk, mv, acc_in, g, w_bf, wb_bf):
    b, t, _ = x.shape
    tb = SAMPLE_MEM_BATCH_TILE
    tok = pl.BlockSpec((tb, t, D_MODEL), lambda i: (i, 0, 0))
    kv = pl.BlockSpec((tb, N_MEM, D_MODEL), lambda i: (i, 0, 0))
    return pl.pallas_call(
        _mem_sample_kernel,
        grid=(b // tb,),
        in_specs=[tok, kv, kv, tok, _const_spec((1, D_MODEL), 1), _const_spec((D_MODEL, 3 * D_MODEL), 1),
                  _const_spec((D_MODEL, D_MODEL), 1)],
        out_specs=tok,
        out_shape=jax.ShapeDtypeStruct((b, t, D_MODEL), _F32),
        compiler_params=pltpu.CompilerParams(
            dimension_semantics=("arbitrary",), vmem_limit_bytes=VMEM_LIMIT),
        name="mem_sample",
    )(x, mk, mv, acc_in, g, w_bf, wb_bf)


def _prep_kernel(x_ref, *rest, has_state):
    if has_state:
        st_ref, rest = rest[0], rest[1:]
    (g_ref, w_ref, mu_ref, w0_ref, wup_ref, a0_ref, aup_ref, kk_ref, ka_ref, rk_ref, ones_ref,
     r_ref, wd_ref, kx_ref, v_ref, kkn_ref, b_ref, bonus_ref, sg_ref, sm_ref, new_ref) = rest[:21]
    if not has_state:
        st_ref = rest[21]

        @pl.when(pl.program_id(1) == 0)
        def _():
            st_ref[...] = jnp.zeros_like(st_ref)

    nb, nt, _ = x_ref.shape
    rows = nb * nt
    tok = lambda a: a.reshape(nb, nt, D_MODEL)
    xn = _rmsnorm(x_ref[...].reshape(rows, D_MODEL), g_ref[...]).astype(_BF16)
    p = _dot(xn, w_ref[:, 0:D_SHIFT])
    prev = jnp.where(_time_index(nb, nt, D_SHIFT) == 0, _bcast_rows(st_ref[...], nt), pltpu.roll(p, 1, axis=0))
    new = p.reshape(nb, nt, D_SHIFT)[:, nt - 1:, :]
    new_ref[...] = new
    if not has_state:
        st_ref[...] = new
    ps = p + mu_ref[...] * (prev - p)
    r = ps[:, 0:D_MODEL]
    k = ps[:, D_MODEL:2 * D_MODEL]
    v = ps[:, 2 * D_MODEL:3 * D_MODEL]
    lora_in = ps[:, 3 * D_MODEL:D_SHIFT]
    z = w0_ref[...] + _dot(jnp.tanh(lora_in).astype(_BF16), wup_ref[...])
    softplus = jnp.maximum(-z, 0.0) + jnp.log(1.0 + jnp.exp(-jnp.abs(z)))
    _store_slab(wd_ref, jnp.exp(-jnp.exp(-softplus - 0.5)), nb, nt)
    a = _sigmoid(a0_ref[...] + _dot(lora_in.astype(_BF16), aup_ref[...]))
    ones = ones_ref[...]
    kk = k * kk_ref[...]
    kk = kk * lax.rsqrt(_seg_sum(kk * kk, ones) + KK_EPS)
    kx = k * (1.0 + (a - 1.0) * ka_ref[...])
    _store_slab(r_ref, r, nb, nt)
    _store_slab(kx_ref, kx, nb, nt)
    _store_slab(v_ref, v, nb, nt)
    _store_slab(kkn_ref, kk, nb, nt)
    _store_slab(b_ref, kk * a, nb, nt)
    bonus_ref[...] = tok(_seg_sum(r * kx * rk_ref[...], ones) * v)
    sg_ref[...] = tok(_silu(_dot(xn, w_ref[:, D_SHIFT:D_SHIFT + D_MODEL])))
    sm_ref[...] = tok(_sigmoid(_dot(xn, w_ref[:, D_SHIFT + D_MODEL:D_SHIFT + 2 * D_MODEL])))


def _slab_spec(nb, nt, grid_t):
    return pl.BlockSpec((HEAD_PAIRS, nb * nt, V7X_LANES), lambda i, j: (0, i * grid_t + j, 0))


def _prep(x, st, weights):
    b, t, _ = x.shape
    nb, nt, grid = _tile_dims(b, t)
    has_state = st is not None
    tok = pl.BlockSpec((nb, nt, D_MODEL), lambda i, j: (i, j, 0))
    st_spec = pl.BlockSpec((nb, 1, D_SHIFT), lambda i, j: (i, 0, 0))
    slab = _slab_spec(nb, nt, grid[1])
    vec = _const_spec((1, D_MODEL), 2)
    w_specs = [
        vec,
        _const_spec((D_MODEL, D_SHIFT + 2 * D_MODEL), 2),
        _const_spec((1, D_SHIFT), 2),
        vec,
        _const_spec((2 * LORA, D_MODEL), 2),
        vec,
        _const_spec((2 * LORA, D_MODEL), 2),
        vec, vec, vec,
        _const_spec((SEG_TILE, SEG_TILE), 2),
    ]
    slab_shape = jax.ShapeDtypeStruct((HEAD_PAIRS, b * t, V7X_LANES), _F32)
    tok_shape = jax.ShapeDtypeStruct((b, t, D_MODEL), _F32)
    return pl.pallas_call(
        functools.partial(_prep_kernel, has_state=has_state),
        grid=grid,
        in_specs=[tok] + ([st_spec] if has_state else []) + w_specs,
        out_specs=[slab] * 6 + [tok] * 3 + [st_spec],
        out_shape=[slab_shape] * 6 + [tok_shape] * 3 + [jax.ShapeDtypeStruct((b, 1, D_SHIFT), _F32)],
        scratch_shapes=[] if has_state else [pltpu.VMEM((nb, 1, D_SHIFT), _F32)],
        compiler_params=_params(),
        name="rwkv_prep",
    )(*([x] + ([st] if has_state else []) + list(weights)))


def _scan_kernel(r_ref, w_ref, k_ref, v_ref, kk_ref, b_ref, *rest, zero_init):
    if zero_init:
        o_ref, sT_ref, s_ref, yt_ref, ot_ref = rest
    else:
        s0_ref, o_ref, sT_ref, s_ref, yt_ref, ot_ref = rest
    nk = RWKV_HEAD
    tt = yt_ref.shape[1]
    half = V7X_LANES // 2
    low = lax.broadcasted_iota(jnp.int32, (nk, V7X_LANES), 1) < half

    @pl.when(pl.program_id(1) == 0)
    def _():
        if zero_init:
            s_ref[...] = jnp.zeros_like(s_ref)
        else:
            s_ref[...] = s0_ref[...]

    def tile_rows(i, tau):
        return pl.ds(pl.multiple_of((2 * i + tau) * GROUP, GROUP), GROUP)

    def to_lanes(i2, carry):
        for i in (2 * i2, 2 * i2 + 1):
            for n, ref in enumerate((r_ref, w_ref, k_ref, v_ref, kk_ref, b_ref)):
                d = jnp.concatenate(
                    [ref[hp, tile_rows(i, tau), :] for tau in (0, 1) for hp in range(HEAD_PAIRS)], axis=0)
                tr = d.T
                top, bot = tr[0:nk], tr[nk:2 * nk]
                yt_ref[n, 2 * i] = jnp.where(low, top, pltpu.roll(bot, half, axis=1))
                yt_ref[n, 2 * i + 1] = jnp.where(low, pltpu.roll(top, half, axis=1), bot)
        return carry

    lax.fori_loop(0, tt // 4, to_lanes, 0)

    def row(n, t, k):
        return yt_ref[n, t, pl.ds(k, 1), :]

    def step(t, carry):
        def p1(k, acc):
            return acc + s_ref[k] * row(4, t, k)

        sa = lax.fori_loop(0, nk, p1, jnp.zeros((nk, V7X_LANES), _F32), unroll=SCAN_UNROLL)
        vv = yt_ref[3, t]

        def p2(k, o):
            s = s_ref[k] * row(1, t, k) - sa * row(5, t, k) + vv * row(2, t, k)
            s_ref[k] = s
            return o + s * row(0, t, k)

        ot_ref[t] = lax.fori_loop(0, nk, p2, jnp.zeros((nk, V7X_LANES), _F32), unroll=SCAN_UNROLL)
        return carry

    lax.fori_loop(0, tt, step, 0)

    def from_lanes(i4, carry):
        for i in (4 * i4, 4 * i4 + 1, 4 * i4 + 2, 4 * i4 + 3):
            y0, y1 = ot_ref[2 * i], ot_ref[2 * i + 1]
            top = jnp.where(low, y0, pltpu.roll(y1, half, axis=1))
            bot = jnp.where(low, pltpu.roll(y0, half, axis=1), y1)
            d = jnp.concatenate([top, bot], axis=0).T
            for tau in (0, 1):
                for hp in range(HEAD_PAIRS):
                    blk = (tau * HEAD_PAIRS + hp) * GROUP
                    o_ref[hp, tile_rows(i, tau), :] = d[blk:blk + GROUP]
        return carry

    lax.fori_loop(0, tt // 8, from_lanes, 0)

    @pl.when(pl.program_id(1) == pl.num_programs(1) - 1)
    def _():
        sT_ref[...] = s_ref[...]


def _scan(ops, s0, t):
    groups = ops[0].shape[1] // (t * GROUP)
    tt = min(PROMPT_TIME_TILE, t)
    nt = t // tt
    op_spec = pl.BlockSpec((HEAD_PAIRS, tt * GROUP, V7X_LANES), lambda g, j: (0, g * nt + j, 0))
    st_spec = pl.BlockSpec((RWKV_HEAD, RWKV_HEAD, V7X_LANES), lambda g, j: (0, 0, g))
    zero_init = s0 is None
    args = list(ops) + ([] if zero_init else [s0])
    return pl.pallas_call(
        functools.partial(_scan_kernel, zero_init=zero_init),
        grid=(groups, nt),
        in_specs=[op_spec] * 6 + ([] if zero_init else [st_spec]),
        out_specs=[op_spec, st_spec],
        out_shape=[
            jax.ShapeDtypeStruct(ops[0].shape, _F32),
            jax.ShapeDtypeStruct((RWKV_HEAD, RWKV_HEAD, groups * V7X_LANES), _F32),
        ],
        scratch_shapes=[
            pltpu.VMEM((RWKV_HEAD, RWKV_HEAD, V7X_LANES), _F32),
            pltpu.VMEM((6, tt, RWKV_HEAD, V7X_LANES), _F32),
            pltpu.VMEM((tt, RWKV_HEAD, V7X_LANES), _F32),
        ],
        compiler_params=_params(),
        name="wkv_scan",
    )(*args)


def _state_to_lanes(s):
    b = s.shape[0]
    s = s.reshape(b // GROUP, GROUP, HEAD_PAIRS, 2, RWKV_HEAD, RWKV_HEAD)
    return s.transpose(5, 4, 0, 3, 2, 1).reshape(RWKV_HEAD, RWKV_HEAD, b * RWKV_HEADS)


def _state_from_lanes(s, b):
    s = s.reshape(RWKV_HEAD, RWKV_HEAD, b // GROUP, 2, HEAD_PAIRS, GROUP)
    return s.transpose(2, 5, 4, 3, 1, 0).reshape(b, RWKV_HEADS, RWKV_HEAD, RWKV_HEAD)


def _out_kernel(o_ref, bonus_ref, sg_ref, sm_ref, acc_ref, x_ref, lng_ref, lnb_ref, wb_ref, wo_ref, nf_ref,
                ones_ref, y_ref):
    nb, nt, _ = x_ref.shape
    rows = nb * nt
    tok = lambda ref: ref[...].reshape(rows, D_MODEL)
    ones = ones_ref[...]
    o = _load_slab(o_ref, nb, nt)
    inv_n = 1.0 / RWKV_HEAD
    d = o - _seg_sum(o, ones) * inv_n
    var = _seg_sum(d * d, ones) * inv_n
    on = d * lax.rsqrt(var + GN_EPS) * lng_ref[...] + lnb_ref[...]
    y = (on + tok(bonus_ref)) * tok(sg_ref)
    merged = tok(acc_ref) + tok(sm_ref) * _dot(y.astype(_BF16), wb_ref[...])
    h = tok(x_ref) + _dot(merged.astype(_BF16), wo_ref[...])
    y_ref[...] = _rmsnorm(h, nf_ref[...]).reshape(nb, nt, D_MODEL)


def _out(o, bonus, sg, sm, acc, x, lng, lnb, wb_bf, wo_bf, nf, ones_bd):
    b, t, _ = x.shape
    nb, nt, grid = _tile_dims(b, t)
    tok = pl.BlockSpec((nb, nt, D_MODEL), lambda i, j: (i, j, 0))
    vec = _const_spec((1, D_MODEL), 2)
    mat = _const_spec((D_MODEL, D_MODEL), 2)
    return pl.pallas_call(
        _out_kernel,
        grid=grid,
        in_specs=[_slab_spec(nb, nt, grid[1])] + [tok] * 5 + [vec, vec, mat, mat, vec,
                                                             _const_spec((SEG_TILE, SEG_TILE), 2)],
        out_specs=tok,
        out_shape=jax.ShapeDtypeStruct((b, t, D_MODEL), _F32),
        compiler_params=_params(),
        name="merge_out",
    )(o, bonus, sg, sm, acc, x, lng, lnb, wb_bf, wo_bf, nf, ones_bd)


def _group(x, conv_st, shift_st, wkv_st, mk, mv, mem_fn, g_in, w_conv, w_mem, conv_w, conv_b, wb, prep_w, out_w):
    b, t, _ = x.shape
    acc, conv_new = _conv(x, conv_st, g_in, w_conv, conv_w, conv_b, wb[0])
    acc = mem_fn(x, mk, mv, acc, g_in, w_mem, wb[2])
    *ops, bonus, sg, sm, shift_new = _prep(x, shift_st, prep_w)
    s0 = None if wkv_st is None else _state_to_lanes(wkv_st)
    o, s_fin = _scan(ops, s0, t)
    y = _out(o, bonus, sg, sm, acc, x, *out_w)
    return y, conv_new[None], shift_new.reshape(b, D_SHIFT)[None], _state_from_lanes(s_fin, b)[None]


def kernel(x_prompt, x_sample, mem_prompt, cache_mem_k, cache_mem_v, state_conv, state_shift, state_wkv, norm_in, w_in, conv_w, conv_b, shift_mu, decay_w0, decay_up, icl_a0, icl_up, k_k, k_a, r_k, ln_x_g, ln_x_b, norm_mem, w_mem_kv, w_branch, w_out, norm_final):
    assert norm_in.shape[0] == 1, "single-layer step"
    bp = x_prompt.shape[0]
    bs = x_sample.shape[0]

    row = lambda a: a.reshape(1, -1)
    w = w_in[0]
    cols = lambda lo, n: w[:, lo:lo + n]
    w_conv = jnp.concatenate([cols(_OFF_CONV, 4 * D_MODEL), cols(_OFF_MCONV, D_MODEL)], axis=1).astype(_BF16)
    w_mem = jnp.concatenate([cols(_OFF_QMEM, 2 * D_MODEL), cols(_OFF_MMEM, D_MODEL)], axis=1).astype(_BF16)
    w_rwkv = jnp.concatenate([cols(_OFF_SHIFT, D_SHIFT + D_MODEL), cols(_OFF_MRWKV, D_MODEL)], axis=1).astype(_BF16)
    wb = w_branch[0].astype(_BF16)
    wo = w_out[0].astype(_BF16)
    zpad = jnp.zeros((LORA, D_MODEL), _F32)
    wup = jnp.concatenate([decay_up[0], zpad], axis=0).astype(_BF16)
    aup = jnp.concatenate([zpad, icl_up[0]], axis=0).astype(_BF16)
    ones_bd = jnp.kron(jnp.eye(SEG_TILE // RWKV_HEAD, dtype=_F32),
                       jnp.ones((RWKV_HEAD, RWKV_HEAD), _F32)).astype(_BF16)
    g_in = row(norm_in[0])
    prep_w = (g_in, w_rwkv, row(shift_mu[0]), row(decay_w0[0]), wup, row(icl_a0[0]), aup, row(k_k[0]),
              row(k_a[0]), row(r_k[0]), ones_bd)
    out_w = (row(ln_x_g[0]), row(ln_x_b[0]), wb[1], wo, row(norm_final), ones_bd)
    shared = (g_in, w_conv, w_mem, conv_w[0], row(conv_b[0]), wb, prep_w, out_w)

    mk, mv = _mem_kv(mem_prompt.reshape(bp * N_MEM, D_MODEL), row(norm_mem[0]), w_mem_kv[0].astype(_BF16))
    y_p, conv_p, shift_p, wkv_p = _group(
        x_prompt, None, None, None, mk.reshape(bp, N_MEM, D_MODEL), mv.reshape(bp, N_MEM, D_MODEL),
        _mem_prompt, *shared)
    y_s, conv_s, shift_s, wkv_s = _group(
        x_sample, state_conv[0], state_shift[0].reshape(bs, 1, D_SHIFT), state_wkv[0],
        cache_mem_k[0].reshape(bs, N_MEM, D_MODEL), cache_mem_v[0].reshape(bs, N_MEM, D_MODEL),
        _mem_sample, *shared)

    kv_shape = (1, bp, N_MEM, MEM_HEADS, MEM_HEAD_DIM)
    return (y_p, y_s, mk.reshape(kv_shape), mv.reshape(kv_shape), conv_p, shift_p, wkv_p,
            conv_s, shift_s, wkv_s)
```

```python
import functools

import jax
import jax.numpy as jnp
from jax import lax
from jax.experimental import pallas as pl
from jax.experimental.pallas import tpu as pltpu

D_MODEL = 1024
N_MEM = 256
RWKV_HEAD = 64
RWKV_HEADS = D_MODEL // RWKV_HEAD
LORA = 64
MEM_HEADS = 4
MEM_HEAD_DIM = D_MODEL // MEM_HEADS
D_SHIFT = 3 * D_MODEL + 2 * LORA
CONV_TAPS = 3
NORM_EPS = 1e-6
GN_EPS = RWKV_HEAD * 1e-5
KK_EPS = 1e-12

_OFF_CONV = 0
_OFF_SHIFT = 4 * D_MODEL
_OFF_GRWKV = _OFF_SHIFT + D_SHIFT
_OFF_QMEM = _OFF_GRWKV + D_MODEL
_OFF_GMEM = _OFF_QMEM + D_MODEL
_OFF_MCONV = _OFF_GMEM + D_MODEL
_OFF_MRWKV = _OFF_MCONV + D_MODEL
_OFF_MMEM = _OFF_MRWKV + D_MODEL

V7X_SUBLANES = 8
V7X_LANES = 128
GROUP = V7X_SUBLANES
HEAD_PAIRS = D_MODEL // V7X_LANES
SEG_TILE = 256
TOKEN_TILE = 256
PROMPT_TIME_TILE = TOKEN_TILE // GROUP
SAMPLE_MEM_BATCH_TILE = 8
SCAN_UNROLL = 32
N_OPS = 6
OP_R, OP_W, OP_K, OP_V, OP_KK, OP_B = range(N_OPS)
SCAN_UNITS_PER_BLOCK = 3
VMEM_LIMIT = 56 * 1024 * 1024

_F32 = jnp.float32
_BF16 = jnp.bfloat16


def _dot(a, b):
    return jnp.dot(a, b, preferred_element_type=_F32)


def _rmsnorm(x, g):
    ms = jnp.mean(x * x, axis=-1, keepdims=True)
    return x * lax.rsqrt(ms + NORM_EPS) * g


def _sigmoid(x):
    return 1.0 / (1.0 + jnp.exp(-x))


def _silu(x):
    return x * _sigmoid(x)


def _seg_sum(x, ones_bd):
    outs = []
    for c in range(D_MODEL // SEG_TILE):
        xc = x[:, c * SEG_TILE:(c + 1) * SEG_TILE]
        hi = xc.astype(_BF16)
        lo = (xc - hi.astype(_F32)).astype(_BF16)
        outs.append(_dot(hi, ones_bd) + _dot(lo, ones_bd))
    return jnp.concatenate(outs, axis=-1)


def _bcast_rows(st, nt):
    nb, _, c = st.shape
    return jnp.broadcast_to(st, (nb, nt, c)).reshape(nb * nt, c)


def _time_index(nb, nt, c):
    return lax.broadcasted_iota(jnp.int32, (nb * nt, c), 0) % nt


def _slab_start(bl, nt):
    return (bl // GROUP) * (nt * GROUP) + bl % GROUP


def _store_slab(ref, n, val, nb, nt):
    for bl in range(nb):
        rows = val[bl * nt:(bl + 1) * nt]
        for hp in range(HEAD_PAIRS):
            ref[n, hp, pl.ds(_slab_start(bl, nt), nt, stride=GROUP), :] = (
                rows[:, hp * V7X_LANES:(hp + 1) * V7X_LANES])


def _load_slab(ref, nb, nt):
    per_batch = []
    for bl in range(nb):
        per_batch.append(jnp.concatenate(
            [ref[hp, pl.ds(_slab_start(bl, nt), nt, stride=GROUP), :] for hp in range(HEAD_PAIRS)],
            axis=-1))
    return jnp.concatenate(per_batch, axis=0)


def _const_spec(shape, ngrid):
    zeros = (0,) * len(shape)
    if ngrid == 1:
        return pl.BlockSpec(shape, lambda i: zeros, pipeline_mode=pl.Buffered(1))
    return pl.BlockSpec(shape, lambda i, j: zeros, pipeline_mode=pl.Buffered(1))


def _tile_dims(b, t):
    if t >= PROMPT_TIME_TILE:
        nb, nt = GROUP, PROMPT_TIME_TILE
    else:
        nb, nt = TOKEN_TILE // t, t
    assert b % nb == 0 and t % nt == 0 and (nb == GROUP or nt == t)
    return nb, nt, (b // nb, t // nt)


def _params():
    return pltpu.CompilerParams(dimension_semantics=("arbitrary", "arbitrary"), vmem_limit_bytes=VMEM_LIMIT)


def _mem_kv_kernel(mem_ref, g_ref, w_ref, k_ref, v_ref):
    xn = _rmsnorm(mem_ref[...], g_ref[...]).astype(_BF16)
    k_ref[...] = _dot(xn, w_ref[:, 0:D_MODEL])
    v_ref[...] = _dot(xn, w_ref[:, D_MODEL:2 * D_MODEL])


def _mem_kv(mem2d, g, w_bf):
    n = mem2d.shape[0]
    tm = TOKEN_TILE
    return pl.pallas_call(
        _mem_kv_kernel,
        grid=(n // tm,),
        in_specs=[
            pl.BlockSpec((tm, D_MODEL), lambda i: (i, 0)),
            pl.BlockSpec((1, D_MODEL), lambda i: (0, 0)),
            pl.BlockSpec((D_MODEL, 2 * D_MODEL), lambda i: (0, 0)),
        ],
        out_specs=[
            pl.BlockSpec((tm, D_MODEL), lambda i: (i, 0)),
            pl.BlockSpec((tm, D_MODEL), lambda i: (i, 0)),
        ],
        out_shape=[jax.ShapeDtypeStruct((n, D_MODEL), _F32)] * 2,
        compiler_params=pltpu.CompilerParams(
            dimension_semantics=("arbitrary",), vmem_limit_bytes=VMEM_LIMIT),
        name="mem_kv",
    )(mem2d, g, w_bf)


def _conv_kernel(x_ref, *rest, has_state):
    if has_state:
        st_ref, g_ref, w_ref, cw_ref, cb_ref, wb_ref, acc_ref, new_ref = rest
    else:
        g_ref, w_ref, cw_ref, cb_ref, wb_ref, acc_ref, new_ref, st_ref = rest

        @pl.when(pl.program_id(1) == 0)
        def _():
            st_ref[...] = jnp.zeros_like(st_ref)

    nb, nt, _ = x_ref.shape
    rows = nb * nt
    xn = _rmsnorm(x_ref[...].reshape(rows, D_MODEL), g_ref[...]).astype(_BF16)
    h = _dot(xn, w_ref[:, 0:D_MODEL])
    cg = _dot(xn, w_ref[:, 2 * D_MODEL:3 * D_MODEL])
    u = cg * h
    s0 = _bcast_rows(st_ref[:, 0:1, :], nt)
    s1 = _bcast_rows(st_ref[:, 1:2, :], nt)
    t = _time_index(nb, nt, D_MODEL)
    prev1 = jnp.where(t == 0, s1, pltpu.roll(u, 1, axis=0))
    prev2 = jnp.where(t == 0, s0, jnp.where(t == 1, s1, pltpu.roll(u, 2, axis=0)))
    conv = cb_ref[...] + prev2 * cw_ref[0:1, :] + prev1 * cw_ref[1:2, :] + u * cw_ref[2:3, :]
    bg = _dot(xn, w_ref[:, D_MODEL:2 * D_MODEL])
    gc = _dot(xn, w_ref[:, 3 * D_MODEL:4 * D_MODEL])
    y = bg * conv * _silu(gc)
    mc = _dot(xn, w_ref[:, 4 * D_MODEL:5 * D_MODEL])
    acc_ref[...] = (_sigmoid(mc) * _dot(y.astype(_BF16), wb_ref[...])).reshape(nb, nt, D_MODEL)
    new = u.reshape(nb, nt, D_MODEL)[:, nt - (CONV_TAPS - 1):, :]
    new_ref[...] = new
    if not has_state:
        st_ref[...] = new


def _conv(x, st, g, w_bf, cw, cb, wb_bf):
    b, t, _ = x.shape
    nb, nt, grid = _tile_dims(b, t)
    has_state = st is not None
    tok = pl.BlockSpec((nb, nt, D_MODEL), lambda i, j: (i, j, 0))
    st_spec = pl.BlockSpec((nb, CONV_TAPS - 1, D_MODEL), lambda i, j: (i, 0, 0))
    return pl.pallas_call(
        functools.partial(_conv_kernel, has_state=has_state),
        grid=grid,
        in_specs=[tok] + ([st_spec] if has_state else []) + [
            _const_spec((1, D_MODEL), 2),
            _const_spec((D_MODEL, 5 * D_MODEL), 2),
            _const_spec((CONV_TAPS, D_MODEL), 2),
            _const_spec((1, D_MODEL), 2),
            _const_spec((D_MODEL, D_MODEL), 2),
        ],
        out_specs=[tok, st_spec],
        out_shape=[
            jax.ShapeDtypeStruct((b, t, D_MODEL), _F32),
            jax.ShapeDtypeStruct((b, CONV_TAPS - 1, D_MODEL), _F32),
        ],
        scratch_shapes=[] if has_state else [pltpu.VMEM((nb, CONV_TAPS - 1, D_MODEL), _F32)],
        compiler_params=_params(),
        name="conv_branch",
    )(*([x] + ([st] if has_state else []) + [g, w_bf, cw, cb, wb_bf]))


def _mem_gate(xn, attn, acc_in, w_ref, wb_ref):
    gm = _dot(xn, w_ref[:, D_MODEL:2 * D_MODEL])
    y = attn * _silu(gm)
    mm = _dot(xn, w_ref[:, 2 * D_MODEL:3 * D_MODEL])
    return acc_in + _sigmoid(mm) * _dot(y.astype(_BF16), wb_ref[...])


def _mem_prompt_kernel(x_ref, k_ref, v_ref, acc_in_ref, g_ref, w_ref, wb_ref, acc_ref):
    xn = _rmsnorm(x_ref[...], g_ref[...]).astype(_BF16)
    q = _dot(xn, w_ref[:, 0:D_MODEL]) * (MEM_HEAD_DIM ** -0.5)
    heads = []
    for hh in range(MEM_HEADS):
        sl = slice(hh * MEM_HEAD_DIM, (hh + 1) * MEM_HEAD_DIM)
        qh = q[:, sl].astype(_BF16)
        kh = k_ref[:, sl].astype(_BF16)
        vh = v_ref[:, sl].astype(_BF16)
        s = lax.dot_general(qh, kh, (((1,), (1,)), ((), ())), preferred_element_type=_F32)
        e = jnp.exp(s - jnp.max(s, axis=-1, keepdims=True))
        l = jnp.sum(e, axis=-1, keepdims=True)
        heads.append(_dot(e.astype(_BF16), vh) / l)
    attn = jnp.concatenate(heads, axis=-1)
    acc_ref[...] = _mem_gate(xn, attn, acc_in_ref[...], w_ref, wb_ref)


def _mem_sample_kernel(x_ref, k_ref, v_ref, acc_in_ref, g_ref, w_ref, wb_ref, acc_ref):
    tb, nt, _ = x_ref.shape
    rows = tb * nt
    xn = _rmsnorm(x_ref[...].reshape(rows, D_MODEL), g_ref[...]).astype(_BF16)
    q = (_dot(xn, w_ref[:, 0:D_MODEL]) * (MEM_HEAD_DIM ** -0.5)).reshape(tb, nt, D_MODEL)
    heads = []
    for hh in range(MEM_HEADS):
        sl = slice(hh * MEM_HEAD_DIM, (hh + 1) * MEM_HEAD_DIM)
        qh = q[:, :, sl].astype(_BF16)
        kh = k_ref[:, :, sl].astype(_BF16)
        vh = v_ref[:, :, sl].astype(_BF16)
        s = jnp.einsum("bqd,bkd->bqk", qh, kh, preferred_element_type=_F32)
        e = jnp.exp(s - jnp.max(s, axis=-1, keepdims=True))
        l = jnp.sum(e, axis=-1, keepdims=True)
        o = jnp.einsum("bqk,bkd->bqd", e.astype(_BF16), vh, preferred_element_type=_F32) / l
        heads.append(o.reshape(rows, MEM_HEAD_DIM))
    attn = jnp.concatenate(heads, axis=-1)
    acc = _mem_gate(xn, attn, acc_in_ref[...].reshape(rows, D_MODEL), w_ref, wb_ref)
    acc_ref[...] = acc.reshape(tb, nt, D_MODEL)


def _mem_prompt(x, mk, mv, acc_in, g, w_bf, wb_bf):
    b, t, _ = x.shape
    tm = TOKEN_TILE
    tok = pl.BlockSpec((None, tm, D_MODEL), lambda i, j: (i, j, 0))
    kv = pl.BlockSpec((None, N_MEM, D_MODEL), lambda i, j: (i, 0, 0))
    return pl.pallas_call(
        _mem_prompt_kernel,
        grid=(b, t // tm),
        in_specs=[tok, kv, kv, tok, _const_spec((1, D_MODEL), 2), _const_spec((D_MODEL, 3 * D_MODEL), 2),
                  _const_spec((D_MODEL, D_MODEL), 2)],
        out_specs=tok,
        out_shape=jax.ShapeDtypeStruct((b, t, D_MODEL), _F32),
        compiler_params=_params(),
        name="mem_prompt",
    )(x, mk, mv, acc_in, g, w_bf, wb_bf)


def _mem_sample(x, mk, mv, acc_in, g, w_bf, wb_bf):
    b, t, _ = x.shape
    tb = SAMPLE_MEM_BATCH_TILE
    tok = pl.BlockSpec((tb, t, D_MODEL), lambda i: (i, 0, 0))
    kv = pl.BlockSpec((tb, N_MEM, D_MODEL), lambda i: (i, 0, 0))
    return pl.pallas_call(
        _mem_sample_kernel,
        grid=(b // tb,),
        in_specs=[tok, kv, kv, tok, _const_spec((1, D_MODEL), 1), _const_spec((D_MODEL, 3 * D_MODEL), 1),
                  _const_spec((D_MODEL, D_MODEL), 1)],
        out_specs=tok,
        out_shape=jax.ShapeDtypeStruct((b, t, D_MODEL), _F32),
        compiler_params=pltpu.CompilerParams(
            dimension_semantics=("arbitrary",), vmem_limit_bytes=VMEM_LIMIT),
        name="mem_sample",
    )(x, mk, mv, acc_in, g, w_bf, wb_bf)


def _prep_kernel(x_ref, *rest, has_state):
    if has_state:
        st_ref, rest = rest[0], rest[1:]
    (g_ref, w_ref, mu_ref, w0_ref, wup_ref, a0_ref, aup_ref, kk_ref, ka_ref, rk_ref, ones_ref,
     ops_ref, bonus_ref, sg_ref, sm_ref, new_ref) = rest[:16]
    if not has_state:
        st_ref = rest[16]

        @pl.when(pl.program_id(1) == 0)
        def _():
            st_ref[...] = jnp.zeros_like(st_ref)

    nb, nt, _ = x_ref.shape
    rows = nb * nt
    tok = lambda a: a.reshape(nb, nt, D_MODEL)
    xn = _rmsnorm(x_ref[...].reshape(rows, D_MODEL), g_ref[...]).astype(_BF16)
    p = _dot(xn, w_ref[:, 0:D_SHIFT])
    prev = jnp.where(_time_index(nb, nt, D_SHIFT) == 0, _bcast_rows(st_ref[...], nt), pltpu.roll(p, 1, axis=0))
    new = p.reshape(nb, nt, D_SHIFT)[:, nt - 1:, :]
    new_ref[...] = new
    if not has_state:
        st_ref[...] = new
    ps = p + mu_ref[...] * (prev - p)
    r = ps[:, 0:D_MODEL]
    k = ps[:, D_MODEL:2 * D_MODEL]
    v = ps[:, 2 * D_MODEL:3 * D_MODEL]
    lora_in = ps[:, 3 * D_MODEL:D_SHIFT]
    z = w0_ref[...] + _dot(jnp.tanh(lora_in).astype(_BF16), wup_ref[...])
    softplus = jnp.maximum(-z, 0.0) + jnp.log(1.0 + jnp.exp(-jnp.abs(z)))
    _store_slab(ops_ref, OP_W, jnp.exp(-jnp.exp(-softplus - 0.5)), nb, nt)
    a = _sigmoid(a0_ref[...] + _dot(lora_in.astype(_BF16), aup_ref[...]))
    ones = ones_ref[...]
    kk = k * kk_ref[...]
    kk = kk * lax.rsqrt(_seg_sum(kk * kk, ones) + KK_EPS)
    kx = k * (1.0 + (a - 1.0) * ka_ref[...])
    _store_slab(ops_ref, OP_R, r, nb, nt)
    _store_slab(ops_ref, OP_K, kx, nb, nt)
    _store_slab(ops_ref, OP_V, v, nb, nt)
    _store_slab(ops_ref, OP_KK, kk, nb, nt)
    _store_slab(ops_ref, OP_B, kk * a, nb, nt)
    bonus_ref[...] = tok(_seg_sum(r * kx * rk_ref[...], ones) * v)
    sg_ref[...] = tok(_silu(_dot(xn, w_ref[:, D_SHIFT:D_SHIFT + D_MODEL])))
    sm_ref[...] = tok(_sigmoid(_dot(xn, w_ref[:, D_SHIFT + D_MODEL:D_SHIFT + 2 * D_MODEL])))


def _ops_spec(rows, index):
    return pl.BlockSpec((N_OPS, HEAD_PAIRS, rows, V7X_LANES), lambda i, j: (0, 0, index(i, j), 0))


def _prep(x, st, weights):
    b, t, _ = x.shape
    nb, nt, grid = _tile_dims(b, t)
    has_state = st is not None
    tok = pl.BlockSpec((nb, nt, D_MODEL), lambda i, j: (i, j, 0))
    st_spec = pl.BlockSpec((nb, 1, D_SHIFT), lambda i, j: (i, 0, 0))
    slab = _ops_spec(nb * nt, lambda i, j: i * grid[1] + j)
    vec = _const_spec((1, D_MODEL), 2)
    w_specs = [
        vec,
        _const_spec((D_MODEL, D_SHIFT + 2 * D_MODEL), 2),
        _const_spec((1, D_SHIFT), 2),
        vec,
        _const_spec((2 * LORA, D_MODEL), 2),
        vec,
        _const_spec((2 * LORA, D_MODEL), 2),
        vec, vec, vec,
        _const_spec((SEG_TILE, SEG_TILE), 2),
    ]
    slab_shape = jax.ShapeDtypeStruct((N_OPS, HEAD_PAIRS, b * t, V7X_LANES), _F32)
    tok_shape = jax.ShapeDtypeStruct((b, t, D_MODEL), _F32)
    return pl.pallas_call(
        functools.partial(_prep_kernel, has_state=has_state),
        grid=grid,
        in_specs=[tok] + ([st_spec] if has_state else []) + w_specs,
        out_specs=[slab] + [tok] * 3 + [st_spec],
        out_shape=[slab_shape] + [tok_shape] * 3 + [jax.ShapeDtypeStruct((b, 1, D_SHIFT), _F32)],
        scratch_shapes=[] if has_state else [pltpu.VMEM((nb, 1, D_SHIFT), _F32)],
        compiler_params=_params(),
        name="rwkv_prep",
    )(*([x] + ([st] if has_state else []) + list(weights)))


def _scan_kernel(ops_ref, *rest, zero_init):
    if zero_init:
        o_ref, sT_ref, s_ref, yt_ref = rest
    else:
        s0_ref, o_ref, sT_ref, s_ref, yt_ref = rest
    nk = RWKV_HEAD
    tt = o_ref.shape[0]
    kb = SCAN_UNROLL

    @pl.when(pl.program_id(1) == 0)
    def _():
        if zero_init:
            s_ref[...] = jnp.zeros_like(s_ref)
        else:
            s_ref[...] = s0_ref[...]

    def to_lanes(q, pair):
        for j in range(SCAN_UNITS_PER_BLOCK):
            u = SCAN_UNITS_PER_BLOCK * q + j
            if isinstance(u, int):
                n, tau = u // 2, u % 2
            else:
                n, tau = lax.shift_right_logical(u, 1), jnp.bitwise_and(u, 1)
            t_dst = 2 * pair + tau
            t_src = jnp.minimum(t_dst, tt - 2 + tau)
            rows = pl.ds(pl.multiple_of(t_src * GROUP, GROUP), GROUP)
            d = jnp.concatenate([ops_ref[n, hp, rows, :] for hp in range(HEAD_PAIRS)], axis=0)
            tr = d.T
            yt_ref[n, t_dst] = jnp.concatenate([tr[0:nk], tr[nk:2 * nk]], axis=1)

    for q in range(4):
        to_lanes(q, 0)

    def row(n, t, k):
        return yt_ref[n, t, pl.ds(k, 1), :]

    def step(pair, tau):
        t = 2 * pair + tau

        def p1(k, acc):
            return acc + s_ref[k] * row(OP_KK, t, k)

        sa = lax.fori_loop(0, nk, p1, jnp.zeros((nk, V7X_LANES), _F32), unroll=kb)
        vv = yt_ref[OP_V, t]

        def p2(c, o):
            for kk in range(kb):
                k = c * kb + kk
                s = s_ref[k] * row(OP_W, t, k) - sa * row(OP_B, t, k) + vv * row(OP_K, t, k)
                s_ref[k] = s
                o = o + s * row(OP_R, t, k)
            to_lanes(2 * tau + c, pair + 1)
            return o

        o_ref[t] = lax.fori_loop(0, nk // kb, p2, jnp.zeros((nk, V7X_LANES), _F32))

    def step_pair(pair, carry):
        step(pair, 0)
        step(pair, 1)
        return carry

    lax.fori_loop(0, tt // 2, step_pair, 0)

    @pl.when(pl.program_id(1) == pl.num_programs(1) - 1)
    def _():
        sT_ref[...] = s_ref[...]


def _scan(ops, s0, t):
    groups = ops.shape[2] // (t * GROUP)
    tt = min(PROMPT_TIME_TILE, t)
    nt = t // tt
    assert tt % 2 == 0 and RWKV_HEAD // SCAN_UNROLL == 2 and 4 * SCAN_UNITS_PER_BLOCK == 2 * N_OPS
    st_spec = pl.BlockSpec((RWKV_HEAD, RWKV_HEAD, V7X_LANES), lambda g, j: (0, 0, g))
    zero_init = s0 is None
    return pl.pallas_call(
        functools.partial(_scan_kernel, zero_init=zero_init),
        grid=(groups, nt),
        in_specs=[_ops_spec(tt * GROUP, lambda g, j: g * nt + j)] + ([] if zero_init else [st_spec]),
        out_specs=[pl.BlockSpec((tt, RWKV_HEAD, V7X_LANES), lambda g, j: (j, 0, g)), st_spec],
        out_shape=[
            jax.ShapeDtypeStruct((t, RWKV_HEAD, groups * V7X_LANES), _F32),
            jax.ShapeDtypeStruct((RWKV_HEAD, RWKV_HEAD, groups * V7X_LANES), _F32),
        ],
        scratch_shapes=[
            pltpu.VMEM((RWKV_HEAD, RWKV_HEAD, V7X_LANES), _F32),
            pltpu.VMEM((N_OPS, tt + 2, RWKV_HEAD, V7X_LANES), _F32),
        ],
        compiler_params=_params(),
        name="wkv_scan",
    )(*([ops] + ([] if zero_init else [s0])))


def _state_to_lanes(s):
    b = s.shape[0]
    s = s.reshape(b // GROUP, GROUP, HEAD_PAIRS, 2, RWKV_HEAD, RWKV_HEAD)
    return s.transpose(5, 4, 0, 3, 2, 1).reshape(RWKV_HEAD, RWKV_HEAD, b * RWKV_HEADS)


def _state_from_lanes(s, b):
    s = s.reshape(RWKV_HEAD, RWKV_HEAD, b // GROUP, 2, HEAD_PAIRS, GROUP)
    return s.transpose(2, 5, 4, 3, 1, 0).reshape(b, RWKV_HEADS, RWKV_HEAD, RWKV_HEAD)


def _lanes_to_slab(o_ref, slab_ref, nb, nt):
    nk = RWKV_HEAD
    half = V7X_LANES // 2
    low = lax.broadcasted_iota(jnp.int32, (nk, V7X_LANES), 1) < half
    for g in range(nb // GROUP):
        lanes = slice(g * V7X_LANES, (g + 1) * V7X_LANES)
        for i in range(nt // 2):
            y0, y1 = o_ref[2 * i, :, lanes], o_ref[2 * i + 1, :, lanes]
            top = jnp.where(low, y0, pltpu.roll(y1, half, axis=1))
            bot = jnp.where(low, pltpu.roll(y0, half, axis=1), y1)
            d = jnp.concatenate([top, bot], axis=0).T
            for tau in (0, 1):
                for hp in range(HEAD_PAIRS):
                    blk = (tau * HEAD_PAIRS + hp) * GROUP
                    row0 = g * nt * GROUP + (2 * i + tau) * GROUP
                    slab_ref[hp, row0:row0 + GROUP, :] = d[blk:blk + GROUP]


def _out_kernel(o_ref, bonus_ref, sg_ref, sm_ref, acc_ref, x_ref, lng_ref, lnb_ref, wb_ref, wo_ref, nf_ref,
                ones_ref, y_ref, slab_ref):
    nb, nt, _ = x_ref.shape
    rows = nb * nt
    tok = lambda ref: ref[...].reshape(rows, D_MODEL)
    ones = ones_ref[...]
    _lanes_to_slab(o_ref, slab_ref, nb, nt)
    o = _load_slab(slab_ref, nb, nt)
    inv_n = 1.0 / RWKV_HEAD
    d = o - _seg_sum(o, ones) * inv_n
    var = _seg_sum(d * d, ones) * inv_n
    on = d * lax.rsqrt(var + GN_EPS) * lng_ref[...] + lnb_ref[...]
    y = (on + tok(bonus_ref)) * tok(sg_ref)
    merged = tok(acc_ref) + tok(sm_ref) * _dot(y.astype(_BF16), wb_ref[...])
    h = tok(x_ref) + _dot(merged.astype(_BF16), wo_ref[...])
    y_ref[...] = _rmsnorm(h, nf_ref[...]).reshape(nb, nt, D_MODEL)


def _out(o, bonus, sg, sm, acc, x, lng, lnb, wb_bf, wo_bf, nf, ones_bd):
    b, t, _ = x.shape
    nb, nt, grid = _tile_dims(b, t)
    tok = pl.BlockSpec((nb, nt, D_MODEL), lambda i, j: (i, j, 0))
    vec = _const_spec((1, D_MODEL), 2)
    mat = _const_spec((D_MODEL, D_MODEL), 2)
    return pl.pallas_call(
        _out_kernel,
        grid=grid,
        in_specs=[pl.BlockSpec((nt, RWKV_HEAD, nb // GROUP * V7X_LANES), lambda i, j: (j, 0, i))] + [tok] * 5 + [
            vec, vec, mat, mat, vec, _const_spec((SEG_TILE, SEG_TILE), 2)],
        out_specs=tok,
        out_shape=jax.ShapeDtypeStruct((b, t, D_MODEL), _F32),
        scratch_shapes=[pltpu.VMEM((HEAD_PAIRS, nb * nt, V7X_LANES), _F32)],
        compiler_params=_params(),
        name="merge_out",
    )(o, bonus, sg, sm, acc, x, lng, lnb, wb_bf, wo_bf, nf, ones_bd)


def _group(x, conv_st, shift_st, wkv_st, mk, mv, mem_fn, g_in, w_conv, w_mem, conv_w, conv_b, wb, prep_w, out_w):
    b, t, _ = x.shape
    acc, conv_new = _conv(x, conv_st, g_in, w_conv, conv_w, conv_b, wb[0])
    acc = mem_fn(x, mk, mv, acc, g_in, w_mem, wb[2])
    ops, bonus, sg, sm, shift_new = _prep(x, shift_st, prep_w)
    s0 = None if wkv_st is None else _state_to_lanes(wkv_st)
    o, s_fin = _scan(ops, s0, t)
    y = _out(o, bonus, sg, sm, acc, x, *out_w)
    return y, conv_new[None], shift_new.reshape(b, D_SHIFT)[None], _state_from_lanes(s_fin, b)[None]


def kernel(x_prompt, x_sample, mem_prompt, cache_mem_k, cache_mem_v, state_conv, state_shift, state_wkv, norm_in, w_in, conv_w, conv_b, shift_mu, decay_w0, decay_up, icl_a0, icl_up, k_k, k_a, r_k, ln_x_g, ln_x_b, norm_mem, w_mem_kv, w_branch, w_out, norm_final):
    assert norm_in.shape[0] == 1, "single-layer step"
    bp = x_prompt.shape[0]
    bs = x_sample.shape[0]

    row = lambda a: a.reshape(1, -1)
    w = w_in[0]
    cols = lambda lo, n: w[:, lo:lo + n]
    w_conv = jnp.concatenate([cols(_OFF_CONV, 4 * D_MODEL), cols(_OFF_MCONV, D_MODEL)], axis=1).astype(_BF16)
    w_mem = jnp.concatenate([cols(_OFF_QMEM, 2 * D_MODEL), cols(_OFF_MMEM, D_MODEL)], axis=1).astype(_BF16)
    w_rwkv = jnp.concatenate([cols(_OFF_SHIFT, D_SHIFT + D_MODEL), cols(_OFF_MRWKV, D_MODEL)], axis=1).astype(_BF16)
    wb = w_branch[0].astype(_BF16)
    wo = w_out[0].astype(_BF16)
    zpad = jnp.zeros((LORA, D_MODEL), _F32)
    wup = jnp.concatenate([decay_up[0], zpad], axis=0).astype(_BF16)
    aup = jnp.concatenate([zpad, icl_up[0]], axis=0).astype(_BF16)
    ones_bd = jnp.kron(jnp.eye(SEG_TILE // RWKV_HEAD, dtype=_F32),
                       jnp.ones((RWKV_HEAD, RWKV_HEAD), _F32)).astype(_BF16)
    g_in = row(norm_in[0])
    prep_w = (g_in, w_rwkv, row(shift_mu[0]), row(decay_w0[0]), wup, row(icl_a0[0]), aup, row(k_k[0]),
              row(k_a[0]), row(r_k[0]), ones_bd)
    out_w = (row(ln_x_g[0]), row(ln_x_b[0]), wb[1], wo, row(norm_final), ones_bd)
    shared = (g_in, w_conv, w_mem, conv_w[0], row(conv_b[0]), wb, prep_w, out_w)

    mk, mv = _mem_kv(mem_prompt.reshape(bp * N_MEM, D_MODEL), row(norm_mem[0]), w_mem_kv[0].astype(_BF16))
    y_p, conv_p, shift_p, wkv_p = _group(
        x_prompt, None, None, None, mk.reshape(bp, N_MEM, D_MODEL), mv.reshape(bp, N_MEM, D_MODEL),
        _mem_prompt, *shared)
    y_s, conv_s, shift_s, wkv_s = _group(
        x_sample, state_conv[0], state_shift[0].reshape(bs, 1, D_SHIFT), state_wkv[0],
        cache_mem_k[0].reshape(bs, N_MEM, D_MODEL).astype(_BF16),
        cache_mem_v[0].reshape(bs, N_MEM, D_MODEL).astype(_BF16), _mem_sample, *shared)

    kv_shape = (1, bp, N_MEM, MEM_HEADS, MEM_HEAD_DIM)
    return (y_p, y_s, mk.reshape(kv_shape), mv.reshape(kv_shape), conv_p, shift_p, wkv_p,
            conv_s, shift_s, wkv_s)
```

```python
import functools

import jax
import jax.numpy as jnp
from jax import lax
from jax.experimental import pallas as pl
from jax.experimental.pallas import tpu as pltpu

D_MODEL = 1024
N_MEM = 256
RWKV_HEAD = 64
RWKV_HEADS = D_MODEL // RWKV_HEAD
LORA = 64
MEM_HEADS = 4
MEM_HEAD_DIM = D_MODEL // MEM_HEADS
D_SHIFT = 3 * D_MODEL + 2 * LORA
CONV_TAPS = 3
NORM_EPS = 1e-6
GN_EPS = RWKV_HEAD * 1e-5
KK_EPS = 1e-12
MASKED_SCORE = -1e30

_OFF_CONV = 0
_OFF_SHIFT = 4 * D_MODEL
_OFF_GRWKV = _OFF_SHIFT + D_SHIFT
_OFF_QMEM = _OFF_GRWKV + D_MODEL
_OFF_GMEM = _OFF_QMEM + D_MODEL
_OFF_MCONV = _OFF_GMEM + D_MODEL
_OFF_MRWKV = _OFF_MCONV + D_MODEL
_OFF_MMEM = _OFF_MRWKV + D_MODEL

V7X_SUBLANES = 8
V7X_LANES = 128
GROUP = V7X_SUBLANES
HEAD_PAIRS = D_MODEL // V7X_LANES
SEG_TILE = 256
TOKEN_TILE = 256
PROMPT_TIME_TILE = TOKEN_TILE // GROUP
SAMPLE_MEM_BATCH_TILE = 8
SCAN_UNROLL = 32
N_OPS = 6
OP_R, OP_W, OP_K, OP_V, OP_KK, OP_B = range(N_OPS)
SCAN_UNITS_PER_BLOCK = 3
VMEM_LIMIT = 56 * 1024 * 1024

_F32 = jnp.float32
_BF16 = jnp.bfloat16


def _dot(a, b):
    return jnp.dot(a, b, preferred_element_type=_F32)


def _rmsnorm(x, g):
    ms = jnp.mean(x * x, axis=-1, keepdims=True)
    return x * lax.rsqrt(ms + NORM_EPS) * g


def _sigmoid(x):
    return 1.0 / (1.0 + jnp.exp(-x))


def _silu(x):
    return x * _sigmoid(x)


def _seg_sum(x, ones_bd):
    xb = x.astype(_BF16)
    return jnp.concatenate(
        [_dot(xb[:, c * SEG_TILE:(c + 1) * SEG_TILE], ones_bd) for c in range(D_MODEL // SEG_TILE)], axis=-1)


def _bcast_rows(st, nt):
    nb, _, c = st.shape
    return jnp.broadcast_to(st, (nb, nt, c)).reshape(nb * nt, c)


def _time_index(nb, nt, c):
    return lax.broadcasted_iota(jnp.int32, (nb * nt, c), 0) % nt


def _slab_start(bl, nt):
    return (bl // GROUP) * (nt * GROUP) + bl % GROUP


def _store_slab(ref, n, val, nb, nt):
    for bl in range(nb):
        rows = val[bl * nt:(bl + 1) * nt]
        for hp in range(HEAD_PAIRS):
            ref[n, hp, pl.ds(_slab_start(bl, nt), nt, stride=GROUP), :] = (
                rows[:, hp * V7X_LANES:(hp + 1) * V7X_LANES])


def _load_slab(ref, nb, nt):
    per_batch = []
    for bl in range(nb):
        per_batch.append(jnp.concatenate(
            [ref[hp, pl.ds(_slab_start(bl, nt), nt, stride=GROUP), :] for hp in range(HEAD_PAIRS)],
            axis=-1))
    return jnp.concatenate(per_batch, axis=0)


def _const_spec(shape, ngrid):
    zeros = (0,) * len(shape)
    if ngrid == 1:
        return pl.BlockSpec(shape, lambda i: zeros, pipeline_mode=pl.Buffered(1))
    return pl.BlockSpec(shape, lambda i, j: zeros, pipeline_mode=pl.Buffered(1))


def _tile_dims(b, t):
    if t >= PROMPT_TIME_TILE:
        nb, nt = GROUP, PROMPT_TIME_TILE
    else:
        nb, nt = TOKEN_TILE // t, t
    assert b % nb == 0 and t % nt == 0 and (nb == GROUP or nt == t)
    return nb, nt, (b // nb, t // nt)


def _params():
    return pltpu.CompilerParams(dimension_semantics=("arbitrary", "arbitrary"), vmem_limit_bytes=VMEM_LIMIT)


def _mem_kv_kernel(mem_ref, g_ref, w_ref, k_ref, v_ref):
    xn = _rmsnorm(mem_ref[...], g_ref[...]).astype(_BF16)
    k_ref[...] = _dot(xn, w_ref[:, 0:D_MODEL])
    v_ref[...] = _dot(xn, w_ref[:, D_MODEL:2 * D_MODEL])


def _mem_kv(mem2d, g, w_bf):
    n = mem2d.shape[0]
    tm = TOKEN_TILE
    return pl.pallas_call(
        _mem_kv_kernel,
        grid=(n // tm,),
        in_specs=[
            pl.BlockSpec((tm, D_MODEL), lambda i: (i, 0)),
            pl.BlockSpec((1, D_MODEL), lambda i: (0, 0)),
            pl.BlockSpec((D_MODEL, 2 * D_MODEL), lambda i: (0, 0)),
        ],
        out_specs=[
            pl.BlockSpec((tm, D_MODEL), lambda i: (i, 0)),
            pl.BlockSpec((tm, D_MODEL), lambda i: (i, 0)),
        ],
        out_shape=[jax.ShapeDtypeStruct((n, D_MODEL), _F32)] * 2,
        compiler_params=pltpu.CompilerParams(
            dimension_semantics=("arbitrary",), vmem_limit_bytes=VMEM_LIMIT),
        name="mem_kv",
    )(mem2d, g, w_bf)


def _conv_kernel(x_ref, *rest, has_state):
    if has_state:
        st_ref, g_ref, w_ref, cw_ref, cb_ref, wb_ref, acc_ref, new_ref = rest
    else:
        g_ref, w_ref, cw_ref, cb_ref, wb_ref, acc_ref, new_ref, st_ref = rest

        @pl.when(pl.program_id(1) == 0)
        def _():
            st_ref[...] = jnp.zeros_like(st_ref)

    nb, nt, _ = x_ref.shape
    rows = nb * nt
    xn = _rmsnorm(x_ref[...].reshape(rows, D_MODEL), g_ref[...]).astype(_BF16)
    h = _dot(xn, w_ref[:, 0:D_MODEL])
    cg = _dot(xn, w_ref[:, 2 * D_MODEL:3 * D_MODEL])
    u = cg * h
    s0 = _bcast_rows(st_ref[:, 0:1, :], nt)
    s1 = _bcast_rows(st_ref[:, 1:2, :], nt)
    t = _time_index(nb, nt, D_MODEL)
    prev1 = jnp.where(t == 0, s1, pltpu.roll(u, 1, axis=0))
    prev2 = jnp.where(t == 0, s0, jnp.where(t == 1, s1, pltpu.roll(u, 2, axis=0)))
    conv = cb_ref[...] + prev2 * cw_ref[0:1, :] + prev1 * cw_ref[1:2, :] + u * cw_ref[2:3, :]
    bg = _dot(xn, w_ref[:, D_MODEL:2 * D_MODEL])
    gc = _dot(xn, w_ref[:, 3 * D_MODEL:4 * D_MODEL])
    y = bg * conv * _silu(gc)
    mc = _dot(xn, w_ref[:, 4 * D_MODEL:5 * D_MODEL])
    acc_ref[...] = (_sigmoid(mc) * _dot(y.astype(_BF16), wb_ref[...])).reshape(nb, nt, D_MODEL)
    new = u.reshape(nb, nt, D_MODEL)[:, nt - (CONV_TAPS - 1):, :]
    new_ref[...] = new
    if not has_state:
        st_ref[...] = new


def _conv(x, st, g, w_bf, cw, cb, wb_bf):
    b, t, _ = x.shape
    nb, nt, grid = _tile_dims(b, t)
    has_state = st is not None
    tok = pl.BlockSpec((nb, nt, D_MODEL), lambda i, j: (i, j, 0))
    st_spec = pl.BlockSpec((nb, CONV_TAPS - 1, D_MODEL), lambda i, j: (i, 0, 0))
    return pl.pallas_call(
        functools.partial(_conv_kernel, has_state=has_state),
        grid=grid,
        in_specs=[tok] + ([st_spec] if has_state else []) + [
            _const_spec((1, D_MODEL), 2),
            _const_spec((D_MODEL, 5 * D_MODEL), 2),
            _const_spec((CONV_TAPS, D_MODEL), 2),
            _const_spec((1, D_MODEL), 2),
            _const_spec((D_MODEL, D_MODEL), 2),
        ],
        out_specs=[tok, st_spec],
        out_shape=[
            jax.ShapeDtypeStruct((b, t, D_MODEL), _F32),
            jax.ShapeDtypeStruct((b, CONV_TAPS - 1, D_MODEL), _F32),
        ],
        scratch_shapes=[] if has_state else [pltpu.VMEM((nb, CONV_TAPS - 1, D_MODEL), _F32)],
        compiler_params=_params(),
        name="conv_branch",
    )(*([x] + ([st] if has_state else []) + [g, w_bf, cw, cb, wb_bf]))


def _mem_gate(xn, attn, acc_in, w_ref, wb_ref):
    gm = _dot(xn, w_ref[:, D_MODEL:2 * D_MODEL])
    y = attn * _silu(gm)
    mm = _dot(xn, w_ref[:, 2 * D_MODEL:3 * D_MODEL])
    return acc_in + _sigmoid(mm) * _dot(y.astype(_BF16), wb_ref[...])


def _mem_prompt_kernel(x_ref, k_ref, v_ref, acc_in_ref, g_ref, w_ref, wb_ref, acc_ref):
    xn = _rmsnorm(x_ref[...], g_ref[...]).astype(_BF16)
    q = _dot(xn, w_ref[:, 0:D_MODEL]) * (MEM_HEAD_DIM ** -0.5)
    heads = []
    for hh in range(MEM_HEADS):
        sl = slice(hh * MEM_HEAD_DIM, (hh + 1) * MEM_HEAD_DIM)
        qh = q[:, sl].astype(_BF16)
        kh = k_ref[:, sl].astype(_BF16)
        vh = v_ref[:, sl].astype(_BF16)
        s = lax.dot_general(qh, kh, (((1,), (1,)), ((), ())), preferred_element_type=_F32)
        e = jnp.exp(s - jnp.max(s, axis=-1, keepdims=True))
        l = jnp.sum(e, axis=-1, keepdims=True)
        heads.append(_dot(e.astype(_BF16), vh) / l)
    attn = jnp.concatenate(heads, axis=-1)
    acc_ref[...] = _mem_gate(xn, attn, acc_in_ref[...], w_ref, wb_ref)


def _mem_sample_kernel(x_ref, k_ref, v_ref, acc_in_ref, g_ref, w_ref, wb_ref, acc_ref):
    tb, nt, _ = x_ref.shape
    rows = tb * nt
    xn = _rmsnorm(x_ref[...].reshape(rows, D_MODEL), g_ref[...]).astype(_BF16)
    q = (_dot(xn, w_ref[:, 0:D_MODEL]) * (MEM_HEAD_DIM ** -0.5)).reshape(tb, nt, D_MODEL)
    nkh = N_MEM * MEM_HEADS
    q4 = jnp.concatenate(
        [q[:, :, hh * MEM_HEAD_DIM:(hh + 1) * MEM_HEAD_DIM] for hh in range(MEM_HEADS)], axis=1).astype(_BF16)
    k_all = k_ref[...].reshape(tb, nkh, MEM_HEAD_DIM).astype(_BF16)
    v_all = v_ref[...].reshape(tb, nkh, MEM_HEAD_DIM).astype(_BF16)
    s = jnp.einsum("bqd,bkd->bqk", q4, k_all, preferred_element_type=_F32)
    q_head = lax.broadcasted_iota(jnp.int32, s.shape, 1) // nt
    k_head = lax.broadcasted_iota(jnp.int32, s.shape, 2) % MEM_HEADS
    s = jnp.where(q_head == k_head, s, MASKED_SCORE)
    e = jnp.exp(s - jnp.max(s, axis=-1, keepdims=True))
    l = jnp.sum(e, axis=-1, keepdims=True)
    o = jnp.einsum("bqk,bkd->bqd", e.astype(_BF16), v_all, preferred_element_type=_F32) / l
    attn = jnp.concatenate([o[:, hh * nt:(hh + 1) * nt, :] for hh in range(MEM_HEADS)], axis=-1)
    attn = attn.reshape(rows, D_MODEL)
    acc = _mem_gate(xn, attn, acc_in_ref[...].reshape(rows, D_MODEL), w_ref, wb_ref)
    acc_ref[...] = acc.reshape(tb, nt, D_MODEL)


def _mem_prompt(x, mk, mv, acc_in, g, w_bf, wb_bf):
    b, t, _ = x.shape
    tm = TOKEN_TILE
    tok = pl.BlockSpec((None, tm, D_MODEL), lambda i, j: (i, j, 0))
    kv = pl.BlockSpec((None, N_MEM, D_MODEL), lambda i, j: (i, 0, 0))
    return pl.pallas_call(
        _mem_prompt_kernel,
        grid=(b, t // tm),
        in_specs=[tok, kv, kv, tok, _const_spec((1, D_MODEL), 2), _const_spec((D_MODEL, 3 * D_MODEL), 2),
                  _const_spec((D_MODEL, D_MODEL), 2)],
        out_specs=tok,
        out_shape=jax.ShapeDtypeStruct((b, t, D_MODEL), _F32),
        compiler_params=_params(),
        name="mem_prompt",
    )(x, mk, mv, acc_in, g, w_bf, wb_bf)


def _mem_sample(x, mk, mv, acc_in, g, w_bf, wb_bf):
    b, t, _ = x.shape
    tb = SAMPLE_MEM_BATCH_TILE
    tok = pl.BlockSpec((tb, t, D_MODEL), lambda i: (i, 0, 0))
    kv = pl.BlockSpec((None, tb, N_MEM, MEM_HEADS, MEM_HEAD_DIM), lambda i: (0, i, 0, 0, 0))
    return pl.pallas_call(
        _mem_sample_kernel,
        grid=(b // tb,),
        in_specs=[tok, kv, kv, tok, _const_spec((1, D_MODEL), 1), _const_spec((D_MODEL, 3 * D_MODEL), 1),
                  _const_spec((D_MODEL, D_MODEL), 1)],
        out_specs=tok,
        out_shape=jax.ShapeDtypeStruct((b, t, D_MODEL), _F32),
        compiler_params=pltpu.CompilerParams(
            dimension_semantics=("arbitrary",), vmem_limit_bytes=VMEM_LIMIT),
        name="mem_sample",
    )(x, mk, mv, acc_in, g, w_bf, wb_bf)


def _prep_project(x_ref, g_ref, w_ref, sg_ref, sm_ref):
    nb, nt, _ = x_ref.shape
    tok = lambda a: a.reshape(nb, nt, D_MODEL)
    xn = _rmsnorm(x_ref[...].reshape(nb * nt, D_MODEL), g_ref[...]).astype(_BF16)
    sg_ref[...] = tok(_silu(_dot(xn, w_ref[:, D_SHIFT:D_SHIFT + D_MODEL])))
    sm_ref[...] = tok(_sigmoid(_dot(xn, w_ref[:, D_SHIFT + D_MODEL:D_SHIFT + 2 * D_MODEL])))
    return _dot(xn, w_ref[:, 0:D_SHIFT])


def _prep_mix(p, prev_rows, nb, nt, mu_ref, w0_ref, wup_ref, a0_ref, aup_ref, kk_ref, ka_ref, rk_ref, ones_ref,
              ops_ref, bonus_ref):
    prev = jnp.where(_time_index(nb, nt, D_SHIFT) == 0, _bcast_rows(prev_rows, nt), pltpu.roll(p, 1, axis=0))
    ps = p + mu_ref[...] * (prev - p)
    r = ps[:, 0:D_MODEL]
    k = ps[:, D_MODEL:2 * D_MODEL]
    v = ps[:, 2 * D_MODEL:3 * D_MODEL]
    lora_in = ps[:, 3 * D_MODEL:D_SHIFT]
    z = w0_ref[...] + _dot(jnp.tanh(lora_in).astype(_BF16), wup_ref[...])
    softplus = jnp.maximum(-z, 0.0) + jnp.log(1.0 + jnp.exp(-jnp.abs(z)))
    _store_slab(ops_ref, OP_W, jnp.exp(-jnp.exp(-softplus - 0.5)), nb, nt)
    a = _sigmoid(a0_ref[...] + _dot(lora_in.astype(_BF16), aup_ref[...]))
    ones = ones_ref[...]
    kk = k * kk_ref[...]
    kk = kk * lax.rsqrt(_seg_sum(kk * kk, ones) + KK_EPS)
    kx = k * (1.0 + (a - 1.0) * ka_ref[...])
    _store_slab(ops_ref, OP_R, r, nb, nt)
    _store_slab(ops_ref, OP_K, kx, nb, nt)
    _store_slab(ops_ref, OP_V, v, nb, nt)
    _store_slab(ops_ref, OP_KK, kk, nb, nt)
    _store_slab(ops_ref, OP_B, kk * a, nb, nt)
    bonus_ref[...] = (_seg_sum(r * kx * rk_ref[...], ones) * v).reshape(nb, nt, D_MODEL)
    return p.reshape(nb, nt, D_SHIFT)[:, nt - 1:, :]


def _prep_kernel(x_ref, st_ref, g_ref, w_ref, *rest):
    mix_w, (ops_ref, bonus_ref, sg_ref, sm_ref, new_ref) = rest[:9], rest[9:]
    nb, nt, _ = x_ref.shape
    p = _prep_project(x_ref, g_ref, w_ref, sg_ref, sm_ref)
    new_ref[...] = _prep_mix(p, st_ref[...], nb, nt, *mix_w, ops_ref, bonus_ref)


def _prep_carry_kernel(x_ref, g_ref, w_ref, *rest):
    mix_w, (ops_ref, bonus_ref, sg_ref, sm_ref, new_ref, st_ref) = rest[:9], rest[9:]
    nb, nt, _ = x_ref.shape

    @pl.when(pl.program_id(1) == 0)
    def _():
        st_ref[...] = jnp.zeros_like(st_ref)

    p = _prep_project(x_ref, g_ref, w_ref, sg_ref, sm_ref)
    new = _prep_mix(p, st_ref[...], nb, nt, *mix_w, ops_ref, bonus_ref)
    st_ref[...] = new
    new_ref[...] = new


def _ops_spec(rows, index):
    return pl.BlockSpec((N_OPS, HEAD_PAIRS, rows, V7X_LANES), lambda i, j: (0, 0, index(i, j), 0))


def _prep(x, st, weights):
    b, t, _ = x.shape
    nb, nt, grid = _tile_dims(b, t)
    has_state = st is not None
    tok = pl.BlockSpec((nb, nt, D_MODEL), lambda i, j: (i, j, 0))
    st_spec = pl.BlockSpec((nb, 1, D_SHIFT), lambda i, j: (i, 0, 0))
    slab = _ops_spec(nb * nt, lambda i, j: i * grid[1] + j)
    vec = _const_spec((1, D_MODEL), 2)
    w_specs = [
        vec,
        _const_spec((D_MODEL, D_SHIFT + 2 * D_MODEL), 2),
        _const_spec((1, D_SHIFT), 2),
        vec,
        _const_spec((2 * LORA, D_MODEL), 2),
        vec,
        _const_spec((2 * LORA, D_MODEL), 2),
        vec, vec, vec,
        _const_spec((SEG_TILE, SEG_TILE), 2),
    ]
    slab_shape = jax.ShapeDtypeStruct((N_OPS, HEAD_PAIRS, b * t, V7X_LANES), _F32)
    tok_shape = jax.ShapeDtypeStruct((b, t, D_MODEL), _F32)
    return pl.pallas_call(
        _prep_kernel if has_state else _prep_carry_kernel,
        grid=grid,
        in_specs=[tok] + ([st_spec] if has_state else []) + w_specs,
        out_specs=[slab] + [tok] * 3 + [st_spec],
        out_shape=[slab_shape] + [tok_shape] * 3 + [jax.ShapeDtypeStruct((b, 1, D_SHIFT), _F32)],
        scratch_shapes=[] if has_state else [pltpu.VMEM((nb, 1, D_SHIFT), _F32)],
        compiler_params=_params(),
        name="rwkv_prep",
    )(*([x] + ([st] if has_state else []) + list(weights)))


def _scan_kernel(ops_ref, *rest, zero_init):
    if zero_init:
        o_ref, sT_ref, s_ref, yt_ref = rest
    else:
        s0_ref, o_ref, sT_ref, s_ref, yt_ref = rest
    nk = RWKV_HEAD
    tt = o_ref.shape[0]
    kb = SCAN_UNROLL

    @pl.when(pl.program_id(1) == 0)
    def _():
        if zero_init:
            s_ref[...] = jnp.zeros_like(s_ref)
        else:
            s_ref[...] = s0_ref[...]

    def to_lanes(q, pair):
        for j in range(SCAN_UNITS_PER_BLOCK):
            u = SCAN_UNITS_PER_BLOCK * q + j
            if isinstance(u, int):
                n, tau = u // 2, u % 2
            else:
                n, tau = lax.shift_right_logical(u, 1), jnp.bitwise_and(u, 1)
            t_dst = 2 * pair + tau
            t_src = jnp.minimum(t_dst, tt - 2 + tau)
            rows = pl.ds(pl.multiple_of(t_src * GROUP, GROUP), GROUP)
            d = jnp.concatenate([ops_ref[n, hp, rows, :] for hp in range(HEAD_PAIRS)], axis=0)
            tr = d.T
            yt_ref[n, t_dst] = jnp.concatenate([tr[0:nk], tr[nk:2 * nk]], axis=1)

    for q in range(4):
        to_lanes(q, 0)

    def row(n, t, k):
        return yt_ref[n, t, pl.ds(k, 1), :]

    def step(pair, tau):
        t = 2 * pair + tau

        def p1(k, acc):
            return acc + s_ref[k] * row(OP_KK, t, k)

        sa = lax.fori_loop(0, nk, p1, jnp.zeros((nk, V7X_LANES), _F32), unroll=kb)
        vv = yt_ref[OP_V, t]

        def p2(c, o):
            for kk in range(kb):
                k = c * kb + kk
                s = s_ref[k] * row(OP_W, t, k) - sa * row(OP_B, t, k) + vv * row(OP_K, t, k)
                s_ref[k] = s
                o = o + s * row(OP_R, t, k)
            to_lanes(2 * tau + c, pair + 1)
            return o

        o_ref[t] = lax.fori_loop(0, nk // kb, p2, jnp.zeros((nk, V7X_LANES), _F32))

    def step_pair(pair, carry):
        step(pair, 0)
        step(pair, 1)
        return carry

    lax.fori_loop(0, tt // 2, step_pair, 0)

    @pl.when(pl.program_id(1) == pl.num_programs(1) - 1)
    def _():
        sT_ref[...] = s_ref[...]


def _scan(ops, s0, t):
    groups = ops.shape[2] // (t * GROUP)
    tt = min(PROMPT_TIME_TILE, t)
    nt = t // tt
    assert tt % 2 == 0 and RWKV_HEAD // SCAN_UNROLL == 2 and 4 * SCAN_UNITS_PER_BLOCK == 2 * N_OPS
    st_spec = pl.BlockSpec((RWKV_HEAD, RWKV_HEAD, V7X_LANES), lambda g, j: (0, 0, g))
    zero_init = s0 is None
    return pl.pallas_call(
        functools.partial(_scan_kernel, zero_init=zero_init),
        grid=(groups, nt),
        in_specs=[_ops_spec(tt * GROUP, lambda g, j: g * nt + j)] + ([] if zero_init else [st_spec]),
        out_specs=[pl.BlockSpec((tt, RWKV_HEAD, V7X_LANES), lambda g, j: (j, 0, g)), st_spec],
        out_shape=[
            jax.ShapeDtypeStruct((t, RWKV_HEAD, groups * V7X_LANES), _F32),
            jax.ShapeDtypeStruct((RWKV_HEAD, RWKV_HEAD, groups * V7X_LANES), _F32),
        ],
        scratch_shapes=[
            pltpu.VMEM((RWKV_HEAD, RWKV_HEAD, V7X_LANES), _F32),
            pltpu.VMEM((N_OPS, tt + 2, RWKV_HEAD, V7X_LANES), _F32),
        ],
        compiler_params=_params(),
        name="wkv_scan",
    )(*([ops] + ([] if zero_init else [s0])))


def _state_to_lanes(s):
    b = s.shape[0]
    s = s.reshape(b // GROUP, GROUP, HEAD_PAIRS, 2, RWKV_HEAD, RWKV_HEAD)
    return s.transpose(5, 4, 0, 3, 2, 1).reshape(RWKV_HEAD, RWKV_HEAD, b * RWKV_HEADS)


def _state_from_lanes(s, b):
    s = s.reshape(RWKV_HEAD, RWKV_HEAD, b // GROUP, 2, HEAD_PAIRS, GROUP)
    return s.transpose(2, 5, 4, 3, 1, 0).reshape(b, RWKV_HEADS, RWKV_HEAD, RWKV_HEAD)


def _lanes_to_slab(o_ref, slab_ref, nb, nt):
    nk = RWKV_HEAD
    half = V7X_LANES // 2
    low = lax.broadcasted_iota(jnp.int32, (nk, V7X_LANES), 1) < half
    for g in range(nb // GROUP):
        lanes = slice(g * V7X_LANES, (g + 1) * V7X_LANES)
        for i in range(nt // 2):
            y0, y1 = o_ref[2 * i, :, lanes], o_ref[2 * i + 1, :, lanes]
            top = jnp.where(low, y0, pltpu.roll(y1, half, axis=1))
            bot = jnp.where(low, pltpu.roll(y0, half, axis=1), y1)
            d = jnp.concatenate([top, bot], axis=0).T
            for tau in (0, 1):
                for hp in range(HEAD_PAIRS):
                    blk = (tau * HEAD_PAIRS + hp) * GROUP
                    row0 = g * nt * GROUP + (2 * i + tau) * GROUP
                    slab_ref[hp, row0:row0 + GROUP, :] = d[blk:blk + GROUP]


def _out_kernel(o_ref, bonus_ref, sg_ref, sm_ref, acc_ref, x_ref, lng_ref, lnb_ref, wb_ref, wo_ref, nf_ref,
                ones_ref, y_ref, slab_ref):
    nb, nt, _ = x_ref.shape
    rows = nb * nt
    tok = lambda ref: ref[...].reshape(rows, D_MODEL)
    ones = ones_ref[...]
    _lanes_to_slab(o_ref, slab_ref, nb, nt)
    o = _load_slab(slab_ref, nb, nt)
    inv_n = 1.0 / RWKV_HEAD
    d = o - _seg_sum(o, ones) * inv_n
    var = _seg_sum(d * d, ones) * inv_n
    on = d * lax.rsqrt(var + GN_EPS) * lng_ref[...] + lnb_ref[...]
    y = (on + tok(bonus_ref)) * tok(sg_ref)
    merged = tok(acc_ref) + tok(sm_ref) * _dot(y.astype(_BF16), wb_ref[...])
    h = tok(x_ref) + _dot(merged.astype(_BF16), wo_ref[...])
    y_ref[...] = _rmsnorm(h, nf_ref[...]).reshape(nb, nt, D_MODEL)


def _out(o, bonus, sg, sm, acc, x, lng, lnb, wb_bf, wo_bf, nf, ones_bd):
    b, t, _ = x.shape
    nb, nt, grid = _tile_dims(b, t)
    tok = pl.BlockSpec((nb, nt, D_MODEL), lambda i, j: (i, j, 0))
    vec = _const_spec((1, D_MODEL), 2)
    mat = _const_spec((D_MODEL, D_MODEL), 2)
    return pl.pallas_call(
        _out_kernel,
        grid=grid,
        in_specs=[pl.BlockSpec((nt, RWKV_HEAD, nb // GROUP * V7X_LANES), lambda i, j: (j, 0, i))] + [tok] * 5 + [
            vec, vec, mat, mat, vec, _const_spec((SEG_TILE, SEG_TILE), 2)],
        out_specs=tok,
        out_shape=jax.ShapeDtypeStruct((b, t, D_MODEL), _F32),
        scratch_shapes=[pltpu.VMEM((HEAD_PAIRS, nb * nt, V7X_LANES), _F32)],
        compiler_params=_params(),
        name="merge_out",
    )(o, bonus, sg, sm, acc, x, lng, lnb, wb_bf, wo_bf, nf, ones_bd)


def _group(x, conv_st, shift_st, wkv_st, mk, mv, mem_fn, g_in, w_conv, w_mem, conv_w, conv_b, wb, prep_w, out_w):
    b, t, _ = x.shape
    acc, conv_new = _conv(x, conv_st, g_in, w_conv, conv_w, conv_b, wb[0])
    acc = mem_fn(x, mk, mv, acc, g_in, w_mem, wb[2])
    ops, bonus, sg, sm, shift_new = _prep(x, shift_st, prep_w)
    s0 = None if wkv_st is None else _state_to_lanes(wkv_st)
    o, s_fin = _scan(ops, s0, t)
    y = _out(o, bonus, sg, sm, acc, x, *out_w)
    return y, conv_new[None], shift_new.reshape(b, D_SHIFT)[None], _state_from_lanes(s_fin, b)[None]


def kernel(x_prompt, x_sample, mem_prompt, cache_mem_k, cache_mem_v, state_conv, state_shift, state_wkv, norm_in, w_in, conv_w, conv_b, shift_mu, decay_w0, decay_up, icl_a0, icl_up, k_k, k_a, r_k, ln_x_g, ln_x_b, norm_mem, w_mem_kv, w_branch, w_out, norm_final):
    assert norm_in.shape[0] == 1, "single-layer step"
    bp = x_prompt.shape[0]
    bs = x_sample.shape[0]

    row = lambda a: a.reshape(1, -1)
    w = w_in[0]
    cols = lambda lo, n: w[:, lo:lo + n]
    w_conv = jnp.concatenate([cols(_OFF_CONV, 4 * D_MODEL), cols(_OFF_MCONV, D_MODEL)], axis=1).astype(_BF16)
    w_mem = jnp.concatenate([cols(_OFF_QMEM, 2 * D_MODEL), cols(_OFF_MMEM, D_MODEL)], axis=1).astype(_BF16)
    w_rwkv = jnp.concatenate([cols(_OFF_SHIFT, D_SHIFT + D_MODEL), cols(_OFF_MRWKV, D_MODEL)], axis=1).astype(_BF16)
    wb = w_branch[0].astype(_BF16)
    wo = w_out[0].astype(_BF16)
    zpad = jnp.zeros((LORA, D_MODEL), _F32)
    wup = jnp.concatenate([decay_up[0], zpad], axis=0).astype(_BF16)
    aup = jnp.concatenate([zpad, icl_up[0]], axis=0).astype(_BF16)
    ones_bd = jnp.kron(jnp.eye(SEG_TILE // RWKV_HEAD, dtype=_F32),
                       jnp.ones((RWKV_HEAD, RWKV_HEAD), _F32)).astype(_BF16)
    g_in = row(norm_in[0])
    prep_w = (g_in, w_rwkv, row(shift_mu[0]), row(decay_w0[0]), wup, row(icl_a0[0]), aup, row(k_k[0]),
              row(k_a[0]), row(r_k[0]), ones_bd)
    out_w = (row(ln_x_g[0]), row(ln_x_b[0]), wb[1], wo, row(norm_final), ones_bd)
    shared = (g_in, w_conv, w_mem, conv_w[0], row(conv_b[0]), wb, prep_w, out_w)

    mk, mv = _mem_kv(mem_prompt.reshape(bp * N_MEM, D_MODEL), row(norm_mem[0]), w_mem_kv[0].astype(_BF16))
    y_p, conv_p, shift_p, wkv_p = _group(
        x_prompt, None, None, None, mk.reshape(bp, N_MEM, D_MODEL), mv.reshape(bp, N_MEM, D_MODEL),
        _mem_prompt, *shared)
    y_s, conv_s, shift_s, wkv_s = _group(
        x_sample, state_conv[0], state_shift[0].reshape(bs, 1, D_SHIFT), state_wkv[0],
        cache_mem_k, cache_mem_v, _mem_sample, *shared)

    kv_shape = (1, bp, N_MEM, MEM_HEADS, MEM_HEAD_DIM)
    return (y_p, y_s, mk.reshape(kv_shape), mv.reshape(kv_shape), conv_p, shift_p, wkv_p,
            conv_s, shift_s, wkv_s)
```

```python
import functools

import jax
import jax.numpy as jnp
from jax import lax
from jax.experimental import pallas as pl
from jax.experimental.pallas import tpu as pltpu

D_MODEL = 1024
N_MEM = 256
RWKV_HEAD = 64
RWKV_HEADS = D_MODEL // RWKV_HEAD
LORA = 64
MEM_HEADS = 4
MEM_HEAD_DIM = D_MODEL // MEM_HEADS
D_SHIFT = 3 * D_MODEL + 2 * LORA
CONV_TAPS = 3
NORM_EPS = 1e-6
GN_EPS = RWKV_HEAD * 1e-5
KK_EPS = 1e-12
MASKED_SCORE = -1e30

_OFF_CONV = 0
_OFF_SHIFT = 4 * D_MODEL
_OFF_GRWKV = _OFF_SHIFT + D_SHIFT
_OFF_QMEM = _OFF_GRWKV + D_MODEL
_OFF_GMEM = _OFF_QMEM + D_MODEL
_OFF_MCONV = _OFF_GMEM + D_MODEL
_OFF_MRWKV = _OFF_MCONV + D_MODEL
_OFF_MMEM = _OFF_MRWKV + D_MODEL

V7X_SUBLANES = 8
V7X_LANES = 128
GROUP = V7X_SUBLANES
HEAD_PAIRS = D_MODEL // V7X_LANES
SEG_TILE = 256
TOKEN_TILE = 256
WIDE_TOKEN_TILE = 512
PROMPT_TIME_TILE = TOKEN_TILE // GROUP
SAMPLE_MEM_BATCH_TILE = 8
SCAN_UNROLL = 32
SCAN_UPDATE_BLOCK = 32
N_OPS = 6
OP_R, OP_W, OP_K, OP_V, OP_KK, OP_B = range(N_OPS)
SCAN_UNITS_PER_BLOCK = 2 * N_OPS // (2 * (RWKV_HEAD // SCAN_UPDATE_BLOCK))
N_SCALED = 4
SC_KK, SC_B, SC_K, SC_R = range(N_SCALED)
VMEM_LIMIT = 56 * 1024 * 1024

_F32 = jnp.float32
_BF16 = jnp.bfloat16


def _dot(a, b):
    return jnp.dot(a, b, preferred_element_type=_F32)


def _rmsnorm(x, g):
    ms = jnp.mean(x * x, axis=-1, keepdims=True)
    return x * lax.rsqrt(ms + NORM_EPS) * g


def _sigmoid(x):
    return 1.0 / (1.0 + jnp.exp(-x))


def _silu(x):
    return x * _sigmoid(x)


def _seg_sum(x, ones_bd):
    xb = x.astype(_BF16)
    return jnp.concatenate(
        [_dot(xb[:, c * SEG_TILE:(c + 1) * SEG_TILE], ones_bd) for c in range(D_MODEL // SEG_TILE)], axis=-1)


def _bcast_rows(st, nt):
    nb, _, c = st.shape
    return jnp.broadcast_to(st, (nb, nt, c)).reshape(nb * nt, c)


def _time_index(nb, nt, c):
    return lax.broadcasted_iota(jnp.int32, (nb * nt, c), 0) % nt


def _slab_start(bl, nt):
    return (bl // GROUP) * (nt * GROUP) + bl % GROUP


def _store_slab(ref, n, val, nb, nt, hp0=0):
    for bl in range(nb):
        rows = val[bl * nt:(bl + 1) * nt]
        for i in range(val.shape[1] // V7X_LANES):
            ref[n, hp0 + i, pl.ds(_slab_start(bl, nt), nt, stride=GROUP), :] = (
                rows[:, i * V7X_LANES:(i + 1) * V7X_LANES])


def _load_slab(ref, nb, nt):
    per_batch = []
    for bl in range(nb):
        per_batch.append(jnp.concatenate(
            [ref[hp, pl.ds(_slab_start(bl, nt), nt, stride=GROUP), :] for hp in range(HEAD_PAIRS)],
            axis=-1))
    return jnp.concatenate(per_batch, axis=0)


def _const_spec(shape, ngrid):
    zeros = (0,) * len(shape)
    if ngrid == 1:
        return pl.BlockSpec(shape, lambda i: zeros, pipeline_mode=pl.Buffered(1))
    return pl.BlockSpec(shape, lambda i, j: zeros, pipeline_mode=pl.Buffered(1))


def _tile_dims(b, t, long_rows=TOKEN_TILE):
    if t >= long_rows // GROUP:
        nb, nt = GROUP, long_rows // GROUP
    else:
        nb, nt = TOKEN_TILE // t, t
    assert b % nb == 0 and t % nt == 0 and (nb == GROUP or nt == t)
    return nb, nt, (b // nb, t // nt)


def _params():
    return pltpu.CompilerParams(dimension_semantics=("arbitrary", "arbitrary"), vmem_limit_bytes=VMEM_LIMIT)


def _mem_kv_kernel(mem_ref, g_ref, w_ref, k_ref, v_ref):
    xn = _rmsnorm(mem_ref[...], g_ref[...]).astype(_BF16)
    k_ref[...] = _dot(xn, w_ref[:, 0:D_MODEL])
    v_ref[...] = _dot(xn, w_ref[:, D_MODEL:2 * D_MODEL])


def _mem_kv(mem2d, g, w_bf):
    n = mem2d.shape[0]
    tm = TOKEN_TILE
    return pl.pallas_call(
        _mem_kv_kernel,
        grid=(n // tm,),
        in_specs=[
            pl.BlockSpec((tm, D_MODEL), lambda i: (i, 0)),
            pl.BlockSpec((1, D_MODEL), lambda i: (0, 0)),
            pl.BlockSpec((D_MODEL, 2 * D_MODEL), lambda i: (0, 0)),
        ],
        out_specs=[
            pl.BlockSpec((tm, D_MODEL), lambda i: (i, 0)),
            pl.BlockSpec((tm, D_MODEL), lambda i: (i, 0)),
        ],
        out_shape=[jax.ShapeDtypeStruct((n, D_MODEL), _F32)] * 2,
        compiler_params=pltpu.CompilerParams(
            dimension_semantics=("arbitrary",), vmem_limit_bytes=VMEM_LIMIT),
        name="mem_kv",
    )(mem2d, g, w_bf)


def _conv_kernel(x_ref, *rest, has_state):
    if has_state:
        st_ref, g_ref, w_ref, cw_ref, cb_ref, wb_ref, acc_ref, new_ref = rest
    else:
        g_ref, w_ref, cw_ref, cb_ref, wb_ref, acc_ref, new_ref, st_ref = rest

        @pl.when(pl.program_id(1) == 0)
        def _():
            st_ref[...] = jnp.zeros_like(st_ref)

    nb, nt, _ = x_ref.shape
    rows = nb * nt
    xn = _rmsnorm(x_ref[...].reshape(rows, D_MODEL), g_ref[...]).astype(_BF16)
    h = _dot(xn, w_ref[:, 0:D_MODEL])
    cg = _dot(xn, w_ref[:, 2 * D_MODEL:3 * D_MODEL])
    u = cg * h
    s0 = _bcast_rows(st_ref[:, 0:1, :], nt)
    s1 = _bcast_rows(st_ref[:, 1:2, :], nt)
    t = _time_index(nb, nt, D_MODEL)
    prev1 = jnp.where(t == 0, s1, pltpu.roll(u, 1, axis=0))
    prev2 = jnp.where(t == 0, s0, jnp.where(t == 1, s1, pltpu.roll(u, 2, axis=0)))
    conv = cb_ref[...] + prev2 * cw_ref[0:1, :] + prev1 * cw_ref[1:2, :] + u * cw_ref[2:3, :]
    bg = _dot(xn, w_ref[:, D_MODEL:2 * D_MODEL])
    gc = _dot(xn, w_ref[:, 3 * D_MODEL:4 * D_MODEL])
    y = bg * conv * _silu(gc)
    mc = _dot(xn, w_ref[:, 4 * D_MODEL:5 * D_MODEL])
    acc_ref[...] = (_sigmoid(mc) * _dot(y.astype(_BF16), wb_ref[...])).reshape(nb, nt, D_MODEL)
    new = u.reshape(nb, nt, D_MODEL)[:, nt - (CONV_TAPS - 1):, :]
    new_ref[...] = new
    if not has_state:
        st_ref[...] = new


def _conv(x, st, g, w_bf, cw, cb, wb_bf):
    b, t, _ = x.shape
    nb, nt, grid = _tile_dims(b, t, WIDE_TOKEN_TILE)
    has_state = st is not None
    tok = pl.BlockSpec((nb, nt, D_MODEL), lambda i, j: (i, j, 0))
    st_spec = pl.BlockSpec((nb, CONV_TAPS - 1, D_MODEL), lambda i, j: (i, 0, 0))
    return pl.pallas_call(
        functools.partial(_conv_kernel, has_state=has_state),
        grid=grid,
        in_specs=[tok] + ([st_spec] if has_state else []) + [
            _const_spec((1, D_MODEL), 2),
            _const_spec((D_MODEL, 5 * D_MODEL), 2),
            _const_spec((CONV_TAPS, D_MODEL), 2),
            _const_spec((1, D_MODEL), 2),
            _const_spec((D_MODEL, D_MODEL), 2),
        ],
        out_specs=[tok, st_spec],
        out_shape=[
            jax.ShapeDtypeStruct((b, t, D_MODEL), _F32),
            jax.ShapeDtypeStruct((b, CONV_TAPS - 1, D_MODEL), _F32),
        ],
        scratch_shapes=[] if has_state else [pltpu.VMEM((nb, CONV_TAPS - 1, D_MODEL), _F32)],
        compiler_params=_params(),
        name="conv_branch",
    )(*([x] + ([st] if has_state else []) + [g, w_bf, cw, cb, wb_bf]))


def _mem_gate(xn, attn, acc_in, w_ref, wb_ref):
    gm = _dot(xn, w_ref[:, D_MODEL:2 * D_MODEL])
    y = attn * _silu(gm)
    mm = _dot(xn, w_ref[:, 2 * D_MODEL:3 * D_MODEL])
    return acc_in + _sigmoid(mm) * _dot(y.astype(_BF16), wb_ref[...])


def _mem_prompt_kernel(x_ref, k_ref, v_ref, acc_in_ref, g_ref, w_ref, wb_ref, acc_ref):
    xn = _rmsnorm(x_ref[...], g_ref[...]).astype(_BF16)
    q = _dot(xn, w_ref[:, 0:D_MODEL]) * (MEM_HEAD_DIM ** -0.5)
    heads = []
    for hh in range(MEM_HEADS):
        sl = slice(hh * MEM_HEAD_DIM, (hh + 1) * MEM_HEAD_DIM)
        qh = q[:, sl].astype(_BF16)
        kh = k_ref[:, sl].astype(_BF16)
        vh = v_ref[:, sl].astype(_BF16)
        s = lax.dot_general(qh, kh, (((1,), (1,)), ((), ())), preferred_element_type=_F32)
        e = jnp.exp(s - jnp.max(s, axis=-1, keepdims=True))
        l = jnp.sum(e, axis=-1, keepdims=True)
        heads.append(_dot(e.astype(_BF16), vh) / l)
    attn = jnp.concatenate(heads, axis=-1)
    acc_ref[...] = _mem_gate(xn, attn, acc_in_ref[...], w_ref, wb_ref)


def _mem_sample_kernel(x_ref, k_ref, v_ref, acc_in_ref, g_ref, w_ref, wb_ref, acc_ref):
    tb, nt, _ = x_ref.shape
    rows = tb * nt
    xn = _rmsnorm(x_ref[...].reshape(rows, D_MODEL), g_ref[...]).astype(_BF16)
    q = (_dot(xn, w_ref[:, 0:D_MODEL]) * (MEM_HEAD_DIM ** -0.5)).reshape(tb, nt, D_MODEL)
    nkh = N_MEM * MEM_HEADS
    q4 = jnp.concatenate(
        [q[:, :, hh * MEM_HEAD_DIM:(hh + 1) * MEM_HEAD_DIM] for hh in range(MEM_HEADS)], axis=1).astype(_BF16)
    k_all = k_ref[...].reshape(tb, nkh, MEM_HEAD_DIM).astype(_BF16)
    v_all = v_ref[...].reshape(tb, nkh, MEM_HEAD_DIM).astype(_BF16)
    s = jnp.einsum("bqd,bkd->bqk", q4, k_all, preferred_element_type=_F32)
    q_head = lax.broadcasted_iota(jnp.int32, s.shape, 1) // nt
    k_head = lax.broadcasted_iota(jnp.int32, s.shape, 2) % MEM_HEADS
    s = jnp.where(q_head == k_head, s, MASKED_SCORE)
    e = jnp.exp(s - jnp.max(s, axis=-1, keepdims=True))
    l = jnp.sum(e, axis=-1, keepdims=True)
    o = jnp.einsum("bqk,bkd->bqd", e.astype(_BF16), v_all, preferred_element_type=_F32) / l
    attn = jnp.concatenate([o[:, hh * nt:(hh + 1) * nt, :] for hh in range(MEM_HEADS)], axis=-1)
    attn = attn.reshape(rows, D_MODEL)
    acc = _mem_gate(xn, attn, acc_in_ref[...].reshape(rows, D_MODEL), w_ref, wb_ref)
    acc_ref[...] = acc.reshape(tb, nt, D_MODEL)


def _mem_prompt(x, mk, mv, acc_in, g, w_bf, wb_bf):
    b, t, _ = x.shape
    tm = min(WIDE_TOKEN_TILE, t)
    tok = pl.BlockSpec((None, tm, D_MODEL), lambda i, j: (i, j, 0))
    kv = pl.BlockSpec((None, N_MEM, D_MODEL), lambda i, j: (i, 0, 0))
    return pl.pallas_call(
        _mem_prompt_kernel,
        grid=(b, t // tm),
        in_specs=[tok, kv, kv, tok, _const_spec((1, D_MODEL), 2), _const_spec((D_MODEL, 3 * D_MODEL), 2),
                  _const_spec((D_MODEL, D_MODEL), 2)],
        out_specs=tok,
        out_shape=jax.ShapeDtypeStruct((b, t, D_MODEL), _F32),
        compiler_params=_params(),
        name="mem_prompt",
    )(x, mk, mv, acc_in, g, w_bf, wb_bf)


def _mem_sample(x, mk, mv, acc_in, g, w_bf, wb_bf):
    b, t, _ = x.shape
    tb = SAMPLE_MEM_BATCH_TILE
    tok = pl.BlockSpec((tb, t, D_MODEL), lambda i: (i, 0, 0))
    kv = pl.BlockSpec((None, tb, N_MEM, MEM_HEADS, MEM_HEAD_DIM), lambda i: (0, i, 0, 0, 0))
    return pl.pallas_call(
        _mem_sample_kernel,
        grid=(b // tb,),
        in_specs=[tok, kv, kv, tok, _const_spec((1, D_MODEL), 1), _const_spec((D_MODEL, 3 * D_MODEL), 1),
                  _const_spec((D_MODEL, D_MODEL), 1)],
        out_specs=tok,
        out_shape=jax.ShapeDtypeStruct((b, t, D_MODEL), _F32),
        compiler_params=pltpu.CompilerParams(
            dimension_semantics=("arbitrary",), vmem_limit_bytes=VMEM_LIMIT),
        name="mem_sample",
    )(x, mk, mv, acc_in, g, w_bf, wb_bf)


def _prep_tile(x_ref, prev_rows, g_ref, w_ref, mu_ref, w0_ref, wup_ref, a0_ref, aup_ref, kk_ref, ka_ref, rk_ref,
               ones_ref, ops_ref, bonus_ref, sg_ref, sm_ref):
    nb, nt, _ = x_ref.shape
    rows = nb * nt
    xn = _rmsnorm(x_ref[...].reshape(rows, D_MODEL), g_ref[...]).astype(_BF16)
    last = {}

    def shifted(lo, n):
        p = _dot(xn, w_ref[:, lo:lo + n])
        last[lo] = p.reshape(nb, nt, n)[:, nt - 1:, :]
        prev = jnp.where(_time_index(nb, nt, n) == 0, _bcast_rows(prev_rows[:, :, lo:lo + n], nt),
                         pltpu.roll(p, 1, axis=0))
        return p + mu_ref[:, lo:lo + n] * (prev - p)

    lora_in = shifted(3 * D_MODEL, 2 * LORA)
    lora_tanh = jnp.tanh(lora_in).astype(_BF16)
    lora_lin = lora_in.astype(_BF16)
    ones = ones_ref[...]
    seg = lambda x: _dot(x.astype(_BF16), ones)
    for c in range(D_MODEL // SEG_TILE):
        sl = slice(c * SEG_TILE, (c + 1) * SEG_TILE)
        hp0 = c * (SEG_TILE // V7X_LANES)
        r = shifted(c * SEG_TILE, SEG_TILE)
        k = shifted(D_MODEL + c * SEG_TILE, SEG_TILE)
        v = shifted(2 * D_MODEL + c * SEG_TILE, SEG_TILE)
        z = w0_ref[:, sl] + _dot(lora_tanh, wup_ref[:, sl])
        softplus = jnp.maximum(-z, 0.0) + jnp.log(1.0 + jnp.exp(-jnp.abs(z)))
        _store_slab(ops_ref, OP_W, jnp.exp(-jnp.exp(-softplus - 0.5)), nb, nt, hp0)
        a = _sigmoid(a0_ref[:, sl] + _dot(lora_lin, aup_ref[:, sl]))
        kk = k * kk_ref[:, sl]
        kk = kk * lax.rsqrt(seg(kk * kk) + KK_EPS)
        kx = k * (1.0 + (a - 1.0) * ka_ref[:, sl])
        _store_slab(ops_ref, OP_R, r, nb, nt, hp0)
        _store_slab(ops_ref, OP_K, kx, nb, nt, hp0)
        _store_slab(ops_ref, OP_V, v, nb, nt, hp0)
        _store_slab(ops_ref, OP_KK, kk, nb, nt, hp0)
        _store_slab(ops_ref, OP_B, kk * a, nb, nt, hp0)
        bonus_ref[:, :, sl] = (seg(r * kx * rk_ref[:, sl]) * v).reshape(nb, nt, SEG_TILE)
    for c in range(D_MODEL // SEG_TILE):
        sl = slice(c * SEG_TILE, (c + 1) * SEG_TILE)
        lo = D_SHIFT + c * SEG_TILE
        sg_ref[:, :, sl] = _silu(_dot(xn, w_ref[:, lo:lo + SEG_TILE])).reshape(nb, nt, SEG_TILE)
        sm_ref[:, :, sl] = _sigmoid(_dot(xn, w_ref[:, lo + D_MODEL:lo + D_MODEL + SEG_TILE])).reshape(
            nb, nt, SEG_TILE)
    return jnp.concatenate([last[lo] for lo in sorted(last)], axis=-1)


def _prep_kernel(x_ref, st_ref, *rest):
    new_ref = rest[-1]
    new_ref[...] = _prep_tile(x_ref, st_ref[...], *rest[:-1])


def _prep_carry_kernel(x_ref, *rest):
    new_ref, st_ref = rest[-2:]

    @pl.when(pl.program_id(1) == 0)
    def _():
        st_ref[...] = jnp.zeros_like(st_ref)

    new = _prep_tile(x_ref, st_ref[...], *rest[:-2])
    st_ref[...] = new
    new_ref[...] = new


def _ops_spec(rows, index):
    return pl.BlockSpec((N_OPS, HEAD_PAIRS, rows, V7X_LANES), lambda i, j: (0, 0, index(i, j), 0))


def _prep(x, st, weights):
    b, t, _ = x.shape
    nb, nt, grid = _tile_dims(b, t)
    has_state = st is not None
    tok = pl.BlockSpec((nb, nt, D_MODEL), lambda i, j: (i, j, 0))
    st_spec = pl.BlockSpec((nb, 1, D_SHIFT), lambda i, j: (i, 0, 0))
    slab = _ops_spec(nb * nt, lambda i, j: i * grid[1] + j)
    vec = _const_spec((1, D_MODEL), 2)
    w_specs = [
        vec,
        _const_spec((D_MODEL, D_SHIFT + 2 * D_MODEL), 2),
        _const_spec((1, D_SHIFT), 2),
        vec,
        _const_spec((2 * LORA, D_MODEL), 2),
        vec,
        _const_spec((2 * LORA, D_MODEL), 2),
        vec, vec, vec,
        _const_spec((SEG_TILE, SEG_TILE), 2),
    ]
    slab_shape = jax.ShapeDtypeStruct((N_OPS, HEAD_PAIRS, b * t, V7X_LANES), _F32)
    tok_shape = jax.ShapeDtypeStruct((b, t, D_MODEL), _F32)
    return pl.pallas_call(
        _prep_kernel if has_state else _prep_carry_kernel,
        grid=grid,
        in_specs=[tok] + ([st_spec] if has_state else []) + w_specs,
        out_specs=[slab] + [tok] * 3 + [st_spec],
        out_shape=[slab_shape] + [tok_shape] * 3 + [jax.ShapeDtypeStruct((b, 1, D_SHIFT), _F32)],
        scratch_shapes=[] if has_state else [pltpu.VMEM((nb, 1, D_SHIFT), _F32)],
        compiler_params=_params(),
        name="rwkv_prep",
    )(*([x] + ([st] if has_state else []) + list(weights)))


def _scan_kernel(ops_ref, *rest, zero_init):
    if zero_init:
        o_ref, sT_ref, s_ref, yt_ref, dec_ref, sc_ref = rest
    else:
        s0_ref, o_ref, sT_ref, s_ref, yt_ref, dec_ref, sc_ref = rest
    nk = RWKV_HEAD
    tt = o_ref.shape[0]
    kb = SCAN_UPDATE_BLOCK

    @pl.when(pl.program_id(1) == 0)
    def _():
        if zero_init:
            s_ref[...] = jnp.zeros_like(s_ref)
        else:
            s_ref[...] = s0_ref[...]

    def to_lanes(q, pair):
        for j in range(SCAN_UNITS_PER_BLOCK):
            u = SCAN_UNITS_PER_BLOCK * q + j
            if isinstance(u, int):
                n, tau = u // 2, u % 2
            else:
                n, tau = lax.shift_right_logical(u, 1), jnp.bitwise_and(u, 1)
            t_dst = 2 * pair + tau
            t_src = jnp.minimum(t_dst, tt - 2 + tau)
            rows = pl.ds(pl.multiple_of(t_src * GROUP, GROUP), GROUP)
            d = jnp.concatenate([ops_ref[n, hp, rows, :] for hp in range(HEAD_PAIRS)], axis=0)
            tr = d.T
            yt_ref[n, t_dst] = jnp.concatenate([tr[0:nk], tr[nk:2 * nk]], axis=1)

    for q in range(2 * N_OPS // SCAN_UNITS_PER_BLOCK):
        to_lanes(q, 0)

    dec_ref[...] = jnp.ones_like(dec_ref)

    def scaled_row(n, k):
        return sc_ref[n, pl.ds(k, 1), :]

    def step(pair, tau):
        t = 2 * pair + tau
        p_prev = dec_ref[...]
        p_now = p_prev * yt_ref[OP_W, t]
        inv = 1.0 / p_now
        sc_ref[SC_KK] = yt_ref[OP_KK, t] * p_prev
        sc_ref[SC_B] = yt_ref[OP_B, t] * inv
        sc_ref[SC_K] = yt_ref[OP_K, t] * inv
        sc_ref[SC_R] = yt_ref[OP_R, t] * p_now
        dec_ref[...] = p_now

        def p1(k, acc):
            return acc + s_ref[k] * scaled_row(SC_KK, k)

        sa = lax.fori_loop(0, nk, p1, jnp.zeros((nk, V7X_LANES), _F32), unroll=SCAN_UNROLL)
        vv = yt_ref[OP_V, t]

        def p2(c, o):
            for kk in range(kb):
                k = c * kb + kk
                s = s_ref[k] - sa * scaled_row(SC_B, k) + vv * scaled_row(SC_K, k)
                s_ref[k] = s
                o = o + s * scaled_row(SC_R, k)
            to_lanes(tau * (nk // kb) + c, pair + 1)
            return o

        o_ref[t] = lax.fori_loop(0, nk // kb, p2, jnp.zeros((nk, V7X_LANES), _F32))

    def step_pair(pair, carry):
        step(pair, 0)
        step(pair, 1)
        return carry

    lax.fori_loop(0, tt // 2, step_pair, 0)

    def restore(k, carry):
        s_ref[k] = s_ref[k] * dec_ref[pl.ds(k, 1), :]
        return carry

    lax.fori_loop(0, nk, restore, 0, unroll=8)

    @pl.when(pl.program_id(1) == pl.num_programs(1) - 1)
    def _():
        sT_ref[...] = s_ref[...]


def _scan(ops, s0, t):
    groups = ops.shape[2] // (t * GROUP)
    tt = min(PROMPT_TIME_TILE, t)
    nt = t // tt
    assert tt % 2 == 0 and RWKV_HEAD % SCAN_UPDATE_BLOCK == 0
    st_spec = pl.BlockSpec((RWKV_HEAD, RWKV_HEAD, V7X_LANES), lambda g, j: (0, 0, g))
    zero_init = s0 is None
    return pl.pallas_call(
        functools.partial(_scan_kernel, zero_init=zero_init),
        grid=(groups, nt),
        in_specs=[_ops_spec(tt * GROUP, lambda g, j: g * nt + j)] + ([] if zero_init else [st_spec]),
        out_specs=[pl.BlockSpec((tt, RWKV_HEAD, V7X_LANES), lambda g, j: (j, 0, g)), st_spec],
        out_shape=[
            jax.ShapeDtypeStruct((t, RWKV_HEAD, groups * V7X_LANES), _F32),
            jax.ShapeDtypeStruct((RWKV_HEAD, RWKV_HEAD, groups * V7X_LANES), _F32),
        ],
        scratch_shapes=[
            pltpu.VMEM((RWKV_HEAD, RWKV_HEAD, V7X_LANES), _F32),
            pltpu.VMEM((N_OPS, tt + 2, RWKV_HEAD, V7X_LANES), _F32),
            pltpu.VMEM((RWKV_HEAD, V7X_LANES), _F32),
            pltpu.VMEM((N_SCALED, RWKV_HEAD, V7X_LANES), _F32),
        ],
        compiler_params=_params(),
        name="wkv_scan",
    )(*([ops] + ([] if zero_init else [s0])))


def _state_to_lanes(s):
    b = s.shape[0]
    s = s.reshape(b // GROUP, GROUP, HEAD_PAIRS, 2, RWKV_HEAD, RWKV_HEAD)
    return s.transpose(5, 4, 0, 3, 2, 1).reshape(RWKV_HEAD, RWKV_HEAD, b * RWKV_HEADS)


def _state_from_lanes(s, b):
    s = s.reshape(RWKV_HEAD, RWKV_HEAD, b // GROUP, 2, HEAD_PAIRS, GROUP)
    return s.transpose(2, 5, 4, 3, 1, 0).reshape(b, RWKV_HEADS, RWKV_HEAD, RWKV_HEAD)


def _lanes_to_slab(o_ref, slab_ref, nb, nt):
    nk = RWKV_HEAD
    half = V7X_LANES // 2
    low = lax.broadcasted_iota(jnp.int32, (nk, V7X_LANES), 1) < half
    for g in range(nb // GROUP):
        lanes = slice(g * V7X_LANES, (g + 1) * V7X_LANES)
        for i in range(nt // 2):
            y0, y1 = o_ref[2 * i, :, lanes], o_ref[2 * i + 1, :, lanes]
            top = jnp.where(low, y0, pltpu.roll(y1, half, axis=1))
            bot = jnp.where(low, pltpu.roll(y0, half, axis=1), y1)
            d = jnp.concatenate([top, bot], axis=0).T
            for tau in (0, 1):
                for hp in range(HEAD_PAIRS):
                    blk = (tau * HEAD_PAIRS + hp) * GROUP
                    row0 = g * nt * GROUP + (2 * i + tau) * GROUP
                    slab_ref[hp, row0:row0 + GROUP, :] = d[blk:blk + GROUP]


def _out_kernel(o_ref, bonus_ref, sg_ref, sm_ref, acc_ref, x_ref, lng_ref, lnb_ref, wb_ref, wo_ref, nf_ref,
                ones_ref, y_ref, slab_ref):
    nb, nt, _ = x_ref.shape
    rows = nb * nt
    tok = lambda ref: ref[...].reshape(rows, D_MODEL)
    ones = ones_ref[...]
    _lanes_to_slab(o_ref, slab_ref, nb, nt)
    o = _load_slab(slab_ref, nb, nt)
    inv_n = 1.0 / RWKV_HEAD
    d = o - _seg_sum(o, ones) * inv_n
    var = _seg_sum(d * d, ones) * inv_n
    on = d * lax.rsqrt(var + GN_EPS) * lng_ref[...] + lnb_ref[...]
    y = (on + tok(bonus_ref)) * tok(sg_ref)
    merged = tok(acc_ref) + tok(sm_ref) * _dot(y.astype(_BF16), wb_ref[...])
    h = tok(x_ref) + _dot(merged.astype(_BF16), wo_ref[...])
    y_ref[...] = _rmsnorm(h, nf_ref[...]).reshape(nb, nt, D_MODEL)


def _out(o, bonus, sg, sm, acc, x, lng, lnb, wb_bf, wo_bf, nf, ones_bd):
    b, t, _ = x.shape
    nb, nt, grid = _tile_dims(b, t, WIDE_TOKEN_TILE)
    tok = pl.BlockSpec((nb, nt, D_MODEL), lambda i, j: (i, j, 0))
    vec = _const_spec((1, D_MODEL), 2)
    mat = _const_spec((D_MODEL, D_MODEL), 2)
    return pl.pallas_call(
        _out_kernel,
        grid=grid,
        in_specs=[pl.BlockSpec((nt, RWKV_HEAD, nb // GROUP * V7X_LANES), lambda i, j: (j, 0, i))] + [tok] * 5 + [
            vec, vec, mat, mat, vec, _const_spec((SEG_TILE, SEG_TILE), 2)],
        out_specs=tok,
        out_shape=jax.ShapeDtypeStruct((b, t, D_MODEL), _F32),
        scratch_shapes=[pltpu.VMEM((HEAD_PAIRS, nb * nt, V7X_LANES), _F32)],
        compiler_params=_params(),
        name="merge_out",
    )(o, bonus, sg, sm, acc, x, lng, lnb, wb_bf, wo_bf, nf, ones_bd)


def _group(x, conv_st, shift_st, wkv_st, mk, mv, mem_fn, g_in, w_conv, w_mem, conv_w, conv_b, wb, prep_w, out_w):
    b, t, _ = x.shape
    acc, conv_new = _conv(x, conv_st, g_in, w_conv, conv_w, conv_b, wb[0])
    acc = mem_fn(x, mk, mv, acc, g_in, w_mem, wb[2])
    ops, bonus, sg, sm, shift_new = _prep(x, shift_st, prep_w)
    s0 = None if wkv_st is None else _state_to_lanes(wkv_st)
    o, s_fin = _scan(ops, s0, t)
    y = _out(o, bonus, sg, sm, acc, x, *out_w)
    return y, conv_new[None], shift_new.reshape(b, D_SHIFT)[None], _state_from_lanes(s_fin, b)[None]


def kernel(x_prompt, x_sample, mem_prompt, cache_mem_k, cache_mem_v, state_conv, state_shift, state_wkv, norm_in, w_in, conv_w, conv_b, shift_mu, decay_w0, decay_up, icl_a0, icl_up, k_k, k_a, r_k, ln_x_g, ln_x_b, norm_mem, w_mem_kv, w_branch, w_out, norm_final):
    assert norm_in.shape[0] == 1, "single-layer step"
    bp = x_prompt.shape[0]
    bs = x_sample.shape[0]

    row = lambda a: a.reshape(1, -1)
    w = w_in[0]
    cols = lambda lo, n: w[:, lo:lo + n]
    w_conv = jnp.concatenate([cols(_OFF_CONV, 4 * D_MODEL), cols(_OFF_MCONV, D_MODEL)], axis=1).astype(_BF16)
    w_mem = jnp.concatenate([cols(_OFF_QMEM, 2 * D_MODEL), cols(_OFF_MMEM, D_MODEL)], axis=1).astype(_BF16)
    w_rwkv = jnp.concatenate([cols(_OFF_SHIFT, D_SHIFT + D_MODEL), cols(_OFF_MRWKV, D_MODEL)], axis=1).astype(_BF16)
    wb = w_branch[0].astype(_BF16)
    wo = w_out[0].astype(_BF16)
    zpad = jnp.zeros((LORA, D_MODEL), _F32)
    wup = jnp.concatenate([decay_up[0], zpad], axis=0).astype(_BF16)
    aup = jnp.concatenate([zpad, icl_up[0]], axis=0).astype(_BF16)
    ones_bd = jnp.kron(jnp.eye(SEG_TILE // RWKV_HEAD, dtype=_F32),
                       jnp.ones((RWKV_HEAD, RWKV_HEAD), _F32)).astype(_BF16)
    g_in = row(norm_in[0])
    prep_w = (g_in, w_rwkv, row(shift_mu[0]), row(decay_w0[0]), wup, row(icl_a0[0]), aup, row(k_k[0]),
              row(k_a[0]), row(r_k[0]), ones_bd)
    out_w = (row(ln_x_g[0]), row(ln_x_b[0]), wb[1], wo, row(norm_final), ones_bd)
    shared = (g_in, w_conv, w_mem, conv_w[0], row(conv_b[0]), wb, prep_w, out_w)

    mk, mv = _mem_kv(mem_prompt.reshape(bp * N_MEM, D_MODEL), row(norm_mem[0]), w_mem_kv[0].astype(_BF16))
    y_p, conv_p, shift_p, wkv_p = _group(
        x_prompt, None, None, None, mk.reshape(bp, N_MEM, D_MODEL), mv.reshape(bp, N_MEM, D_MODEL),
        _mem_prompt, *shared)
    y_s, conv_s, shift_s, wkv_s = _group(
        x_sample, state_conv[0], state_shift[0].reshape(bs, 1, D_SHIFT), state_wkv[0],
        cache_mem_k, cache_mem_v, _mem_sample, *shared)

    kv_shape = (1, bp, N_MEM, MEM_HEADS, MEM_HEAD_DIM)
    return (y_p, y_s, mk.reshape(kv_shape), mv.reshape(kv_shape), conv_p, shift_p, wkv_p,
            conv_s, shift_s, wkv_s)
```

```python
import functools

import jax
import jax.numpy as jnp
from jax import lax
from jax.experimental import pallas as pl
from jax.experimental.pallas import tpu as pltpu

D_MODEL = 1024
N_MEM = 256
RWKV_HEAD = 64
RWKV_HEADS = D_MODEL // RWKV_HEAD
LORA = 64
MEM_HEADS = 4
MEM_HEAD_DIM = D_MODEL // MEM_HEADS
D_SHIFT = 3 * D_MODEL + 2 * LORA
CONV_TAPS = 3
NORM_EPS = 1e-6
GN_EPS = RWKV_HEAD * 1e-5
KK_EPS = 1e-12
MASKED_SCORE = -1e30

_OFF_CONV = 0
_OFF_SHIFT = 4 * D_MODEL
_OFF_GRWKV = _OFF_SHIFT + D_SHIFT
_OFF_QMEM = _OFF_GRWKV + D_MODEL
_OFF_GMEM = _OFF_QMEM + D_MODEL
_OFF_MCONV = _OFF_GMEM + D_MODEL
_OFF_MRWKV = _OFF_MCONV + D_MODEL
_OFF_MMEM = _OFF_MRWKV + D_MODEL

V7X_SUBLANES = 8
V7X_LANES = 128
GROUP = V7X_SUBLANES
HEAD_PAIRS = D_MODEL // V7X_LANES
SEG_TILE = 256
TOKEN_TILE = 256
WIDE_TOKEN_TILE = 512
PROMPT_TIME_TILE = TOKEN_TILE // GROUP
SAMPLE_MEM_BATCH_TILE = 8
SCAN_UNROLL = 32
SCAN_UPDATE_BLOCK = 32
N_OPS = 6
OP_R, OP_W, OP_K, OP_V, OP_KK, OP_B = range(N_OPS)
N_SCALED = 4
SC_KK, SC_B, SC_K, SC_R = range(N_SCALED)
VMEM_LIMIT = 56 * 1024 * 1024

_F32 = jnp.float32
_BF16 = jnp.bfloat16


def _dot(a, b):
    return jnp.dot(a, b, preferred_element_type=_F32)


def _rmsnorm(x, g):
    ms = jnp.mean(x * x, axis=-1, keepdims=True)
    return x * lax.rsqrt(ms + NORM_EPS) * g


def _sigmoid(x):
    return 1.0 / (1.0 + jnp.exp(-x))


def _silu(x):
    return x * _sigmoid(x)


def _seg_sum(x, ones_bd):
    xb = x.astype(_BF16)
    return jnp.concatenate(
        [_dot(xb[:, c * SEG_TILE:(c + 1) * SEG_TILE], ones_bd) for c in range(D_MODEL // SEG_TILE)], axis=-1)


def _bcast_rows(st, nt):
    nb, _, c = st.shape
    return jnp.broadcast_to(st, (nb, nt, c)).reshape(nb * nt, c)


def _time_index(nb, nt, c):
    return lax.broadcasted_iota(jnp.int32, (nb * nt, c), 0) % nt


def _slab_start(bl, nt):
    return (bl // GROUP) * (nt * GROUP) + bl % GROUP


def _store_step_pairs(ref, n, val, nb, nt, hp0, scr_ref):
    rows = nb * nt
    nt2 = nt // 2
    half = V7X_LANES // 2
    for i in range(val.shape[1] // V7X_LANES):
        scr_ref[...] = val[:, i * V7X_LANES:(i + 1) * V7X_LANES]
        if nt2 % V7X_SUBLANES == 0:
            even = scr_ref[pl.ds(0, rows // 2, stride=2), :]
            odd = scr_ref[pl.ds(1, rows // 2, stride=2), :]
            low = lax.broadcasted_iota(jnp.int32, even.shape, 1) < half
            heads = (jnp.where(low, even, pltpu.roll(odd, half, axis=1)),
                     jnp.where(low, pltpu.roll(even, half, axis=1), odd))
            for h2 in range(2):
                for bl in range(nb):
                    ref[n, hp0 + i, h2, pl.ds(_slab_start(bl, nt2), nt2, stride=GROUP), :] = (
                        heads[h2][bl * nt2:(bl + 1) * nt2])
        else:
            low = lax.broadcasted_iota(jnp.int32, (GROUP, V7X_LANES), 1) < half
            for g in range(nb // GROUP):
                for t2 in range(nt2):
                    src = g * GROUP * nt + 2 * t2
                    even = scr_ref[pl.ds(src, GROUP, stride=nt), :]
                    odd = scr_ref[pl.ds(src + 1, GROUP, stride=nt), :]
                    row0 = g * nt2 * GROUP + t2 * GROUP
                    ref[n, hp0 + i, 0, row0:row0 + GROUP, :] = jnp.where(low, even, pltpu.roll(odd, half, axis=1))
                    ref[n, hp0 + i, 1, row0:row0 + GROUP, :] = jnp.where(low, pltpu.roll(even, half, axis=1), odd)


def _load_slab(ref, nb, nt):
    per_batch = []
    for bl in range(nb):
        per_batch.append(jnp.concatenate(
            [ref[hp, pl.ds(_slab_start(bl, nt), nt, stride=GROUP), :] for hp in range(HEAD_PAIRS)],
            axis=-1))
    return jnp.concatenate(per_batch, axis=0)


def _const_spec(shape, ngrid):
    zeros = (0,) * len(shape)
    if ngrid == 1:
        return pl.BlockSpec(shape, lambda i: zeros, pipeline_mode=pl.Buffered(1))
    return pl.BlockSpec(shape, lambda i, j: zeros, pipeline_mode=pl.Buffered(1))


def _tile_dims(b, t, long_rows=TOKEN_TILE):
    if t >= long_rows // GROUP:
        nb, nt = GROUP, long_rows // GROUP
    else:
        nb, nt = TOKEN_TILE // t, t
    assert b % nb == 0 and t % nt == 0 and (nb == GROUP or nt == t)
    return nb, nt, (b // nb, t // nt)


def _params():
    return pltpu.CompilerParams(dimension_semantics=("arbitrary", "arbitrary"), vmem_limit_bytes=VMEM_LIMIT)


def _mem_kv_kernel(mem_ref, g_ref, w_ref, k_ref, v_ref):
    xn = _rmsnorm(mem_ref[...], g_ref[...]).astype(_BF16)
    k_ref[...] = _dot(xn, w_ref[:, 0:D_MODEL])
    v_ref[...] = _dot(xn, w_ref[:, D_MODEL:2 * D_MODEL])


def _mem_kv(mem2d, g, w_bf):
    n = mem2d.shape[0]
    tm = TOKEN_TILE
    return pl.pallas_call(
        _mem_kv_kernel,
        grid=(n // tm,),
        in_specs=[
            pl.BlockSpec((tm, D_MODEL), lambda i: (i, 0)),
            pl.BlockSpec((1, D_MODEL), lambda i: (0, 0)),
            pl.BlockSpec((D_MODEL, 2 * D_MODEL), lambda i: (0, 0)),
        ],
        out_specs=[
            pl.BlockSpec((tm, D_MODEL), lambda i: (i, 0)),
            pl.BlockSpec((tm, D_MODEL), lambda i: (i, 0)),
        ],
        out_shape=[jax.ShapeDtypeStruct((n, D_MODEL), _F32)] * 2,
        compiler_params=pltpu.CompilerParams(
            dimension_semantics=("arbitrary",), vmem_limit_bytes=VMEM_LIMIT),
        name="mem_kv",
    )(mem2d, g, w_bf)


def _conv_kernel(x_ref, *rest, has_state):
    if has_state:
        st_ref, g_ref, w_ref, cw_ref, cb_ref, wb_ref, acc_ref, new_ref = rest
    else:
        g_ref, w_ref, cw_ref, cb_ref, wb_ref, acc_ref, new_ref, st_ref = rest

        @pl.when(pl.program_id(1) == 0)
        def _():
            st_ref[...] = jnp.zeros_like(st_ref)

    nb, nt, _ = x_ref.shape
    rows = nb * nt
    xn = _rmsnorm(x_ref[...].reshape(rows, D_MODEL), g_ref[...]).astype(_BF16)
    h = _dot(xn, w_ref[:, 0:D_MODEL])
    cg = _dot(xn, w_ref[:, 2 * D_MODEL:3 * D_MODEL])
    u = cg * h
    s0 = _bcast_rows(st_ref[:, 0:1, :], nt)
    s1 = _bcast_rows(st_ref[:, 1:2, :], nt)
    t = _time_index(nb, nt, D_MODEL)
    prev1 = jnp.where(t == 0, s1, pltpu.roll(u, 1, axis=0))
    prev2 = jnp.where(t == 0, s0, jnp.where(t == 1, s1, pltpu.roll(u, 2, axis=0)))
    conv = cb_ref[...] + prev2 * cw_ref[0:1, :] + prev1 * cw_ref[1:2, :] + u * cw_ref[2:3, :]
    bg = _dot(xn, w_ref[:, D_MODEL:2 * D_MODEL])
    gc = _dot(xn, w_ref[:, 3 * D_MODEL:4 * D_MODEL])
    y = bg * conv * _silu(gc)
    mc = _dot(xn, w_ref[:, 4 * D_MODEL:5 * D_MODEL])
    acc_ref[...] = (_sigmoid(mc) * _dot(y.astype(_BF16), wb_ref[...])).reshape(nb, nt, D_MODEL)
    new = u.reshape(nb, nt, D_MODEL)[:, nt - (CONV_TAPS - 1):, :]
    new_ref[...] = new
    if not has_state:
        st_ref[...] = new


def _conv(x, st, g, w_bf, cw, cb, wb_bf):
    b, t, _ = x.shape
    nb, nt, grid = _tile_dims(b, t, WIDE_TOKEN_TILE)
    has_state = st is not None
    tok = pl.BlockSpec((nb, nt, D_MODEL), lambda i, j: (i, j, 0))
    st_spec = pl.BlockSpec((nb, CONV_TAPS - 1, D_MODEL), lambda i, j: (i, 0, 0))
    return pl.pallas_call(
        functools.partial(_conv_kernel, has_state=has_state),
        grid=grid,
        in_specs=[tok] + ([st_spec] if has_state else []) + [
            _const_spec((1, D_MODEL), 2),
            _const_spec((D_MODEL, 5 * D_MODEL), 2),
            _const_spec((CONV_TAPS, D_MODEL), 2),
            _const_spec((1, D_MODEL), 2),
            _const_spec((D_MODEL, D_MODEL), 2),
        ],
        out_specs=[tok, st_spec],
        out_shape=[
            jax.ShapeDtypeStruct((b, t, D_MODEL), _F32),
            jax.ShapeDtypeStruct((b, CONV_TAPS - 1, D_MODEL), _F32),
        ],
        scratch_shapes=[] if has_state else [pltpu.VMEM((nb, CONV_TAPS - 1, D_MODEL), _F32)],
        compiler_params=_params(),
        name="conv_branch",
    )(*([x] + ([st] if has_state else []) + [g, w_bf, cw, cb, wb_bf]))


def _mem_gate(xn, attn, acc_in, w_ref, wb_ref):
    gm = _dot(xn, w_ref[:, D_MODEL:2 * D_MODEL])
    y = attn * _silu(gm)
    mm = _dot(xn, w_ref[:, 2 * D_MODEL:3 * D_MODEL])
    return acc_in + _sigmoid(mm) * _dot(y.astype(_BF16), wb_ref[...])


def _mem_prompt_kernel(x_ref, k_ref, v_ref, acc_in_ref, g_ref, w_ref, wb_ref, acc_ref):
    xn = _rmsnorm(x_ref[...], g_ref[...]).astype(_BF16)
    q = _dot(xn, w_ref[:, 0:D_MODEL]) * (MEM_HEAD_DIM ** -0.5)
    heads = []
    for hh in range(MEM_HEADS):
        sl = slice(hh * MEM_HEAD_DIM, (hh + 1) * MEM_HEAD_DIM)
        qh = q[:, sl].astype(_BF16)
        kh = k_ref[:, sl].astype(_BF16)
        vh = v_ref[:, sl].astype(_BF16)
        s = lax.dot_general(qh, kh, (((1,), (1,)), ((), ())), preferred_element_type=_F32)
        e = jnp.exp(s - jnp.max(s, axis=-1, keepdims=True))
        l = jnp.sum(e, axis=-1, keepdims=True)
        heads.append(_dot(e.astype(_BF16), vh) / l)
    attn = jnp.concatenate(heads, axis=-1)
    acc_ref[...] = _mem_gate(xn, attn, acc_in_ref[...], w_ref, wb_ref)


def _mem_sample_kernel(x_ref, k_ref, v_ref, acc_in_ref, g_ref, w_ref, wb_ref, acc_ref):
    tb, nt, _ = x_ref.shape
    rows = tb * nt
    xn = _rmsnorm(x_ref[...].reshape(rows, D_MODEL), g_ref[...]).astype(_BF16)
    q = (_dot(xn, w_ref[:, 0:D_MODEL]) * (MEM_HEAD_DIM ** -0.5)).reshape(tb, nt, D_MODEL)
    nkh = N_MEM * MEM_HEADS
    q4 = jnp.concatenate(
        [q[:, :, hh * MEM_HEAD_DIM:(hh + 1) * MEM_HEAD_DIM] for hh in range(MEM_HEADS)], axis=1).astype(_BF16)
    k_all = k_ref[...].reshape(tb, nkh, MEM_HEAD_DIM).astype(_BF16)
    v_all = v_ref[...].reshape(tb, nkh, MEM_HEAD_DIM).astype(_BF16)
    s = jnp.einsum("bqd,bkd->bqk", q4, k_all, preferred_element_type=_F32)
    q_head = lax.broadcasted_iota(jnp.int32, s.shape, 1) // nt
    k_head = lax.broadcasted_iota(jnp.int32, s.shape, 2) % MEM_HEADS
    s = jnp.where(q_head == k_head, s, MASKED_SCORE)
    e = jnp.exp(s - jnp.max(s, axis=-1, keepdims=True))
    l = jnp.sum(e, axis=-1, keepdims=True)
    o = jnp.einsum("bqk,bkd->bqd", e.astype(_BF16), v_all, preferred_element_type=_F32) / l
    attn = jnp.concatenate([o[:, hh * nt:(hh + 1) * nt, :] for hh in range(MEM_HEADS)], axis=-1)
    attn = attn.reshape(rows, D_MODEL)
    acc = _mem_gate(xn, attn, acc_in_ref[...].reshape(rows, D_MODEL), w_ref, wb_ref)
    acc_ref[...] = acc.reshape(tb, nt, D_MODEL)


def _mem_prompt(x, mk, mv, acc_in, g, w_bf, wb_bf):
    b, t, _ = x.shape
    tm = min(WIDE_TOKEN_TILE, t)
    tok = pl.BlockSpec((None, tm, D_MODEL), lambda i, j: (i, j, 0))
    kv = pl.BlockSpec((None, N_MEM, D_MODEL), lambda i, j: (i, 0, 0))
    return pl.pallas_call(
        _mem_prompt_kernel,
        grid=(b, t // tm),
        in_specs=[tok, kv, kv, tok, _const_spec((1, D_MODEL), 2), _const_spec((D_MODEL, 3 * D_MODEL), 2),
                  _const_spec((D_MODEL, D_MODEL), 2)],
        out_specs=tok,
        out_shape=jax.ShapeDtypeStruct((b, t, D_MODEL), _F32),
        compiler_params=_params(),
        name="mem_prompt",
    )(x, mk, mv, acc_in, g, w_bf, wb_bf)


def _mem_sample(x, mk, mv, acc_in, g, w_bf, wb_bf):
    b, t, _ = x.shape
    tb = SAMPLE_MEM_BATCH_TILE
    tok = pl.BlockSpec((tb, t, D_MODEL), lambda i: (i, 0, 0))
    kv = pl.BlockSpec((None, tb, N_MEM, MEM_HEADS, MEM_HEAD_DIM), lambda i: (0, i, 0, 0, 0))
    return pl.pallas_call(
        _mem_sample_kernel,
        grid=(b // tb,),
        in_specs=[tok, kv, kv, tok, _const_spec((1, D_MODEL), 1), _const_spec((D_MODEL, 3 * D_MODEL), 1),
                  _const_spec((D_MODEL, D_MODEL), 1)],
        out_specs=tok,
        out_shape=jax.ShapeDtypeStruct((b, t, D_MODEL), _F32),
        compiler_params=pltpu.CompilerParams(
            dimension_semantics=("arbitrary",), vmem_limit_bytes=VMEM_LIMIT),
        name="mem_sample",
    )(x, mk, mv, acc_in, g, w_bf, wb_bf)


def _prep_tile(x_ref, prev_rows, scr_ref, g_ref, w_ref, mu_ref, w0_ref, wup_ref, a0_ref, aup_ref, kk_ref, ka_ref,
               rk_ref, ones_ref, ops_ref, bonus_ref, sg_ref, sm_ref):
    nb, nt, _ = x_ref.shape
    rows = nb * nt
    xn = _rmsnorm(x_ref[...].reshape(rows, D_MODEL), g_ref[...]).astype(_BF16)
    last = {}

    def shifted(lo, n):
        p = _dot(xn, w_ref[:, lo:lo + n])
        last[lo] = p.reshape(nb, nt, n)[:, nt - 1:, :]
        prev = jnp.where(_time_index(nb, nt, n) == 0, _bcast_rows(prev_rows[:, :, lo:lo + n], nt),
                         pltpu.roll(p, 1, axis=0))
        return p + mu_ref[:, lo:lo + n] * (prev - p)

    lora_in = shifted(3 * D_MODEL, 2 * LORA)
    lora_tanh = jnp.tanh(lora_in).astype(_BF16)
    lora_lin = lora_in.astype(_BF16)
    ones = ones_ref[...]
    seg = lambda x: _dot(x.astype(_BF16), ones)
    for c in range(D_MODEL // SEG_TILE):
        sl = slice(c * SEG_TILE, (c + 1) * SEG_TILE)
        hp0 = c * (SEG_TILE // V7X_LANES)
        r = shifted(c * SEG_TILE, SEG_TILE)
        k = shifted(D_MODEL + c * SEG_TILE, SEG_TILE)
        v = shifted(2 * D_MODEL + c * SEG_TILE, SEG_TILE)
        z = w0_ref[:, sl] + _dot(lora_tanh, wup_ref[:, sl])
        softplus = jnp.maximum(-z, 0.0) + jnp.log(1.0 + jnp.exp(-jnp.abs(z)))
        store = lambda n, val: _store_step_pairs(ops_ref, n, val, nb, nt, hp0, scr_ref)
        store(OP_W, jnp.exp(-jnp.exp(-softplus - 0.5)))
        a = _sigmoid(a0_ref[:, sl] + _dot(lora_lin, aup_ref[:, sl]))
        kk = k * kk_ref[:, sl]
        kk = kk * lax.rsqrt(seg(kk * kk) + KK_EPS)
        kx = k * (1.0 + (a - 1.0) * ka_ref[:, sl])
        store(OP_R, r)
        store(OP_K, kx)
        store(OP_V, v)
        store(OP_KK, kk)
        store(OP_B, kk * a)
        bonus_ref[:, :, sl] = (seg(r * kx * rk_ref[:, sl]) * v).reshape(nb, nt, SEG_TILE)
    for c in range(D_MODEL // SEG_TILE):
        sl = slice(c * SEG_TILE, (c + 1) * SEG_TILE)
        lo = D_SHIFT + c * SEG_TILE
        sg_ref[:, :, sl] = _silu(_dot(xn, w_ref[:, lo:lo + SEG_TILE])).reshape(nb, nt, SEG_TILE)
        sm_ref[:, :, sl] = _sigmoid(_dot(xn, w_ref[:, lo + D_MODEL:lo + D_MODEL + SEG_TILE])).reshape(
            nb, nt, SEG_TILE)
    return jnp.concatenate([last[lo] for lo in sorted(last)], axis=-1)


def _prep_kernel(x_ref, st_ref, *rest):
    new_ref, scr_ref = rest[-2:]
    new_ref[...] = _prep_tile(x_ref, st_ref[...], scr_ref, *rest[:-2])


def _prep_carry_kernel(x_ref, *rest):
    new_ref, scr_ref, st_ref = rest[-3:]

    @pl.when(pl.program_id(1) == 0)
    def _():
        st_ref[...] = jnp.zeros_like(st_ref)

    new = _prep_tile(x_ref, st_ref[...], scr_ref, *rest[:-3])
    st_ref[...] = new
    new_ref[...] = new


def _ops_spec(steps, index):
    return pl.BlockSpec((N_OPS, HEAD_PAIRS, 2, steps // 2 * GROUP, V7X_LANES),
                        lambda i, j: (0, 0, 0, index(i, j), 0))


def _prep(x, st, weights):
    b, t, _ = x.shape
    nb, nt, grid = _tile_dims(b, t)
    has_state = st is not None
    tok = pl.BlockSpec((nb, nt, D_MODEL), lambda i, j: (i, j, 0))
    st_spec = pl.BlockSpec((nb, 1, D_SHIFT), lambda i, j: (i, 0, 0))
    slab = _ops_spec(nb // GROUP * nt, lambda i, j: i * grid[1] + j)
    vec = _const_spec((1, D_MODEL), 2)
    w_specs = [
        vec,
        _const_spec((D_MODEL, D_SHIFT + 2 * D_MODEL), 2),
        _const_spec((1, D_SHIFT), 2),
        vec,
        _const_spec((2 * LORA, D_MODEL), 2),
        vec,
        _const_spec((2 * LORA, D_MODEL), 2),
        vec, vec, vec,
        _const_spec((SEG_TILE, SEG_TILE), 2),
    ]
    slab_shape = jax.ShapeDtypeStruct((N_OPS, HEAD_PAIRS, 2, b * t // 2, V7X_LANES), _F32)
    tok_shape = jax.ShapeDtypeStruct((b, t, D_MODEL), _F32)
    stage = pltpu.VMEM((nb * nt, V7X_LANES), _F32)
    return pl.pallas_call(
        _prep_kernel if has_state else _prep_carry_kernel,
        grid=grid,
        in_specs=[tok] + ([st_spec] if has_state else []) + w_specs,
        out_specs=[slab] + [tok] * 3 + [st_spec],
        out_shape=[slab_shape] + [tok_shape] * 3 + [jax.ShapeDtypeStruct((b, 1, D_SHIFT), _F32)],
        scratch_shapes=[stage] + ([] if has_state else [pltpu.VMEM((nb, 1, D_SHIFT), _F32)]),
        compiler_params=_params(),
        name="rwkv_prep",
    )(*([x] + ([st] if has_state else []) + list(weights)))


def _scan_kernel(ops_ref, *rest, zero_init):
    if zero_init:
        o_ref, sT_ref, s_ref, yt_ref, dec_ref, sc_ref = rest
    else:
        s0_ref, o_ref, sT_ref, s_ref, yt_ref, dec_ref, sc_ref = rest
    nk = RWKV_HEAD
    tt = o_ref.shape[0]
    kb = SCAN_UPDATE_BLOCK

    @pl.when(pl.program_id(1) == 0)
    def _():
        if zero_init:
            s_ref[...] = jnp.zeros_like(s_ref)
        else:
            s_ref[...] = s0_ref[...]

    def to_lanes(n, pair):
        src = jnp.minimum(pair, tt // 2 - 1)
        rows = pl.ds(pl.multiple_of(src * GROUP, GROUP), GROUP)
        d = jnp.concatenate(
            [ops_ref[n, hp, h2, rows, :] for h2 in range(2) for hp in range(HEAD_PAIRS)], axis=0)
        tr = d.T
        yt_ref[n, 2 * pair] = tr[0:nk]
        yt_ref[n, 2 * pair + 1] = tr[nk:2 * nk]

    for n in range(N_OPS):
        to_lanes(n, 0)

    dec_ref[...] = jnp.ones_like(dec_ref)

    def scaled_row(n, k):
        return sc_ref[n, pl.ds(k, 1), :]

    def step(pair, tau):
        t = 2 * pair + tau
        p_prev = dec_ref[...]
        p_now = p_prev * yt_ref[OP_W, t]
        inv = 1.0 / p_now
        sc_ref[SC_KK] = yt_ref[OP_KK, t] * p_prev
        sc_ref[SC_B] = yt_ref[OP_B, t] * inv
        sc_ref[SC_K] = yt_ref[OP_K, t] * inv
        sc_ref[SC_R] = yt_ref[OP_R, t] * p_now
        dec_ref[...] = p_now

        def p1(k, acc):
            return acc + s_ref[k] * scaled_row(SC_KK, k)

        sa = lax.fori_loop(0, nk, p1, jnp.zeros((nk, V7X_LANES), _F32), unroll=SCAN_UNROLL)
        vv = yt_ref[OP_V, t]

        def p2(c, o):
            for kk in range(kb):
                k = c * kb + kk
                s = s_ref[k] - sa * scaled_row(SC_B, k) + vv * scaled_row(SC_K, k)
                s_ref[k] = s
                o = o + s * scaled_row(SC_R, k)
            for n in ((2 * c, 2 * c + 1) if tau == 0 else (4 + c,)):
                to_lanes(n, pair + 1)
            return o

        o_ref[t] = lax.fori_loop(0, nk // kb, p2, jnp.zeros((nk, V7X_LANES), _F32))

    def step_pair(pair, carry):
        step(pair, 0)
        step(pair, 1)
        return carry

    lax.fori_loop(0, tt // 2, step_pair, 0)

    def restore(k, carry):
        s_ref[k] = s_ref[k] * dec_ref[pl.ds(k, 1), :]
        return carry

    lax.fori_loop(0, nk, restore, 0, unroll=8)

    @pl.when(pl.program_id(1) == pl.num_programs(1) - 1)
    def _():
        sT_ref[...] = s_ref[...]


def _scan(ops, s0, t):
    groups = ops.shape[3] // (t // 2 * GROUP)
    tt = min(PROMPT_TIME_TILE, t)
    nt = t // tt
    assert tt % 2 == 0 and RWKV_HEAD // SCAN_UPDATE_BLOCK == 2 and N_OPS == 6
    st_spec = pl.BlockSpec((RWKV_HEAD, RWKV_HEAD, V7X_LANES), lambda g, j: (0, 0, g))
    zero_init = s0 is None
    return pl.pallas_call(
        functools.partial(_scan_kernel, zero_init=zero_init),
        grid=(groups, nt),
        in_specs=[_ops_spec(tt, lambda g, j: g * nt + j)] + ([] if zero_init else [st_spec]),
        out_specs=[pl.BlockSpec((tt, RWKV_HEAD, V7X_LANES), lambda g, j: (j, 0, g)), st_spec],
        out_shape=[
            jax.ShapeDtypeStruct((t, RWKV_HEAD, groups * V7X_LANES), _F32),
            jax.ShapeDtypeStruct((RWKV_HEAD, RWKV_HEAD, groups * V7X_LANES), _F32),
        ],
        scratch_shapes=[
            pltpu.VMEM((RWKV_HEAD, RWKV_HEAD, V7X_LANES), _F32),
            pltpu.VMEM((N_OPS, tt + 2, RWKV_HEAD, V7X_LANES), _F32),
            pltpu.VMEM((RWKV_HEAD, V7X_LANES), _F32),
            pltpu.VMEM((N_SCALED, RWKV_HEAD, V7X_LANES), _F32),
        ],
        compiler_params=_params(),
        name="wkv_scan",
    )(*([ops] + ([] if zero_init else [s0])))


def _state_to_lanes(s):
    b = s.shape[0]
    s = s.reshape(b // GROUP, GROUP, HEAD_PAIRS, 2, RWKV_HEAD, RWKV_HEAD)
    return s.transpose(5, 4, 0, 3, 2, 1).reshape(RWKV_HEAD, RWKV_HEAD, b * RWKV_HEADS)


def _state_from_lanes(s, b):
    s = s.reshape(RWKV_HEAD, RWKV_HEAD, b // GROUP, 2, HEAD_PAIRS, GROUP)
    return s.transpose(2, 5, 4, 3, 1, 0).reshape(b, RWKV_HEADS, RWKV_HEAD, RWKV_HEAD)


def _lanes_to_slab(o_ref, slab_ref, nb, nt):
    nk = RWKV_HEAD
    half = V7X_LANES // 2
    low = lax.broadcasted_iota(jnp.int32, (nk, V7X_LANES), 1) < half
    for g in range(nb // GROUP):
        lanes = slice(g * V7X_LANES, (g + 1) * V7X_LANES)
        for i in range(nt // 2):
            y0, y1 = o_ref[2 * i, :, lanes], o_ref[2 * i + 1, :, lanes]
            top = jnp.where(low, y0, pltpu.roll(y1, half, axis=1))
            bot = jnp.where(low, pltpu.roll(y0, half, axis=1), y1)
            d = jnp.concatenate([top, bot], axis=0).T
            for tau in (0, 1):
                for hp in range(HEAD_PAIRS):
                    blk = (tau * HEAD_PAIRS + hp) * GROUP
                    row0 = g * nt * GROUP + (2 * i + tau) * GROUP
                    slab_ref[hp, row0:row0 + GROUP, :] = d[blk:blk + GROUP]


def _out_kernel(o_ref, bonus_ref, sg_ref, sm_ref, acc_ref, x_ref, lng_ref, lnb_ref, wb_ref, wo_ref, nf_ref,
                ones_ref, y_ref, slab_ref):
    nb, nt, _ = x_ref.shape
    rows = nb * nt
    tok = lambda ref: ref[...].reshape(rows, D_MODEL)
    ones = ones_ref[...]
    _lanes_to_slab(o_ref, slab_ref, nb, nt)
    o = _load_slab(slab_ref, nb, nt)
    inv_n = 1.0 / RWKV_HEAD
    d = o - _seg_sum(o, ones) * inv_n
    var = _seg_sum(d * d, ones) * inv_n
    on = d * lax.rsqrt(var + GN_EPS) * lng_ref[...] + lnb_ref[...]
    y = (on + tok(bonus_ref)) * tok(sg_ref)
    merged = tok(acc_ref) + tok(sm_ref) * _dot(y.astype(_BF16), wb_ref[...])
    h = tok(x_ref) + _dot(merged.astype(_BF16), wo_ref[...])
    y_ref[...] = _rmsnorm(h, nf_ref[...]).reshape(nb, nt, D_MODEL)


def _out(o, bonus, sg, sm, acc, x, lng, lnb, wb_bf, wo_bf, nf, ones_bd):
    b, t, _ = x.shape
    nb, nt, grid = _tile_dims(b, t, WIDE_TOKEN_TILE)
    tok = pl.BlockSpec((nb, nt, D_MODEL), lambda i, j: (i, j, 0))
    vec = _const_spec((1, D_MODEL), 2)
    mat = _const_spec((D_MODEL, D_MODEL), 2)
    return pl.pallas_call(
        _out_kernel,
        grid=grid,
        in_specs=[pl.BlockSpec((nt, RWKV_HEAD, nb // GROUP * V7X_LANES), lambda i, j: (j, 0, i))] + [tok] * 5 + [
            vec, vec, mat, mat, vec, _const_spec((SEG_TILE, SEG_TILE), 2)],
        out_specs=tok,
        out_shape=jax.ShapeDtypeStruct((b, t, D_MODEL), _F32),
        scratch_shapes=[pltpu.VMEM((HEAD_PAIRS, nb * nt, V7X_LANES), _F32)],
        compiler_params=_params(),
        name="merge_out",
    )(o, bonus, sg, sm, acc, x, lng, lnb, wb_bf, wo_bf, nf, ones_bd)


def _group(x, conv_st, shift_st, wkv_st, mk, mv, mem_fn, g_in, w_conv, w_mem, conv_w, conv_b, wb, prep_w, out_w):
    b, t, _ = x.shape
    acc, conv_new = _conv(x, conv_st, g_in, w_conv, conv_w, conv_b, wb[0])
    acc = mem_fn(x, mk, mv, acc, g_in, w_mem, wb[2])
    ops, bonus, sg, sm, shift_new = _prep(x, shift_st, prep_w)
    s0 = None if wkv_st is None else _state_to_lanes(wkv_st)
    o, s_fin = _scan(ops, s0, t)
    y = _out(o, bonus, sg, sm, acc, x, *out_w)
    return y, conv_new[None], shift_new.reshape(b, D_SHIFT)[None], _state_from_lanes(s_fin, b)[None]


def kernel(x_prompt, x_sample, mem_prompt, cache_mem_k, cache_mem_v, state_conv, state_shift, state_wkv, norm_in, w_in, conv_w, conv_b, shift_mu, decay_w0, decay_up, icl_a0, icl_up, k_k, k_a, r_k, ln_x_g, ln_x_b, norm_mem, w_mem_kv, w_branch, w_out, norm_final):
    assert norm_in.shape[0] == 1, "single-layer step"
    bp = x_prompt.shape[0]
    bs = x_sample.shape[0]

    row = lambda a: a.reshape(1, -1)
    w = w_in[0]
    cols = lambda lo, n: w[:, lo:lo + n]
    w_conv = jnp.concatenate([cols(_OFF_CONV, 4 * D_MODEL), cols(_OFF_MCONV, D_MODEL)], axis=1).astype(_BF16)
    w_mem = jnp.concatenate([cols(_OFF_QMEM, 2 * D_MODEL), cols(_OFF_MMEM, D_MODEL)], axis=1).astype(_BF16)
    w_rwkv = jnp.concatenate([cols(_OFF_SHIFT, D_SHIFT + D_MODEL), cols(_OFF_MRWKV, D_MODEL)], axis=1).astype(_BF16)
    wb = w_branch[0].astype(_BF16)
    wo = w_out[0].astype(_BF16)
    zpad = jnp.zeros((LORA, D_MODEL), _F32)
    wup = jnp.concatenate([decay_up[0], zpad], axis=0).astype(_BF16)
    aup = jnp.concatenate([zpad, icl_up[0]], axis=0).astype(_BF16)
    ones_bd = jnp.kron(jnp.eye(SEG_TILE // RWKV_HEAD, dtype=_F32),
                       jnp.ones((RWKV_HEAD, RWKV_HEAD), _F32)).astype(_BF16)
    g_in = row(norm_in[0])
    prep_w = (g_in, w_rwkv, row(shift_mu[0]), row(decay_w0[0]), wup, row(icl_a0[0]), aup, row(k_k[0]),
              row(k_a[0]), row(r_k[0]), ones_bd)
    out_w = (row(ln_x_g[0]), row(ln_x_b[0]), wb[1], wo, row(norm_final), ones_bd)
    shared = (g_in, w_conv, w_mem, conv_w[0], row(conv_b[0]), wb, prep_w, out_w)

    mk, mv = _mem_kv(mem_prompt.reshape(bp * N_MEM, D_MODEL), row(norm_mem[0]), w_mem_kv[0].astype(_BF16))
    y_p, conv_p, shift_p, wkv_p = _group(
        x_prompt, None, None, None, mk.reshape(bp, N_MEM, D_MODEL), mv.reshape(bp, N_MEM, D_MODEL),
        _mem_prompt, *shared)
    y_s, conv_s, shift_s, wkv_s = _group(
        x_sample, state_conv[0], state_shift[0].reshape(bs, 1, D_SHIFT), state_wkv[0],
        cache_mem_k, cache_mem_v, _mem_sample, *shared)

    kv_shape = (1, bp, N_MEM, MEM_HEADS, MEM_HEAD_DIM)
    return (y_p, y_s, mk.reshape(kv_shape), mv.reshape(kv_shape), conv_p, shift_p, wkv_p,
            conv_s, shift_s, wkv_s)
```

```python
import functools

import jax
import jax.numpy as jnp
from jax import lax
from jax.experimental import pallas as pl
from jax.experimental.pallas import tpu as pltpu

D_MODEL = 1024
N_MEM = 256
RWKV_HEAD = 64
RWKV_HEADS = D_MODEL // RWKV_HEAD
LORA = 64
MEM_HEADS = 4
MEM_HEAD_DIM = D_MODEL // MEM_HEADS
D_SHIFT = 3 * D_MODEL + 2 * LORA
CONV_TAPS = 3
NORM_EPS = 1e-6
GN_EPS = RWKV_HEAD * 1e-5
KK_EPS = 1e-12
MASKED_SCORE = -1e30

_OFF_CONV = 0
_OFF_SHIFT = 4 * D_MODEL
_OFF_GRWKV = _OFF_SHIFT + D_SHIFT
_OFF_QMEM = _OFF_GRWKV + D_MODEL
_OFF_GMEM = _OFF_QMEM + D_MODEL
_OFF_MCONV = _OFF_GMEM + D_MODEL
_OFF_MRWKV = _OFF_MCONV + D_MODEL
_OFF_MMEM = _OFF_MRWKV + D_MODEL

V7X_SUBLANES = 8
V7X_LANES = 128
GROUP = V7X_SUBLANES
HEAD_PAIRS = D_MODEL // V7X_LANES
SEG_TILE = 256
TOKEN_TILE = 256
WIDE_TOKEN_TILE = 512
PROMPT_TIME_TILE = TOKEN_TILE // GROUP
SAMPLE_MEM_BATCH_TILE = 8
SCAN_UNROLL = 64
SCAN_UPDATE_BLOCK = 32
N_OPS = 6
OP_R, OP_W, OP_K, OP_V, OP_KK, OP_B = range(N_OPS)
N_SCALED = 4
SC_KK, SC_B, SC_K, SC_R = range(N_SCALED)
VMEM_LIMIT = 56 * 1024 * 1024

_F32 = jnp.float32
_BF16 = jnp.bfloat16


def _dot(a, b):
    return jnp.dot(a, b, preferred_element_type=_F32)


def _rmsnorm(x, g):
    ms = jnp.mean(x * x, axis=-1, keepdims=True)
    return x * lax.rsqrt(ms + NORM_EPS) * g


def _sigmoid(x):
    return 1.0 / (1.0 + jnp.exp(-x))


def _silu(x):
    return x * _sigmoid(x)


def _seg_sum(x, ones_bd):
    xb = x.astype(_BF16)
    return jnp.concatenate(
        [_dot(xb[:, c * SEG_TILE:(c + 1) * SEG_TILE], ones_bd) for c in range(D_MODEL // SEG_TILE)], axis=-1)


def _bcast_rows(st, nt):
    nb, _, c = st.shape
    return jnp.broadcast_to(st, (nb, nt, c)).reshape(nb * nt, c)


def _time_index(nb, nt, c):
    return lax.broadcasted_iota(jnp.int32, (nb * nt, c), 0) % nt


def _slab_start(bl, nt):
    return (bl // GROUP) * (nt * GROUP) + bl % GROUP


def _store_step_pairs(ref, n, val, nb, nt, hp0, scr_ref):
    rows = nb * nt
    nt2 = nt // 2
    half = V7X_LANES // 2
    for i in range(val.shape[1] // V7X_LANES):
        scr_ref[...] = val[:, i * V7X_LANES:(i + 1) * V7X_LANES]
        if nt2 % V7X_SUBLANES == 0:
            even = scr_ref[pl.ds(0, rows // 2, stride=2), :]
            odd = scr_ref[pl.ds(1, rows // 2, stride=2), :]
            low = lax.broadcasted_iota(jnp.int32, even.shape, 1) < half
            heads = (jnp.where(low, even, pltpu.roll(odd, half, axis=1)),
                     jnp.where(low, pltpu.roll(even, half, axis=1), odd))
            for h2 in range(2):
                for bl in range(nb):
                    ref[n, hp0 + i, h2, pl.ds(_slab_start(bl, nt2), nt2, stride=GROUP), :] = (
                        heads[h2][bl * nt2:(bl + 1) * nt2])
        else:
            low = lax.broadcasted_iota(jnp.int32, (GROUP, V7X_LANES), 1) < half
            for g in range(nb // GROUP):
                for t2 in range(nt2):
                    src = g * GROUP * nt + 2 * t2
                    even = scr_ref[pl.ds(src, GROUP, stride=nt), :]
                    odd = scr_ref[pl.ds(src + 1, GROUP, stride=nt), :]
                    row0 = g * nt2 * GROUP + t2 * GROUP
                    ref[n, hp0 + i, 0, row0:row0 + GROUP, :] = jnp.where(low, even, pltpu.roll(odd, half, axis=1))
                    ref[n, hp0 + i, 1, row0:row0 + GROUP, :] = jnp.where(low, pltpu.roll(even, half, axis=1), odd)


def _load_slab(ref, nb, nt):
    per_batch = []
    for bl in range(nb):
        per_batch.append(jnp.concatenate(
            [ref[hp, pl.ds(_slab_start(bl, nt), nt, stride=GROUP), :] for hp in range(HEAD_PAIRS)],
            axis=-1))
    return jnp.concatenate(per_batch, axis=0)


def _const_spec(shape, ngrid):
    zeros = (0,) * len(shape)
    if ngrid == 1:
        return pl.BlockSpec(shape, lambda i: zeros, pipeline_mode=pl.Buffered(1))
    return pl.BlockSpec(shape, lambda i, j: zeros, pipeline_mode=pl.Buffered(1))


def _w_spec(offset, width, ngrid):
    shape = (pl.Element(D_MODEL), pl.Element(width))
    if ngrid == 1:
        return pl.BlockSpec(shape, lambda i: (0, offset), pipeline_mode=pl.Buffered(1))
    return pl.BlockSpec(shape, lambda i, j: (0, offset), pipeline_mode=pl.Buffered(1))


def _tile_dims(b, t, long_rows=TOKEN_TILE):
    if t >= long_rows // GROUP:
        nb, nt = GROUP, long_rows // GROUP
    else:
        nb, nt = TOKEN_TILE // t, t
    assert b % nb == 0 and t % nt == 0 and (nb == GROUP or nt == t)
    return nb, nt, (b // nb, t // nt)


def _params():
    return pltpu.CompilerParams(dimension_semantics=("arbitrary", "arbitrary"), vmem_limit_bytes=VMEM_LIMIT)


def _mem_kv_kernel(mem_ref, g_ref, w_ref, k_ref, v_ref, k4_ref, v4_ref):
    xn = _rmsnorm(mem_ref[...], g_ref[...]).astype(_BF16)
    for col0, flat_ref, head_ref in ((0, k_ref, k4_ref), (D_MODEL, v_ref, v4_ref)):
        val = _dot(xn, w_ref[:, col0:col0 + D_MODEL])
        flat_ref[...] = val
        for hh in range(MEM_HEADS):
            head_ref[:, hh, :] = val[:, hh * MEM_HEAD_DIM:(hh + 1) * MEM_HEAD_DIM]


def _mem_kv(mem, g, w_bf):
    b = mem.shape[0]
    flat = pl.BlockSpec((None, N_MEM, D_MODEL), lambda i: (i, 0, 0))
    heads = pl.BlockSpec((None, N_MEM, MEM_HEADS, MEM_HEAD_DIM), lambda i: (i, 0, 0, 0))
    return pl.pallas_call(
        _mem_kv_kernel,
        grid=(b,),
        in_specs=[
            flat,
            pl.BlockSpec((1, D_MODEL), lambda i: (0, 0)),
            pl.BlockSpec((D_MODEL, 2 * D_MODEL), lambda i: (0, 0)),
        ],
        out_specs=[flat, flat, heads, heads],
        out_shape=[jax.ShapeDtypeStruct((b, N_MEM, D_MODEL), _F32)] * 2 + [
            jax.ShapeDtypeStruct((b, N_MEM, MEM_HEADS, MEM_HEAD_DIM), _F32)] * 2,
        compiler_params=pltpu.CompilerParams(
            dimension_semantics=("arbitrary",), vmem_limit_bytes=VMEM_LIMIT),
        name="mem_kv",
    )(mem, g, w_bf)


def _conv_kernel(x_ref, *rest, has_state):
    if has_state:
        st_ref, g_ref, w_ref, wg_ref, cw_ref, cb_ref, wb_ref, acc_ref, new_ref = rest
    else:
        g_ref, w_ref, wg_ref, cw_ref, cb_ref, wb_ref, acc_ref, new_ref, st_ref = rest

        @pl.when(pl.program_id(1) == 0)
        def _():
            st_ref[...] = jnp.zeros_like(st_ref)

    nb, nt, _ = x_ref.shape
    rows = nb * nt
    xn = _rmsnorm(x_ref[...].reshape(rows, D_MODEL), g_ref[...]).astype(_BF16)
    h = _dot(xn, w_ref[:, 0:D_MODEL])
    cg = _dot(xn, w_ref[:, 2 * D_MODEL:3 * D_MODEL])
    u = cg * h
    s0 = _bcast_rows(st_ref[:, 0:1, :], nt)
    s1 = _bcast_rows(st_ref[:, 1:2, :], nt)
    t = _time_index(nb, nt, D_MODEL)
    prev1 = jnp.where(t == 0, s1, pltpu.roll(u, 1, axis=0))
    prev2 = jnp.where(t == 0, s0, jnp.where(t == 1, s1, pltpu.roll(u, 2, axis=0)))
    conv = cb_ref[...] + prev2 * cw_ref[0:1, :] + prev1 * cw_ref[1:2, :] + u * cw_ref[2:3, :]
    bg = _dot(xn, w_ref[:, D_MODEL:2 * D_MODEL])
    gc = _dot(xn, w_ref[:, 3 * D_MODEL:4 * D_MODEL])
    y = bg * conv * _silu(gc)
    mc = _dot(xn, wg_ref[...])
    acc_ref[...] = (_sigmoid(mc) * _dot(y.astype(_BF16), wb_ref[...])).reshape(nb, nt, D_MODEL)
    new = u.reshape(nb, nt, D_MODEL)[:, nt - (CONV_TAPS - 1):, :]
    new_ref[...] = new
    if not has_state:
        st_ref[...] = new


def _conv(x, st, g, w_bf, cw, cb, wb_bf):
    b, t, _ = x.shape
    nb, nt, grid = _tile_dims(b, t, WIDE_TOKEN_TILE)
    has_state = st is not None
    tok = pl.BlockSpec((nb, nt, D_MODEL), lambda i, j: (i, j, 0))
    st_spec = pl.BlockSpec((nb, CONV_TAPS - 1, D_MODEL), lambda i, j: (i, 0, 0))
    return pl.pallas_call(
        functools.partial(_conv_kernel, has_state=has_state),
        grid=grid,
        in_specs=[tok] + ([st_spec] if has_state else []) + [
            _const_spec((1, D_MODEL), 2),
            _w_spec(_OFF_CONV, 4 * D_MODEL, 2),
            _w_spec(_OFF_MCONV, D_MODEL, 2),
            _const_spec((CONV_TAPS, D_MODEL), 2),
            _const_spec((1, D_MODEL), 2),
            _const_spec((D_MODEL, D_MODEL), 2),
        ],
        out_specs=[tok, st_spec],
        out_shape=[
            jax.ShapeDtypeStruct((b, t, D_MODEL), _F32),
            jax.ShapeDtypeStruct((b, CONV_TAPS - 1, D_MODEL), _F32),
        ],
        scratch_shapes=[] if has_state else [pltpu.VMEM((nb, CONV_TAPS - 1, D_MODEL), _F32)],
        compiler_params=_params(),
        name="conv_branch",
    )(*([x] + ([st] if has_state else []) + [g, w_bf, w_bf, cw, cb, wb_bf]))


def _mem_gate(xn, attn, acc_in, w_ref, wg_ref, wb_ref):
    gm = _dot(xn, w_ref[:, D_MODEL:2 * D_MODEL])
    y = attn * _silu(gm)
    mm = _dot(xn, wg_ref[...])
    return acc_in + _sigmoid(mm) * _dot(y.astype(_BF16), wb_ref[...])


def _mem_prompt_kernel(x_ref, k_ref, v_ref, acc_in_ref, g_ref, w_ref, wg_ref, wb_ref, acc_ref):
    xn = _rmsnorm(x_ref[...], g_ref[...]).astype(_BF16)
    q = _dot(xn, w_ref[:, 0:D_MODEL]) * (MEM_HEAD_DIM ** -0.5)
    heads = []
    for hh in range(MEM_HEADS):
        sl = slice(hh * MEM_HEAD_DIM, (hh + 1) * MEM_HEAD_DIM)
        qh = q[:, sl].astype(_BF16)
        kh = k_ref[:, sl].astype(_BF16)
        vh = v_ref[:, sl].astype(_BF16)
        s = lax.dot_general(qh, kh, (((1,), (1,)), ((), ())), preferred_element_type=_F32)
        e = jnp.exp(s - jnp.max(s, axis=-1, keepdims=True))
        l = jnp.sum(e, axis=-1, keepdims=True)
        heads.append(_dot(e.astype(_BF16), vh) / l)
    attn = jnp.concatenate(heads, axis=-1)
    acc_ref[...] = _mem_gate(xn, attn, acc_in_ref[...], w_ref, wg_ref, wb_ref)


def _mem_sample_kernel(x_ref, k_ref, v_ref, acc_in_ref, g_ref, w_ref, wg_ref, wb_ref, acc_ref):
    tb, nt, _ = x_ref.shape
    rows = tb * nt
    xn = _rmsnorm(x_ref[...].reshape(rows, D_MODEL), g_ref[...]).astype(_BF16)
    q = (_dot(xn, w_ref[:, 0:D_MODEL]) * (MEM_HEAD_DIM ** -0.5)).reshape(tb, nt, D_MODEL)
    nkh = N_MEM * MEM_HEADS
    q4 = jnp.concatenate(
        [q[:, :, hh * MEM_HEAD_DIM:(hh + 1) * MEM_HEAD_DIM] for hh in range(MEM_HEADS)], axis=1).astype(_BF16)
    k_all = k_ref[...].reshape(tb, nkh, MEM_HEAD_DIM).astype(_BF16)
    v_all = v_ref[...].reshape(tb, nkh, MEM_HEAD_DIM).astype(_BF16)
    s = jnp.einsum("bqd,bkd->bqk", q4, k_all, preferred_element_type=_F32)
    q_head = lax.broadcasted_iota(jnp.int32, s.shape, 1) // nt
    k_head = lax.broadcasted_iota(jnp.int32, s.shape, 2) % MEM_HEADS
    s = jnp.where(q_head == k_head, s, MASKED_SCORE)
    e = jnp.exp(s - jnp.max(s, axis=-1, keepdims=True))
    l = jnp.sum(e, axis=-1, keepdims=True)
    o = jnp.einsum("bqk,bkd->bqd", e.astype(_BF16), v_all, preferred_element_type=_F32) / l
    attn = jnp.concatenate([o[:, hh * nt:(hh + 1) * nt, :] for hh in range(MEM_HEADS)], axis=-1)
    attn = attn.reshape(rows, D_MODEL)
    acc = _mem_gate(xn, attn, acc_in_ref[...].reshape(rows, D_MODEL), w_ref, wg_ref, wb_ref)
    acc_ref[...] = acc.reshape(tb, nt, D_MODEL)


def _mem_prompt(x, mk, mv, acc_in, g, w_bf, wb_bf):
    b, t, _ = x.shape
    tm = min(WIDE_TOKEN_TILE, t)
    tok = pl.BlockSpec((None, tm, D_MODEL), lambda i, j: (i, j, 0))
    kv = pl.BlockSpec((None, N_MEM, D_MODEL), lambda i, j: (i, 0, 0))
    return pl.pallas_call(
        _mem_prompt_kernel,
        grid=(b, t // tm),
        in_specs=[tok, kv, kv, tok, _const_spec((1, D_MODEL), 2), _w_spec(_OFF_QMEM, 2 * D_MODEL, 2),
                  _w_spec(_OFF_MMEM, D_MODEL, 2), _const_spec((D_MODEL, D_MODEL), 2)],
        out_specs=tok,
        out_shape=jax.ShapeDtypeStruct((b, t, D_MODEL), _F32),
        compiler_params=_params(),
        name="mem_prompt",
    )(x, mk, mv, acc_in, g, w_bf, w_bf, wb_bf)


def _mem_sample(x, mk, mv, acc_in, g, w_bf, wb_bf):
    b, t, _ = x.shape
    tb = SAMPLE_MEM_BATCH_TILE
    tok = pl.BlockSpec((tb, t, D_MODEL), lambda i: (i, 0, 0))
    kv = pl.BlockSpec((None, tb, N_MEM, MEM_HEADS, MEM_HEAD_DIM), lambda i: (0, i, 0, 0, 0))
    return pl.pallas_call(
        _mem_sample_kernel,
        grid=(b // tb,),
        in_specs=[tok, kv, kv, tok, _const_spec((1, D_MODEL), 1), _w_spec(_OFF_QMEM, 2 * D_MODEL, 1),
                  _w_spec(_OFF_MMEM, D_MODEL, 1), _const_spec((D_MODEL, D_MODEL), 1)],
        out_specs=tok,
        out_shape=jax.ShapeDtypeStruct((b, t, D_MODEL), _F32),
        compiler_params=pltpu.CompilerParams(
            dimension_semantics=("arbitrary",), vmem_limit_bytes=VMEM_LIMIT),
        name="mem_sample",
    )(x, mk, mv, acc_in, g, w_bf, w_bf, wb_bf)


def _prep_tile(x_ref, prev_rows, scr_ref, g_ref, w_ref, wg_ref, mu_ref, w0_ref, wup_ref, a0_ref, aup_ref, kk_ref,
               ka_ref, rk_ref, ones_ref, ops_ref, bonus_ref, sg_ref, sm_ref):
    nb, nt, _ = x_ref.shape
    rows = nb * nt
    xn = _rmsnorm(x_ref[...].reshape(rows, D_MODEL), g_ref[...]).astype(_BF16)
    last = {}

    def shifted(lo, n):
        p = _dot(xn, w_ref[:, lo:lo + n])
        last[lo] = p.reshape(nb, nt, n)[:, nt - 1:, :]
        prev = jnp.where(_time_index(nb, nt, n) == 0, _bcast_rows(prev_rows[:, :, lo:lo + n], nt),
                         pltpu.roll(p, 1, axis=0))
        return p + mu_ref[:, lo:lo + n] * (prev - p)

    lora_in = shifted(3 * D_MODEL, 2 * LORA)
    lora_tanh = jnp.tanh(lora_in).astype(_BF16)
    lora_lin = lora_in.astype(_BF16)
    ones = ones_ref[...]
    seg = lambda x: _dot(x.astype(_BF16), ones)
    for c in range(D_MODEL // SEG_TILE):
        sl = slice(c * SEG_TILE, (c + 1) * SEG_TILE)
        hp0 = c * (SEG_TILE // V7X_LANES)
        r = shifted(c * SEG_TILE, SEG_TILE)
        k = shifted(D_MODEL + c * SEG_TILE, SEG_TILE)
        v = shifted(2 * D_MODEL + c * SEG_TILE, SEG_TILE)
        z = w0_ref[:, sl] + _dot(lora_tanh, wup_ref[:, sl])
        softplus = jnp.maximum(-z, 0.0) + jnp.log(1.0 + jnp.exp(-jnp.abs(z)))
        store = lambda n, val: _store_step_pairs(ops_ref, n, val, nb, nt, hp0, scr_ref)
        store(OP_W, jnp.exp(-jnp.exp(-softplus - 0.5)))
        a = _sigmoid(a0_ref[:, sl] + _dot(lora_lin, aup_ref[:, sl]))
        kk = k * kk_ref[:, sl]
        kk = kk * lax.rsqrt(seg(kk * kk) + KK_EPS)
        kx = k * (1.0 + (a - 1.0) * ka_ref[:, sl])
        store(OP_R, r)
        store(OP_K, kx)
        store(OP_V, v)
        store(OP_KK, kk)
        store(OP_B, kk * a)
        bonus_ref[:, :, sl] = (seg(r * kx * rk_ref[:, sl]) * v).reshape(nb, nt, SEG_TILE)
    for c in range(D_MODEL // SEG_TILE):
        sl = slice(c * SEG_TILE, (c + 1) * SEG_TILE)
        lo = D_SHIFT + c * SEG_TILE
        sg_ref[:, :, sl] = _silu(_dot(xn, w_ref[:, lo:lo + SEG_TILE])).reshape(nb, nt, SEG_TILE)
        sm_ref[:, :, sl] = _sigmoid(_dot(xn, wg_ref[:, sl])).reshape(nb, nt, SEG_TILE)
    return jnp.concatenate([last[lo] for lo in sorted(last)], axis=-1)


def _prep_kernel(x_ref, st_ref, *rest):
    new_ref, scr_ref = rest[-2:]
    new_ref[...] = _prep_tile(x_ref, st_ref[...], scr_ref, *rest[:-2])


def _prep_carry_kernel(x_ref, *rest):
    new_ref, scr_ref, st_ref = rest[-3:]

    @pl.when(pl.program_id(1) == 0)
    def _():
        st_ref[...] = jnp.zeros_like(st_ref)

    new = _prep_tile(x_ref, st_ref[...], scr_ref, *rest[:-3])
    st_ref[...] = new
    new_ref[...] = new


def _ops_spec(steps, index):
    return pl.BlockSpec((N_OPS, HEAD_PAIRS, 2, steps // 2 * GROUP, V7X_LANES),
                        lambda i, j: (0, 0, 0, index(i, j), 0))


def _prep(x, st, weights):
    b, t, _ = x.shape
    nb, nt, grid = _tile_dims(b, t)
    has_state = st is not None
    tok = pl.BlockSpec((nb, nt, D_MODEL), lambda i, j: (i, j, 0))
    st_spec = pl.BlockSpec((nb, 1, D_SHIFT), lambda i, j: (i, 0, 0))
    slab = _ops_spec(nb // GROUP * nt, lambda i, j: i * grid[1] + j)
    vec = _const_spec((1, D_MODEL), 2)
    w_specs = [
        vec,
        _w_spec(_OFF_SHIFT, D_SHIFT + D_MODEL, 2),
        _w_spec(_OFF_MRWKV, D_MODEL, 2),
        _const_spec((1, D_SHIFT), 2),
        vec,
        _const_spec((2 * LORA, D_MODEL), 2),
        vec,
        _const_spec((2 * LORA, D_MODEL), 2),
        vec, vec, vec,
        _const_spec((SEG_TILE, SEG_TILE), 2),
    ]
    slab_shape = jax.ShapeDtypeStruct((N_OPS, HEAD_PAIRS, 2, b * t // 2, V7X_LANES), _F32)
    tok_shape = jax.ShapeDtypeStruct((b, t, D_MODEL), _F32)
    stage = pltpu.VMEM((nb * nt, V7X_LANES), _F32)
    return pl.pallas_call(
        _prep_kernel if has_state else _prep_carry_kernel,
        grid=grid,
        in_specs=[tok] + ([st_spec] if has_state else []) + w_specs,
        out_specs=[slab] + [tok] * 3 + [st_spec],
        out_shape=[slab_shape] + [tok_shape] * 3 + [jax.ShapeDtypeStruct((b, 1, D_SHIFT), _F32)],
        scratch_shapes=[stage] + ([] if has_state else [pltpu.VMEM((nb, 1, D_SHIFT), _F32)]),
        compiler_params=_params(),
        name="rwkv_prep",
    )(*([x] + ([st] if has_state else []) + list(weights)))


def _scan_kernel(ops_ref, *rest, zero_init):
    if zero_init:
        o_ref, sT_ref, s_ref, yt_ref, dec_ref, sc_ref = rest
    else:
        s0_ref, o_ref, sT_ref, s_ref, yt_ref, dec_ref, sc_ref = rest
    nk = RWKV_HEAD
    tt = o_ref.shape[0]
    kb = SCAN_UPDATE_BLOCK

    @pl.when(pl.program_id(1) == 0)
    def _():
        if zero_init:
            s_ref[...] = jnp.zeros_like(s_ref)
        else:
            s_ref[...] = s0_ref[...]

    def to_lanes(n, pair):
        src = jnp.minimum(pair, tt // 2 - 1)
        rows = pl.ds(pl.multiple_of(src * GROUP, GROUP), GROUP)
        d = jnp.concatenate(
            [ops_ref[n, hp, h2, rows, :] for h2 in range(2) for hp in range(HEAD_PAIRS)], axis=0)
        tr = d.T
        yt_ref[n, 2 * pair] = tr[0:nk]
        yt_ref[n, 2 * pair + 1] = tr[nk:2 * nk]

    for n in range(N_OPS):
        to_lanes(n, 0)

    dec_ref[...] = jnp.ones_like(dec_ref)

    def scaled_row(n, k):
        return sc_ref[n, pl.ds(k, 1), :]

    def step(pair, tau):
        t = 2 * pair + tau
        p_prev = dec_ref[...]
        p_now = p_prev * yt_ref[OP_W, t]
        inv = 1.0 / p_now
        sc_ref[SC_KK] = yt_ref[OP_KK, t] * p_prev
        sc_ref[SC_B] = yt_ref[OP_B, t] * inv
        sc_ref[SC_K] = yt_ref[OP_K, t] * inv
        sc_ref[SC_R] = yt_ref[OP_R, t] * p_now
        dec_ref[...] = p_now

        def p1(k, acc):
            return acc + s_ref[k] * scaled_row(SC_KK, k)

        sa = lax.fori_loop(0, nk, p1, jnp.zeros((nk, V7X_LANES), _F32), unroll=SCAN_UNROLL)
        vv = yt_ref[OP_V, t]

        def p2(c, o):
            for kk in range(kb):
                k = c * kb + kk
                s = s_ref[k] - sa * scaled_row(SC_B, k) + vv * scaled_row(SC_K, k)
                s_ref[k] = s
                o = o + s * scaled_row(SC_R, k)
            for n in ((2 * c, 2 * c + 1) if tau == 0 else (4 + c,)):
                to_lanes(n, pair + 1)
            return o

        o_ref[t] = lax.fori_loop(0, nk // kb, p2, jnp.zeros((nk, V7X_LANES), _F32))

    def step_pair(pair, carry):
        step(pair, 0)
        step(pair, 1)
        return carry

    lax.fori_loop(0, tt // 2, step_pair, 0)

    def restore(k, carry):
        s_ref[k] = s_ref[k] * dec_ref[pl.ds(k, 1), :]
        return carry

    lax.fori_loop(0, nk, restore, 0, unroll=8)

    @pl.when(pl.program_id(1) == pl.num_programs(1) - 1)
    def _():
        sT_ref[...] = s_ref[...]


def _scan(ops, s0, t):
    groups = ops.shape[3] // (t // 2 * GROUP)
    tt = min(PROMPT_TIME_TILE, t)
    nt = t // tt
    assert tt % 2 == 0 and RWKV_HEAD // SCAN_UPDATE_BLOCK == 2 and N_OPS == 6
    st_spec = pl.BlockSpec((RWKV_HEAD, RWKV_HEAD, V7X_LANES), lambda g, j: (0, 0, g))
    zero_init = s0 is None
    return pl.pallas_call(
        functools.partial(_scan_kernel, zero_init=zero_init),
        grid=(groups, nt),
        in_specs=[_ops_spec(tt, lambda g, j: g * nt + j)] + ([] if zero_init else [st_spec]),
        out_specs=[pl.BlockSpec((tt, RWKV_HEAD, V7X_LANES), lambda g, j: (j, 0, g)), st_spec],
        out_shape=[
            jax.ShapeDtypeStruct((t, RWKV_HEAD, groups * V7X_LANES), _F32),
            jax.ShapeDtypeStruct((RWKV_HEAD, RWKV_HEAD, groups * V7X_LANES), _F32),
        ],
        scratch_shapes=[
            pltpu.VMEM((RWKV_HEAD, RWKV_HEAD, V7X_LANES), _F32),
            pltpu.VMEM((N_OPS, tt + 2, RWKV_HEAD, V7X_LANES), _F32),
            pltpu.VMEM((RWKV_HEAD, V7X_LANES), _F32),
            pltpu.VMEM((N_SCALED, RWKV_HEAD, V7X_LANES), _F32),
        ],
        compiler_params=_params(),
        name="wkv_scan",
    )(*([ops] + ([] if zero_init else [s0])))


def _state_to_lanes(s):
    b = s.shape[0]
    s = s.reshape(b // GROUP, GROUP, HEAD_PAIRS, 2, RWKV_HEAD, RWKV_HEAD)
    return s.transpose(5, 4, 0, 3, 2, 1).reshape(RWKV_HEAD, RWKV_HEAD, b * RWKV_HEADS)


def _state_from_lanes(s, b):
    s = s.reshape(RWKV_HEAD, RWKV_HEAD, b // GROUP, 2, HEAD_PAIRS, GROUP)
    return s.transpose(2, 5, 4, 3, 1, 0).reshape(b, RWKV_HEADS, RWKV_HEAD, RWKV_HEAD)


def _lanes_to_slab(o_ref, slab_ref, nb, nt):
    nk = RWKV_HEAD
    half = V7X_LANES // 2
    low = lax.broadcasted_iota(jnp.int32, (nk, V7X_LANES), 1) < half
    for g in range(nb // GROUP):
        lanes = slice(g * V7X_LANES, (g + 1) * V7X_LANES)
        for i in range(nt // 2):
            y0, y1 = o_ref[2 * i, :, lanes], o_ref[2 * i + 1, :, lanes]
            top = jnp.where(low, y0, pltpu.roll(y1, half, axis=1))
            bot = jnp.where(low, pltpu.roll(y0, half, axis=1), y1)
            d = jnp.concatenate([top, bot], axis=0).T
            for tau in (0, 1):
                for hp in range(HEAD_PAIRS):
                    blk = (tau * HEAD_PAIRS + hp) * GROUP
                    row0 = g * nt * GROUP + (2 * i + tau) * GROUP
                    slab_ref[hp, row0:row0 + GROUP, :] = d[blk:blk + GROUP]


def _out_kernel(o_ref, bonus_ref, sg_ref, sm_ref, acc_ref, x_ref, lng_ref, lnb_ref, wb_ref, wo_ref, nf_ref,
                ones_ref, y_ref, slab_ref):
    nb, nt, _ = x_ref.shape
    rows = nb * nt
    tok = lambda ref: ref[...].reshape(rows, D_MODEL)
    ones = ones_ref[...]
    _lanes_to_slab(o_ref, slab_ref, nb, nt)
    o = _load_slab(slab_ref, nb, nt)
    inv_n = 1.0 / RWKV_HEAD
    d = o - _seg_sum(o, ones) * inv_n
    var = _seg_sum(d * d, ones) * inv_n
    on = d * lax.rsqrt(var + GN_EPS) * lng_ref[...] + lnb_ref[...]
    y = (on + tok(bonus_ref)) * tok(sg_ref)
    merged = tok(acc_ref) + tok(sm_ref) * _dot(y.astype(_BF16), wb_ref[...])
    h = tok(x_ref) + _dot(merged.astype(_BF16), wo_ref[...])
    y_ref[...] = _rmsnorm(h, nf_ref[...]).reshape(nb, nt, D_MODEL)


def _out(o, bonus, sg, sm, acc, x, lng, lnb, wb_bf, wo_bf, nf, ones_bd):
    b, t, _ = x.shape
    nb, nt, grid = _tile_dims(b, t, WIDE_TOKEN_TILE)
    tok = pl.BlockSpec((nb, nt, D_MODEL), lambda i, j: (i, j, 0))
    vec = _const_spec((1, D_MODEL), 2)
    mat = _const_spec((D_MODEL, D_MODEL), 2)
    return pl.pallas_call(
        _out_kernel,
        grid=grid,
        in_specs=[pl.BlockSpec((nt, RWKV_HEAD, nb // GROUP * V7X_LANES), lambda i, j: (j, 0, i))] + [tok] * 5 + [
            vec, vec, mat, mat, vec, _const_spec((SEG_TILE, SEG_TILE), 2)],
        out_specs=tok,
        out_shape=jax.ShapeDtypeStruct((b, t, D_MODEL), _F32),
        scratch_shapes=[pltpu.VMEM((HEAD_PAIRS, nb * nt, V7X_LANES), _F32)],
        compiler_params=_params(),
        name="merge_out",
    )(o, bonus, sg, sm, acc, x, lng, lnb, wb_bf, wo_bf, nf, ones_bd)


def _group(x, conv_st, shift_st, wkv_st, mk, mv, mem_fn, g_in, w_all, conv_w, conv_b, wb, prep_w, out_w):
    b, t, _ = x.shape
    acc, conv_new = _conv(x, conv_st, g_in, w_all, conv_w, conv_b, wb[0])
    acc = mem_fn(x, mk, mv, acc, g_in, w_all, wb[2])
    ops, bonus, sg, sm, shift_new = _prep(x, shift_st, prep_w)
    s0 = None if wkv_st is None else _state_to_lanes(wkv_st)
    o, s_fin = _scan(ops, s0, t)
    y = _out(o, bonus, sg, sm, acc, x, *out_w)
    return y, conv_new[None], shift_new.reshape(b, D_SHIFT)[None], _state_from_lanes(s_fin, b)[None]


def kernel(x_prompt, x_sample, mem_prompt, cache_mem_k, cache_mem_v, state_conv, state_shift, state_wkv, norm_in, w_in, conv_w, conv_b, shift_mu, decay_w0, decay_up, icl_a0, icl_up, k_k, k_a, r_k, ln_x_g, ln_x_b, norm_mem, w_mem_kv, w_branch, w_out, norm_final):
    assert norm_in.shape[0] == 1, "single-layer step"
    bp = x_prompt.shape[0]
    bs = x_sample.shape[0]

    row = lambda a: a.reshape(1, -1)
    w_all = w_in[0].astype(_BF16)
    wb = w_branch[0].astype(_BF16)
    wo = w_out[0].astype(_BF16)
    zpad = jnp.zeros((LORA, D_MODEL), _F32)
    wup = jnp.concatenate([decay_up[0], zpad], axis=0).astype(_BF16)
    aup = jnp.concatenate([zpad, icl_up[0]], axis=0).astype(_BF16)
    ones_bd = jnp.kron(jnp.eye(SEG_TILE // RWKV_HEAD, dtype=_F32),
                       jnp.ones((RWKV_HEAD, RWKV_HEAD), _F32)).astype(_BF16)
    g_in = row(norm_in[0])
    prep_w = (g_in, w_all, w_all, row(shift_mu[0]), row(decay_w0[0]), wup, row(icl_a0[0]), aup, row(k_k[0]),
              row(k_a[0]), row(r_k[0]), ones_bd)
    out_w = (row(ln_x_g[0]), row(ln_x_b[0]), wb[1], wo, row(norm_final), ones_bd)
    shared = (g_in, w_all, conv_w[0], row(conv_b[0]), wb, prep_w, out_w)

    mk, mv, mk_heads, mv_heads = _mem_kv(mem_prompt, row(norm_mem[0]), w_mem_kv[0].astype(_BF16))
    y_p, conv_p, shift_p, wkv_p = _group(x_prompt, None, None, None, mk, mv, _mem_prompt, *shared)
    y_s, conv_s, shift_s, wkv_s = _group(
        x_sample, state_conv[0], state_shift[0].reshape(bs, 1, D_SHIFT), state_wkv[0],
        cache_mem_k, cache_mem_v, _mem_sample, *shared)

    return (y_p, y_s, mk_heads[None], mv_heads[None], conv_p, shift_p, wkv_p, conv_s, shift_s, wkv_s)
```

```python
import functools
import math

import jax
import jax.numpy as jnp
from jax import lax
from jax.experimental import pallas as pl
from jax.experimental.pallas import tpu as pltpu

D_MODEL = 1024
N_MEM = 256
RWKV_HEAD = 64
RWKV_HEADS = D_MODEL // RWKV_HEAD
LORA = 64
MEM_HEADS = 4
MEM_HEAD_DIM = D_MODEL // MEM_HEADS
D_SHIFT = 3 * D_MODEL + 2 * LORA
CONV_TAPS = 3
NORM_EPS = 1e-6
GN_EPS = RWKV_HEAD * 1e-5
KK_EPS = 1e-12
DECAY_RATE = math.exp(-0.5)
MASKED_SCORE = -1e30

_OFF_CONV = 0
_OFF_SHIFT = 4 * D_MODEL
_OFF_GRWKV = _OFF_SHIFT + D_SHIFT
_OFF_QMEM = _OFF_GRWKV + D_MODEL
_OFF_GMEM = _OFF_QMEM + D_MODEL
_OFF_MCONV = _OFF_GMEM + D_MODEL
_OFF_MRWKV = _OFF_MCONV + D_MODEL
_OFF_MMEM = _OFF_MRWKV + D_MODEL

V7X_SUBLANES = 8
V7X_LANES = 128
GROUP = V7X_SUBLANES
HEAD_PAIRS = D_MODEL // V7X_LANES
SEG_TILE = 256
TOKEN_TILE = 256
WIDE_TOKEN_TILE = 512
PROMPT_TIME_TILE = TOKEN_TILE // GROUP
SAMPLE_MEM_BATCH_TILE = 8
SCAN_UNROLL = 64
SCAN_UPDATE_BLOCK = 32
N_OPS = 6
OP_R, OP_W, OP_K, OP_V, OP_KK, OP_B = range(N_OPS)
N_SCALED = 4
SC_KK, SC_B, SC_K, SC_R = range(N_SCALED)
VMEM_LIMIT = 56 * 1024 * 1024

_F32 = jnp.float32
_BF16 = jnp.bfloat16


def _dot(a, b):
    return jnp.dot(a, b, preferred_element_type=_F32)


def _rmsnorm(x, g):
    ms = jnp.mean(x * x, axis=-1, keepdims=True)
    return x * lax.rsqrt(ms + NORM_EPS) * g


def _sigmoid(x):
    return 0.5 * jnp.tanh(0.5 * x) + 0.5


def _silu(x):
    return x * _sigmoid(x)


def _seg_sum(x, ones_bd):
    xb = x.astype(_BF16)
    return jnp.concatenate(
        [_dot(xb[:, c * SEG_TILE:(c + 1) * SEG_TILE], ones_bd) for c in range(D_MODEL // SEG_TILE)], axis=-1)


def _bcast_rows(st, nt):
    nb, _, c = st.shape
    return jnp.broadcast_to(st, (nb, nt, c)).reshape(nb * nt, c)


def _time_index(nb, nt, c):
    return lax.broadcasted_iota(jnp.int32, (nb * nt, c), 0) % nt


def _slab_start(bl, nt):
    return (bl // GROUP) * (nt * GROUP) + bl % GROUP


def _store_step_pairs(ref, n, val, nb, nt, hp0, scr_ref):
    rows = nb * nt
    nt2 = nt // 2
    half = V7X_LANES // 2
    for i in range(val.shape[1] // V7X_LANES):
        scr_ref[...] = val[:, i * V7X_LANES:(i + 1) * V7X_LANES]
        if nt2 % V7X_SUBLANES == 0:
            even = scr_ref[pl.ds(0, rows // 2, stride=2), :]
            odd = scr_ref[pl.ds(1, rows // 2, stride=2), :]
            low = lax.broadcasted_iota(jnp.int32, even.shape, 1) < half
            heads = (jnp.where(low, even, pltpu.roll(odd, half, axis=1)),
                     jnp.where(low, pltpu.roll(even, half, axis=1), odd))
            for h2 in range(2):
                for bl in range(nb):
                    ref[n, hp0 + i, h2, pl.ds(_slab_start(bl, nt2), nt2, stride=GROUP), :] = (
                        heads[h2][bl * nt2:(bl + 1) * nt2])
        else:
            low = lax.broadcasted_iota(jnp.int32, (GROUP, V7X_LANES), 1) < half
            for g in range(nb // GROUP):
                for t2 in range(nt2):
                    src = g * GROUP * nt + 2 * t2
                    even = scr_ref[pl.ds(src, GROUP, stride=nt), :]
                    odd = scr_ref[pl.ds(src + 1, GROUP, stride=nt), :]
                    row0 = g * nt2 * GROUP + t2 * GROUP
                    ref[n, hp0 + i, 0, row0:row0 + GROUP, :] = jnp.where(low, even, pltpu.roll(odd, half, axis=1))
                    ref[n, hp0 + i, 1, row0:row0 + GROUP, :] = jnp.where(low, pltpu.roll(even, half, axis=1), odd)


def _load_slab(ref, nb, nt):
    per_batch = []
    for bl in range(nb):
        per_batch.append(jnp.concatenate(
            [ref[hp, pl.ds(_slab_start(bl, nt), nt, stride=GROUP), :] for hp in range(HEAD_PAIRS)],
            axis=-1))
    return jnp.concatenate(per_batch, axis=0)


def _const_spec(shape, ngrid):
    zeros = (0,) * len(shape)
    if ngrid == 1:
        return pl.BlockSpec(shape, lambda i: zeros, pipeline_mode=pl.Buffered(1))
    return pl.BlockSpec(shape, lambda i, j: zeros, pipeline_mode=pl.Buffered(1))


def _w_spec(offset, width, ngrid):
    shape = (pl.Element(D_MODEL), pl.Element(width))
    if ngrid == 1:
        return pl.BlockSpec(shape, lambda i: (0, offset), pipeline_mode=pl.Buffered(1))
    return pl.BlockSpec(shape, lambda i, j: (0, offset), pipeline_mode=pl.Buffered(1))


def _tile_dims(b, t, long_rows=TOKEN_TILE):
    if t >= long_rows // GROUP:
        nb, nt = GROUP, long_rows // GROUP
    else:
        nb, nt = TOKEN_TILE // t, t
    assert b % nb == 0 and t % nt == 0 and (nb == GROUP or nt == t)
    return nb, nt, (b // nb, t // nt)


def _params():
    return pltpu.CompilerParams(dimension_semantics=("arbitrary", "arbitrary"), vmem_limit_bytes=VMEM_LIMIT)


def _mem_kv_kernel(mem_ref, g_ref, w_ref, k_ref, v_ref, k4_ref, v4_ref):
    xn = _rmsnorm(mem_ref[...], g_ref[...]).astype(_BF16)
    for col0, flat_ref, head_ref in ((0, k_ref, k4_ref), (D_MODEL, v_ref, v4_ref)):
        val = _dot(xn, w_ref[:, col0:col0 + D_MODEL])
        flat_ref[...] = val
        for hh in range(MEM_HEADS):
            head_ref[:, hh, :] = val[:, hh * MEM_HEAD_DIM:(hh + 1) * MEM_HEAD_DIM]


def _mem_kv(mem, g, w_bf):
    b = mem.shape[0]
    flat = pl.BlockSpec((None, N_MEM, D_MODEL), lambda i: (i, 0, 0))
    heads = pl.BlockSpec((None, N_MEM, MEM_HEADS, MEM_HEAD_DIM), lambda i: (i, 0, 0, 0))
    return pl.pallas_call(
        _mem_kv_kernel,
        grid=(b,),
        in_specs=[
            flat,
            pl.BlockSpec((1, D_MODEL), lambda i: (0, 0)),
            pl.BlockSpec((D_MODEL, 2 * D_MODEL), lambda i: (0, 0)),
        ],
        out_specs=[flat, flat, heads, heads],
        out_shape=[jax.ShapeDtypeStruct((b, N_MEM, D_MODEL), _F32)] * 2 + [
            jax.ShapeDtypeStruct((b, N_MEM, MEM_HEADS, MEM_HEAD_DIM), _F32)] * 2,
        compiler_params=pltpu.CompilerParams(
            dimension_semantics=("arbitrary",), vmem_limit_bytes=VMEM_LIMIT),
        name="mem_kv",
    )(mem, g, w_bf)


def _conv_kernel(x_ref, *rest, has_state):
    if has_state:
        st_ref, g_ref, w_ref, wg_ref, cw_ref, cb_ref, wb_ref, acc_ref, new_ref = rest
    else:
        g_ref, w_ref, wg_ref, cw_ref, cb_ref, wb_ref, acc_ref, new_ref, st_ref = rest

        @pl.when(pl.program_id(1) == 0)
        def _():
            st_ref[...] = jnp.zeros_like(st_ref)

    nb, nt, _ = x_ref.shape
    rows = nb * nt
    xn = _rmsnorm(x_ref[...].reshape(rows, D_MODEL), g_ref[...]).astype(_BF16)
    h = _dot(xn, w_ref[:, 0:D_MODEL])
    cg = _dot(xn, w_ref[:, 2 * D_MODEL:3 * D_MODEL])
    u = cg * h
    s0 = _bcast_rows(st_ref[:, 0:1, :], nt)
    s1 = _bcast_rows(st_ref[:, 1:2, :], nt)
    t = _time_index(nb, nt, D_MODEL)
    prev1 = jnp.where(t == 0, s1, pltpu.roll(u, 1, axis=0))
    prev2 = jnp.where(t == 0, s0, jnp.where(t == 1, s1, pltpu.roll(u, 2, axis=0)))
    conv = cb_ref[...] + prev2 * cw_ref[0:1, :] + prev1 * cw_ref[1:2, :] + u * cw_ref[2:3, :]
    bg = _dot(xn, w_ref[:, D_MODEL:2 * D_MODEL])
    gc = _dot(xn, w_ref[:, 3 * D_MODEL:4 * D_MODEL])
    y = bg * conv * _silu(gc)
    mc = _dot(xn, wg_ref[...])
    acc_ref[...] = (_sigmoid(mc) * _dot(y.astype(_BF16), wb_ref[...])).reshape(nb, nt, D_MODEL)
    new = u.reshape(nb, nt, D_MODEL)[:, nt - (CONV_TAPS - 1):, :]
    new_ref[...] = new
    if not has_state:
        st_ref[...] = new


def _conv(x, st, g, w_bf, cw, cb, wb_bf):
    b, t, _ = x.shape
    nb, nt, grid = _tile_dims(b, t, WIDE_TOKEN_TILE)
    has_state = st is not None
    tok = pl.BlockSpec((nb, nt, D_MODEL), lambda i, j: (i, j, 0))
    st_spec = pl.BlockSpec((nb, CONV_TAPS - 1, D_MODEL), lambda i, j: (i, 0, 0))
    return pl.pallas_call(
        functools.partial(_conv_kernel, has_state=has_state),
        grid=grid,
        in_specs=[tok] + ([st_spec] if has_state else []) + [
            _const_spec((1, D_MODEL), 2),
            _w_spec(_OFF_CONV, 4 * D_MODEL, 2),
            _w_spec(_OFF_MCONV, D_MODEL, 2),
            _const_spec((CONV_TAPS, D_MODEL), 2),
            _const_spec((1, D_MODEL), 2),
            _const_spec((D_MODEL, D_MODEL), 2),
        ],
        out_specs=[tok, st_spec],
        out_shape=[
            jax.ShapeDtypeStruct((b, t, D_MODEL), _F32),
            jax.ShapeDtypeStruct((b, CONV_TAPS - 1, D_MODEL), _F32),
        ],
        scratch_shapes=[] if has_state else [pltpu.VMEM((nb, CONV_TAPS - 1, D_MODEL), _F32)],
        compiler_params=_params(),
        name="conv_branch",
    )(*([x] + ([st] if has_state else []) + [g, w_bf, w_bf, cw, cb, wb_bf]))


def _mem_gate(xn, attn, acc_in, w_ref, wg_ref, wb_ref):
    gm = _dot(xn, w_ref[:, D_MODEL:2 * D_MODEL])
    y = attn * _silu(gm)
    mm = _dot(xn, wg_ref[...])
    return acc_in + _sigmoid(mm) * _dot(y.astype(_BF16), wb_ref[...])


def _mem_prompt_kernel(x_ref, k_ref, v_ref, acc_in_ref, g_ref, w_ref, wg_ref, wb_ref, acc_ref):
    xn = _rmsnorm(x_ref[...], g_ref[...]).astype(_BF16)
    q = _dot(xn, w_ref[:, 0:D_MODEL]) * (MEM_HEAD_DIM ** -0.5)
    heads = []
    for hh in range(MEM_HEADS):
        sl = slice(hh * MEM_HEAD_DIM, (hh + 1) * MEM_HEAD_DIM)
        qh = q[:, sl].astype(_BF16)
        kh = k_ref[:, sl].astype(_BF16)
        vh = v_ref[:, sl].astype(_BF16)
        s = lax.dot_general(qh, kh, (((1,), (1,)), ((), ())), preferred_element_type=_F32)
        e = jnp.exp(s - jnp.max(s, axis=-1, keepdims=True))
        l = jnp.sum(e, axis=-1, keepdims=True)
        heads.append(_dot(e.astype(_BF16), vh) / l)
    attn = jnp.concatenate(heads, axis=-1)
    acc_ref[...] = _mem_gate(xn, attn, acc_in_ref[...], w_ref, wg_ref, wb_ref)


def _mem_sample_kernel(x_ref, k_ref, v_ref, acc_in_ref, g_ref, w_ref, wg_ref, wb_ref, acc_ref):
    tb, nt, _ = x_ref.shape
    rows = tb * nt
    xn = _rmsnorm(x_ref[...].reshape(rows, D_MODEL), g_ref[...]).astype(_BF16)
    q = (_dot(xn, w_ref[:, 0:D_MODEL]) * (MEM_HEAD_DIM ** -0.5)).reshape(tb, nt, D_MODEL)
    nkh = N_MEM * MEM_HEADS
    q4 = jnp.concatenate(
        [q[:, :, hh * MEM_HEAD_DIM:(hh + 1) * MEM_HEAD_DIM] for hh in range(MEM_HEADS)], axis=1).astype(_BF16)
    k_all = k_ref[...].reshape(tb, nkh, MEM_HEAD_DIM).astype(_BF16)
    v_all = v_ref[...].reshape(tb, nkh, MEM_HEAD_DIM).astype(_BF16)
    s = jnp.einsum("bqd,bkd->bqk", q4, k_all, preferred_element_type=_F32)
    q_head = lax.broadcasted_iota(jnp.int32, s.shape, 1) // nt
    k_head = lax.broadcasted_iota(jnp.int32, s.shape, 2) % MEM_HEADS
    s = jnp.where(q_head == k_head, s, MASKED_SCORE)
    e = jnp.exp(s - jnp.max(s, axis=-1, keepdims=True))
    l = jnp.sum(e, axis=-1, keepdims=True)
    o = jnp.einsum("bqk,bkd->bqd", e.astype(_BF16), v_all, preferred_element_type=_F32) / l
    attn = jnp.concatenate([o[:, hh * nt:(hh + 1) * nt, :] for hh in range(MEM_HEADS)], axis=-1)
    attn = attn.reshape(rows, D_MODEL)
    acc = _mem_gate(xn, attn, acc_in_ref[...].reshape(rows, D_MODEL), w_ref, wg_ref, wb_ref)
    acc_ref[...] = acc.reshape(tb, nt, D_MODEL)


def _mem_prompt(x, mk, mv, acc_in, g, w_bf, wb_bf):
    b, t, _ = x.shape
    tm = min(WIDE_TOKEN_TILE, t)
    tok = pl.BlockSpec((None, tm, D_MODEL), lambda i, j: (i, j, 0))
    kv = pl.BlockSpec((None, N_MEM, D_MODEL), lambda i, j: (i, 0, 0))
    return pl.pallas_call(
        _mem_prompt_kernel,
        grid=(b, t // tm),
        in_specs=[tok, kv, kv, tok, _const_spec((1, D_MODEL), 2), _w_spec(_OFF_QMEM, 2 * D_MODEL, 2),
                  _w_spec(_OFF_MMEM, D_MODEL, 2), _const_spec((D_MODEL, D_MODEL), 2)],
        out_specs=tok,
        out_shape=jax.ShapeDtypeStruct((b, t, D_MODEL), _F32),
        compiler_params=_params(),
        name="mem_prompt",
    )(x, mk, mv, acc_in, g, w_bf, w_bf, wb_bf)


def _mem_sample(x, mk, mv, acc_in, g, w_bf, wb_bf):
    b, t, _ = x.shape
    tb = SAMPLE_MEM_BATCH_TILE
    tok = pl.BlockSpec((tb, t, D_MODEL), lambda i: (i, 0, 0))
    kv = pl.BlockSpec((None, tb, N_MEM, MEM_HEADS, MEM_HEAD_DIM), lambda i: (0, i, 0, 0, 0))
    return pl.pallas_call(
        _mem_sample_kernel,
        grid=(b // tb,),
        in_specs=[tok, kv, kv, tok, _const_spec((1, D_MODEL), 1), _w_spec(_OFF_QMEM, 2 * D_MODEL, 1),
                  _w_spec(_OFF_MMEM, D_MODEL, 1), _const_spec((D_MODEL, D_MODEL), 1)],
        out_specs=tok,
        out_shape=jax.ShapeDtypeStruct((b, t, D_MODEL), _F32),
        compiler_params=pltpu.CompilerParams(
            dimension_semantics=("arbitrary",), vmem_limit_bytes=VMEM_LIMIT),
        name="mem_sample",
    )(x, mk, mv, acc_in, g, w_bf, w_bf, wb_bf)


def _prep_tile(x_ref, prev_rows, scr_ref, g_ref, w_ref, wg_ref, mu_ref, w0_ref, wup_ref, a0_ref, aup_ref, kk_ref,
               ka_ref, rk_ref, ones_ref, ops_ref, bonus_ref, sg_ref, sm_ref):
    nb, nt, _ = x_ref.shape
    rows = nb * nt
    xn = _rmsnorm(x_ref[...].reshape(rows, D_MODEL), g_ref[...]).astype(_BF16)
    last = {}

    def shifted(lo, n):
        p = _dot(xn, w_ref[:, lo:lo + n])
        last[lo] = p.reshape(nb, nt, n)[:, nt - 1:, :]
        prev = jnp.where(_time_index(nb, nt, n) == 0, _bcast_rows(prev_rows[:, :, lo:lo + n], nt),
                         pltpu.roll(p, 1, axis=0))
        return p + mu_ref[:, lo:lo + n] * (prev - p)

    lora_in = shifted(3 * D_MODEL, 2 * LORA)
    lora_tanh = jnp.tanh(lora_in).astype(_BF16)
    lora_lin = lora_in.astype(_BF16)
    ones = ones_ref[...]
    seg = lambda x: _dot(x.astype(_BF16), ones)
    for c in range(D_MODEL // SEG_TILE):
        sl = slice(c * SEG_TILE, (c + 1) * SEG_TILE)
        hp0 = c * (SEG_TILE // V7X_LANES)
        r = shifted(c * SEG_TILE, SEG_TILE)
        k = shifted(D_MODEL + c * SEG_TILE, SEG_TILE)
        v = shifted(2 * D_MODEL + c * SEG_TILE, SEG_TILE)
        z = w0_ref[:, sl] + _dot(lora_tanh, wup_ref[:, sl])
        store = lambda n, val: _store_step_pairs(ops_ref, n, val, nb, nt, hp0, scr_ref)
        store(OP_W, jnp.exp(-DECAY_RATE * _sigmoid(z)))
        a = _sigmoid(a0_ref[:, sl] + _dot(lora_lin, aup_ref[:, sl]))
        kk = k * kk_ref[:, sl]
        kk = kk * lax.rsqrt(seg(kk * kk) + KK_EPS)
        kx = k * (1.0 + (a - 1.0) * ka_ref[:, sl])
        store(OP_R, r)
        store(OP_K, kx)
        store(OP_V, v)
        store(OP_KK, kk)
        store(OP_B, kk * a)
        bonus_ref[:, :, sl] = (seg(r * kx * rk_ref[:, sl]) * v).reshape(nb, nt, SEG_TILE)
    for c in range(D_MODEL // SEG_TILE):
        sl = slice(c * SEG_TILE, (c + 1) * SEG_TILE)
        lo = D_SHIFT + c * SEG_TILE
        sg_ref[:, :, sl] = _silu(_dot(xn, w_ref[:, lo:lo + SEG_TILE])).reshape(nb, nt, SEG_TILE)
        sm_ref[:, :, sl] = _sigmoid(_dot(xn, wg_ref[:, sl])).reshape(nb, nt, SEG_TILE)
    return jnp.concatenate([last[lo] for lo in sorted(last)], axis=-1)


def _prep_kernel(x_ref, st_ref, *rest):
    new_ref, scr_ref = rest[-2:]
    new_ref[...] = _prep_tile(x_ref, st_ref[...], scr_ref, *rest[:-2])


def _prep_carry_kernel(x_ref, *rest):
    new_ref, scr_ref, st_ref = rest[-3:]

    @pl.when(pl.program_id(1) == 0)
    def _():
        st_ref[...] = jnp.zeros_like(st_ref)

    new = _prep_tile(x_ref, st_ref[...], scr_ref, *rest[:-3])
    st_ref[...] = new
    new_ref[...] = new


def _ops_spec(steps, index):
    return pl.BlockSpec((N_OPS, HEAD_PAIRS, 2, steps // 2 * GROUP, V7X_LANES),
                        lambda i, j: (0, 0, 0, index(i, j), 0))


def _prep(x, st, weights):
    b, t, _ = x.shape
    nb, nt, grid = _tile_dims(b, t)
    has_state = st is not None
    tok = pl.BlockSpec((nb, nt, D_MODEL), lambda i, j: (i, j, 0))
    st_spec = pl.BlockSpec((nb, 1, D_SHIFT), lambda i, j: (i, 0, 0))
    slab = _ops_spec(nb // GROUP * nt, lambda i, j: i * grid[1] + j)
    vec = _const_spec((1, D_MODEL), 2)
    w_specs = [
        vec,
        _w_spec(_OFF_SHIFT, D_SHIFT + D_MODEL, 2),
        _w_spec(_OFF_MRWKV, D_MODEL, 2),
        _const_spec((1, D_SHIFT), 2),
        vec,
        _const_spec((2 * LORA, D_MODEL), 2),
        vec,
        _const_spec((2 * LORA, D_MODEL), 2),
        vec, vec, vec,
        _const_spec((SEG_TILE, SEG_TILE), 2),
    ]
    slab_shape = jax.ShapeDtypeStruct((N_OPS, HEAD_PAIRS, 2, b * t // 2, V7X_LANES), _F32)
    tok_shape = jax.ShapeDtypeStruct((b, t, D_MODEL), _F32)
    stage = pltpu.VMEM((nb * nt, V7X_LANES), _F32)
    return pl.pallas_call(
        _prep_kernel if has_state else _prep_carry_kernel,
        grid=grid,
        in_specs=[tok] + ([st_spec] if has_state else []) + w_specs,
        out_specs=[slab] + [tok] * 3 + [st_spec],
        out_shape=[slab_shape] + [tok_shape] * 3 + [jax.ShapeDtypeStruct((b, 1, D_SHIFT), _F32)],
        scratch_shapes=[stage] + ([] if has_state else [pltpu.VMEM((nb, 1, D_SHIFT), _F32)]),
        compiler_params=_params(),
        name="rwkv_prep",
    )(*([x] + ([st] if has_state else []) + list(weights)))


def _scan_kernel(ops_ref, *rest, zero_init):
    if zero_init:
        o_ref, sT_ref, s_ref, yt_ref, dec_ref, sc_ref = rest
    else:
        s0_ref, o_ref, sT_ref, s_ref, yt_ref, dec_ref, sc_ref = rest
    nk = RWKV_HEAD
    tt = o_ref.shape[0]
    kb = SCAN_UPDATE_BLOCK

    @pl.when(pl.program_id(1) == 0)
    def _():
        if zero_init:
            s_ref[...] = jnp.zeros_like(s_ref)
        else:
            s_ref[...] = s0_ref[...]

    def to_lanes(n, pair):
        src = jnp.minimum(pair, tt // 2 - 1)
        rows = pl.ds(pl.multiple_of(src * GROUP, GROUP), GROUP)
        d = jnp.concatenate(
            [ops_ref[n, hp, h2, rows, :] for h2 in range(2) for hp in range(HEAD_PAIRS)], axis=0)
        tr = d.T
        yt_ref[n, 2 * pair] = tr[0:nk]
        yt_ref[n, 2 * pair + 1] = tr[nk:2 * nk]

    for n in range(N_OPS):
        to_lanes(n, 0)

    dec_ref[...] = jnp.ones_like(dec_ref)

    def scaled_row(n, k):
        return sc_ref[n, pl.ds(k, 1), :]

    def step(pair, tau):
        t = 2 * pair + tau
        p_prev = dec_ref[...]
        p_now = p_prev * yt_ref[OP_W, t]
        inv = 1.0 / p_now
        sc_ref[SC_KK] = yt_ref[OP_KK, t] * p_prev
        sc_ref[SC_B] = yt_ref[OP_B, t] * inv
        sc_ref[SC_K] = yt_ref[OP_K, t] * inv
        sc_ref[SC_R] = yt_ref[OP_R, t] * p_now
        dec_ref[...] = p_now

        def p1(k, acc):
            return acc + s_ref[k] * scaled_row(SC_KK, k)

        sa = lax.fori_loop(0, nk, p1, jnp.zeros((nk, V7X_LANES), _F32), unroll=SCAN_UNROLL)
        vv = yt_ref[OP_V, t]

        def p2(c, o):
            for kk in range(kb):
                k = c * kb + kk
                s = s_ref[k] - sa * scaled_row(SC_B, k) + vv * scaled_row(SC_K, k)
                s_ref[k] = s
                o = o + s * scaled_row(SC_R, k)
            for n in ((2 * c, 2 * c + 1) if tau == 0 else (4 + c,)):
                to_lanes(n, pair + 1)
            return o

        o_ref[t] = lax.fori_loop(0, nk // kb, p2, jnp.zeros((nk, V7X_LANES), _F32))

    def step_pair(pair, carry):
        step(pair, 0)
        step(pair, 1)
        return carry

    lax.fori_loop(0, tt // 2, step_pair, 0)

    def restore(k, carry):
        s_ref[k] = s_ref[k] * dec_ref[pl.ds(k, 1), :]
        return carry

    lax.fori_loop(0, nk, restore, 0, unroll=8)

    @pl.when(pl.program_id(1) == pl.num_programs(1) - 1)
    def _():
        sT_ref[...] = s_ref[...]


def _scan(ops, s0, t):
    groups = ops.shape[3] // (t // 2 * GROUP)
    tt = min(PROMPT_TIME_TILE, t)
    nt = t // tt
    assert tt % 2 == 0 and RWKV_HEAD // SCAN_UPDATE_BLOCK == 2 and N_OPS == 6
    st_spec = pl.BlockSpec((RWKV_HEAD, RWKV_HEAD, V7X_LANES), lambda g, j: (0, 0, g))
    zero_init = s0 is None
    return pl.pallas_call(
        functools.partial(_scan_kernel, zero_init=zero_init),
        grid=(groups, nt),
        in_specs=[_ops_spec(tt, lambda g, j: g * nt + j)] + ([] if zero_init else [st_spec]),
        out_specs=[pl.BlockSpec((tt, RWKV_HEAD, V7X_LANES), lambda g, j: (j, 0, g)), st_spec],
        out_shape=[
            jax.ShapeDtypeStruct((t, RWKV_HEAD, groups * V7X_LANES), _F32),
            jax.ShapeDtypeStruct((RWKV_HEAD, RWKV_HEAD, groups * V7X_LANES), _F32),
        ],
        scratch_shapes=[
            pltpu.VMEM((RWKV_HEAD, RWKV_HEAD, V7X_LANES), _F32),
            pltpu.VMEM((N_OPS, tt + 2, RWKV_HEAD, V7X_LANES), _F32),
            pltpu.VMEM((RWKV_HEAD, V7X_LANES), _F32),
            pltpu.VMEM((N_SCALED, RWKV_HEAD, V7X_LANES), _F32),
        ],
        compiler_params=_params(),
        name="wkv_scan",
    )(*([ops] + ([] if zero_init else [s0])))


def _state_to_lanes(s):
    b = s.shape[0]
    s = s.reshape(b // GROUP, GROUP, HEAD_PAIRS, 2, RWKV_HEAD, RWKV_HEAD)
    return s.transpose(5, 4, 0, 3, 2, 1).reshape(RWKV_HEAD, RWKV_HEAD, b * RWKV_HEADS)


def _state_from_lanes(s, b):
    s = s.reshape(RWKV_HEAD, RWKV_HEAD, b // GROUP, 2, HEAD_PAIRS, GROUP)
    return s.transpose(2, 5, 4, 3, 1, 0).reshape(b, RWKV_HEADS, RWKV_HEAD, RWKV_HEAD)


def _lanes_to_slab(o_ref, slab_ref, nb, nt):
    nk = RWKV_HEAD
    half = V7X_LANES // 2
    low = lax.broadcasted_iota(jnp.int32, (nk, V7X_LANES), 1) < half
    for g in range(nb // GROUP):
        lanes = slice(g * V7X_LANES, (g + 1) * V7X_LANES)
        for i in range(nt // 2):
            y0, y1 = o_ref[2 * i, :, lanes], o_ref[2 * i + 1, :, lanes]
            top = jnp.where(low, y0, pltpu.roll(y1, half, axis=1))
            bot = jnp.where(low, pltpu.roll(y0, half, axis=1), y1)
            d = jnp.concatenate([top, bot], axis=0).T
            for tau in (0, 1):
                for hp in range(HEAD_PAIRS):
                    blk = (tau * HEAD_PAIRS + hp) * GROUP
                    row0 = g * nt * GROUP + (2 * i + tau) * GROUP
                    slab_ref[hp, row0:row0 + GROUP, :] = d[blk:blk + GROUP]


def _out_kernel(o_ref, bonus_ref, sg_ref, sm_ref, acc_ref, x_ref, lng_ref, lnb_ref, wb_ref, wo_ref, nf_ref,
                ones_ref, y_ref, slab_ref):
    nb, nt, _ = x_ref.shape
    rows = nb * nt
    tok = lambda ref: ref[...].reshape(rows, D_MODEL)
    ones = ones_ref[...]
    _lanes_to_slab(o_ref, slab_ref, nb, nt)
    o = _load_slab(slab_ref, nb, nt)
    inv_n = 1.0 / RWKV_HEAD
    d = o - _seg_sum(o, ones) * inv_n
    var = _seg_sum(d * d, ones) * inv_n
    on = d * lax.rsqrt(var + GN_EPS) * lng_ref[...] + lnb_ref[...]
    y = (on + tok(bonus_ref)) * tok(sg_ref)
    merged = tok(acc_ref) + tok(sm_ref) * _dot(y.astype(_BF16), wb_ref[...])
    h = tok(x_ref) + _dot(merged.astype(_BF16), wo_ref[...])
    y_ref[...] = _rmsnorm(h, nf_ref[...]).reshape(nb, nt, D_MODEL)


def _out(o, bonus, sg, sm, acc, x, lng, lnb, wb_bf, wo_bf, nf, ones_bd):
    b, t, _ = x.shape
    nb, nt, grid = _tile_dims(b, t, WIDE_TOKEN_TILE)
    tok = pl.BlockSpec((nb, nt, D_MODEL), lambda i, j: (i, j, 0))
    vec = _const_spec((1, D_MODEL), 2)
    mat = _const_spec((D_MODEL, D_MODEL), 2)
    return pl.pallas_call(
        _out_kernel,
        grid=grid,
        in_specs=[pl.BlockSpec((nt, RWKV_HEAD, nb // GROUP * V7X_LANES), lambda i, j: (j, 0, i))] + [tok] * 5 + [
            vec, vec, mat, mat, vec, _const_spec((SEG_TILE, SEG_TILE), 2)],
        out_specs=tok,
        out_shape=jax.ShapeDtypeStruct((b, t, D_MODEL), _F32),
        scratch_shapes=[pltpu.VMEM((HEAD_PAIRS, nb * nt, V7X_LANES), _F32)],
        compiler_params=_params(),
        name="merge_out",
    )(o, bonus, sg, sm, acc, x, lng, lnb, wb_bf, wo_bf, nf, ones_bd)


def _group(x, conv_st, shift_st, wkv_st, mk, mv, mem_fn, g_in, w_all, conv_w, conv_b, wb, prep_w, out_w):
    b, t, _ = x.shape
    acc, conv_new = _conv(x, conv_st, g_in, w_all, conv_w, conv_b, wb[0])
    acc = mem_fn(x, mk, mv, acc, g_in, w_all, wb[2])
    ops, bonus, sg, sm, shift_new = _prep(x, shift_st, prep_w)
    s0 = None if wkv_st is None else _state_to_lanes(wkv_st)
    o, s_fin = _scan(ops, s0, t)
    y = _out(o, bonus, sg, sm, acc, x, *out_w)
    return y, conv_new[None], shift_new.reshape(b, D_SHIFT)[None], _state_from_lanes(s_fin, b)[None]


def kernel(x_prompt, x_sample, mem_prompt, cache_mem_k, cache_mem_v, state_conv, state_shift, state_wkv, norm_in, w_in, conv_w, conv_b, shift_mu, decay_w0, decay_up, icl_a0, icl_up, k_k, k_a, r_k, ln_x_g, ln_x_b, norm_mem, w_mem_kv, w_branch, w_out, norm_final):
    assert norm_in.shape[0] == 1, "single-layer step"
    bp = x_prompt.shape[0]
    bs = x_sample.shape[0]

    row = lambda a: a.reshape(1, -1)
    w_all = w_in[0].astype(_BF16)
    wb = w_branch[0].astype(_BF16)
    wo = w_out[0].astype(_BF16)
    zpad = jnp.zeros((LORA, D_MODEL), _F32)
    wup = jnp.concatenate([decay_up[0], zpad], axis=0).astype(_BF16)
    aup = jnp.concatenate([zpad, icl_up[0]], axis=0).astype(_BF16)
    ones_bd = jnp.kron(jnp.eye(SEG_TILE // RWKV_HEAD, dtype=_F32),
                       jnp.ones((RWKV_HEAD, RWKV_HEAD), _F32)).astype(_BF16)
    g_in = row(norm_in[0])
    prep_w = (g_in, w_all, w_all, row(shift_mu[0]), row(decay_w0[0]), wup, row(icl_a0[0]), aup, row(k_k[0]),
              row(k_a[0]), row(r_k[0]), ones_bd)
    out_w = (row(ln_x_g[0]), row(ln_x_b[0]), wb[1], wo, row(norm_final), ones_bd)
    shared = (g_in, w_all, conv_w[0], row(conv_b[0]), wb, prep_w, out_w)

    mk, mv, mk_heads, mv_heads = _mem_kv(mem_prompt, row(norm_mem[0]), w_mem_kv[0].astype(_BF16))
    y_p, conv_p, shift_p, wkv_p = _group(x_prompt, None, None, None, mk, mv, _mem_prompt, *shared)
    y_s, conv_s, shift_s, wkv_s = _group(
        x_sample, state_conv[0], state_shift[0].reshape(bs, 1, D_SHIFT), state_wkv[0],
        cache_mem_k, cache_mem_v, _mem_sample, *shared)

    return (y_p, y_s, mk_heads[None], mv_heads[None], conv_p, shift_p, wkv_p, conv_s, shift_s, wkv_s)
```

```python
import functools
import math

import jax
import jax.numpy as jnp
from jax import lax
from jax.experimental import pallas as pl
from jax.experimental.pallas import tpu as pltpu

D_MODEL = 1024
N_MEM = 256
RWKV_HEAD = 64
RWKV_HEADS = D_MODEL // RWKV_HEAD
LORA = 64
MEM_HEADS = 4
MEM_HEAD_DIM = D_MODEL // MEM_HEADS
D_SHIFT = 3 * D_MODEL + 2 * LORA
CONV_TAPS = 3
NORM_EPS = 1e-6
GN_EPS = RWKV_HEAD * 1e-5
KK_EPS = 1e-12
DECAY_RATE = math.exp(-0.5)
MASKED_SCORE = -1e30

_OFF_CONV = 0
_OFF_SHIFT = 4 * D_MODEL
_OFF_GRWKV = _OFF_SHIFT + D_SHIFT
_OFF_QMEM = _OFF_GRWKV + D_MODEL
_OFF_GMEM = _OFF_QMEM + D_MODEL
_OFF_MCONV = _OFF_GMEM + D_MODEL
_OFF_MRWKV = _OFF_MCONV + D_MODEL
_OFF_MMEM = _OFF_MRWKV + D_MODEL

V7X_SUBLANES = 8
V7X_LANES = 128
GROUP = V7X_SUBLANES
HEAD_PAIRS = D_MODEL // V7X_LANES
SEG_TILE = 256
TOKEN_TILE = 256
WIDE_TOKEN_TILE = 512
SCAN_TIME_TILE = 64
SAMPLE_MEM_BATCH_TILE = 8
SCAN_UNROLL = 64
SCAN_UPDATE_BLOCK = 32
N_OPS = 6
OP_R, OP_W, OP_K, OP_V, OP_KK, OP_B = range(N_OPS)
N_SCALED = 4
SC_KK, SC_B, SC_K, SC_R = range(N_SCALED)
MIB = 1024 * 1024
V7X_VMEM_BYTES = 64 * MIB

_F32 = jnp.float32
_BF16 = jnp.bfloat16


def _dot(a, b):
    return jnp.dot(a, b, preferred_element_type=_F32)


def _rmsnorm(x, g):
    ms = jnp.mean(x * x, axis=-1, keepdims=True)
    return x * lax.rsqrt(ms + NORM_EPS) * g


def _sigmoid(x):
    return 0.5 * jnp.tanh(0.5 * x) + 0.5


def _silu(x):
    return x * _sigmoid(x)


def _seg_sum(x, ones_bd):
    xb = x.astype(_BF16)
    return jnp.concatenate(
        [_dot(xb[:, c * SEG_TILE:(c + 1) * SEG_TILE], ones_bd) for c in range(D_MODEL // SEG_TILE)], axis=-1)


def _bcast_rows(st, nt):
    nb, _, c = st.shape
    return jnp.broadcast_to(st, (nb, nt, c)).reshape(nb * nt, c)


def _time_index(nb, nt, c):
    return lax.broadcasted_iota(jnp.int32, (nb * nt, c), 0) % nt


def _slab_start(bl, nt):
    return (bl // GROUP) * (nt * GROUP) + bl % GROUP


def _store_step_pairs(ref, n, val, nb, nt, hp0, scr_ref):
    rows = nb * nt
    nt2 = nt // 2
    half = V7X_LANES // 2
    for i in range(val.shape[1] // V7X_LANES):
        scr_ref[...] = val[:, i * V7X_LANES:(i + 1) * V7X_LANES]
        if nt2 % V7X_SUBLANES == 0:
            even = scr_ref[pl.ds(0, rows // 2, stride=2), :]
            odd = scr_ref[pl.ds(1, rows // 2, stride=2), :]
            low = lax.broadcasted_iota(jnp.int32, even.shape, 1) < half
            heads = (jnp.where(low, even, pltpu.roll(odd, half, axis=1)),
                     jnp.where(low, pltpu.roll(even, half, axis=1), odd))
            for h2 in range(2):
                for bl in range(nb):
                    ref[n, hp0 + i, h2, pl.ds(_slab_start(bl, nt2), nt2, stride=GROUP), :] = (
                        heads[h2][bl * nt2:(bl + 1) * nt2])
        else:
            low = lax.broadcasted_iota(jnp.int32, (GROUP, V7X_LANES), 1) < half
            for g in range(nb // GROUP):
                for t2 in range(nt2):
                    src = g * GROUP * nt + 2 * t2
                    even = scr_ref[pl.ds(src, GROUP, stride=nt), :]
                    odd = scr_ref[pl.ds(src + 1, GROUP, stride=nt), :]
                    row0 = g * nt2 * GROUP + t2 * GROUP
                    ref[n, hp0 + i, 0, row0:row0 + GROUP, :] = jnp.where(low, even, pltpu.roll(odd, half, axis=1))
                    ref[n, hp0 + i, 1, row0:row0 + GROUP, :] = jnp.where(low, pltpu.roll(even, half, axis=1), odd)


def _load_slab(ref, nb, nt):
    per_batch = []
    for bl in range(nb):
        per_batch.append(jnp.concatenate(
            [ref[hp, pl.ds(_slab_start(bl, nt), nt, stride=GROUP), :] for hp in range(HEAD_PAIRS)],
            axis=-1))
    return jnp.concatenate(per_batch, axis=0)


def _const_spec(shape, ngrid):
    zeros = (0,) * len(shape)
    if ngrid == 1:
        return pl.BlockSpec(shape, lambda i: zeros, pipeline_mode=pl.Buffered(1))
    return pl.BlockSpec(shape, lambda i, j: zeros, pipeline_mode=pl.Buffered(1))


def _w_spec(offset, width, ngrid):
    shape = (pl.Element(D_MODEL), pl.Element(width))
    if ngrid == 1:
        return pl.BlockSpec(shape, lambda i: (0, offset), pipeline_mode=pl.Buffered(1))
    return pl.BlockSpec(shape, lambda i, j: (0, offset), pipeline_mode=pl.Buffered(1))


def _tile_dims(b, t, long_rows=TOKEN_TILE):
    if t >= long_rows // GROUP:
        nb, nt = GROUP, long_rows // GROUP
    else:
        nb, nt = TOKEN_TILE // t, t
    assert b % nb == 0 and t % nt == 0 and (nb == GROUP or nt == t)
    return nb, nt, (b // nb, t // nt)


def _vmem_bytes(shape, dtype):
    itemsize = jnp.dtype(dtype).itemsize
    dims = [1 if d is None else getattr(d, "block_size", d) for d in shape]
    dims[-1] = -(-dims[-1] // V7X_LANES) * V7X_LANES
    if len(dims) > 1:
        tile_rows = V7X_SUBLANES * 4 // itemsize
        rows = dims[-2]
        if rows < tile_rows:
            tile_rows = 1 << (rows - 1).bit_length()
        dims[-2] = -(-rows // tile_rows) * tile_rows
    return math.prod(dims) * itemsize


def _tile_f32_bytes(rows, n_live):
    return n_live * rows * D_MODEL * 4


def _call(kernel, args, *, grid, in_specs, out_specs, out_shape, scratch_shapes=(), live_bytes=0, name):
    outs = list(zip(out_specs, out_shape)) if isinstance(out_specs, (list, tuple)) else [(out_specs, out_shape)]
    blocks = [(s, a.dtype) for s, a in zip(in_specs, args)] + [(s, o.dtype) for s, o in outs]
    need = sum((1 if s.pipeline_mode is not None else 2) * _vmem_bytes(s.block_shape, dt) for s, dt in blocks)
    need += sum(_vmem_bytes(s.shape, s.dtype) for s in scratch_shapes) + live_bytes
    assert need <= V7X_VMEM_BYTES, (name, need)
    return pl.pallas_call(
        kernel, grid=grid, in_specs=in_specs, out_specs=out_specs, out_shape=out_shape,
        scratch_shapes=list(scratch_shapes),
        compiler_params=pltpu.CompilerParams(
            dimension_semantics=("arbitrary",) * len(grid), vmem_limit_bytes=-(-need // MIB) * MIB),
        name=name,
    )(*args)


def _mem_kv_kernel(mem_ref, g_ref, w_ref, k_ref, v_ref, k4_ref, v4_ref):
    xn = _rmsnorm(mem_ref[...], g_ref[...]).astype(_BF16)
    for col0, flat_ref, head_ref in ((0, k_ref, k4_ref), (D_MODEL, v_ref, v4_ref)):
        val = _dot(xn, w_ref[:, col0:col0 + D_MODEL])
        flat_ref[...] = val
        for hh in range(MEM_HEADS):
            head_ref[:, hh, :] = val[:, hh * MEM_HEAD_DIM:(hh + 1) * MEM_HEAD_DIM]


def _mem_kv(mem, g, w_bf):
    b = mem.shape[0]
    flat = pl.BlockSpec((None, N_MEM, D_MODEL), lambda i: (i, 0, 0))
    heads = pl.BlockSpec((None, N_MEM, MEM_HEADS, MEM_HEAD_DIM), lambda i: (i, 0, 0, 0))
    return _call(
        _mem_kv_kernel, (mem, g, w_bf),
        grid=(b,),
        in_specs=[flat, _const_spec((1, D_MODEL), 1), _const_spec((D_MODEL, 2 * D_MODEL), 1)],
        out_specs=[flat, flat, heads, heads],
        out_shape=[jax.ShapeDtypeStruct((b, N_MEM, D_MODEL), _F32)] * 2 + [
            jax.ShapeDtypeStruct((b, N_MEM, MEM_HEADS, MEM_HEAD_DIM), _F32)] * 2,
        live_bytes=_tile_f32_bytes(N_MEM, 4),
        name="mem_kv",
    )


def _conv_kernel(x_ref, *rest, has_state):
    if has_state:
        st_ref, g_ref, w_ref, wg_ref, cw_ref, cb_ref, wb_ref, acc_ref, new_ref = rest
    else:
        g_ref, w_ref, wg_ref, cw_ref, cb_ref, wb_ref, acc_ref, new_ref, st_ref = rest

        @pl.when(pl.program_id(1) == 0)
        def _():
            st_ref[...] = jnp.zeros_like(st_ref)

    nb, nt, _ = x_ref.shape
    rows = nb * nt
    xn = _rmsnorm(x_ref[...].reshape(rows, D_MODEL), g_ref[...]).astype(_BF16)
    h = _dot(xn, w_ref[:, 0:D_MODEL])
    cg = _dot(xn, w_ref[:, 2 * D_MODEL:3 * D_MODEL])
    u = cg * h
    s0 = _bcast_rows(st_ref[:, 0:1, :], nt)
    s1 = _bcast_rows(st_ref[:, 1:2, :], nt)
    t = _time_index(nb, nt, D_MODEL)
    prev1 = jnp.where(t == 0, s1, pltpu.roll(u, 1, axis=0))
    prev2 = jnp.where(t == 0, s0, jnp.where(t == 1, s1, pltpu.roll(u, 2, axis=0)))
    conv = cb_ref[...] + prev2 * cw_ref[0:1, :] + prev1 * cw_ref[1:2, :] + u * cw_ref[2:3, :]
    bg = _dot(xn, w_ref[:, D_MODEL:2 * D_MODEL])
    gc = _dot(xn, w_ref[:, 3 * D_MODEL:4 * D_MODEL])
    y = bg * conv * _silu(gc)
    mc = _dot(xn, wg_ref[...])
    acc_ref[...] = (_sigmoid(mc) * _dot(y.astype(_BF16), wb_ref[...])).reshape(nb, nt, D_MODEL)
    new = u.reshape(nb, nt, D_MODEL)[:, nt - (CONV_TAPS - 1):, :]
    new_ref[...] = new
    if not has_state:
        st_ref[...] = new


def _conv(x, st, g, w_bf, cw, cb, wb_bf):
    b, t, _ = x.shape
    nb, nt, grid = _tile_dims(b, t, WIDE_TOKEN_TILE)
    has_state = st is not None
    tok = pl.BlockSpec((nb, nt, D_MODEL), lambda i, j: (i, j, 0))
    st_spec = pl.BlockSpec((nb, CONV_TAPS - 1, D_MODEL), lambda i, j: (i, 0, 0))
    return _call(
        functools.partial(_conv_kernel, has_state=has_state),
        [x] + ([st] if has_state else []) + [g, w_bf, w_bf, cw, cb, wb_bf],
        grid=grid,
        in_specs=[tok] + ([st_spec] if has_state else []) + [
            _const_spec((1, D_MODEL), 2),
            _w_spec(_OFF_CONV, 4 * D_MODEL, 2),
            _w_spec(_OFF_MCONV, D_MODEL, 2),
            _const_spec((CONV_TAPS, D_MODEL), 2),
            _const_spec((1, D_MODEL), 2),
            _const_spec((D_MODEL, D_MODEL), 2),
        ],
        out_specs=[tok, st_spec],
        out_shape=[
            jax.ShapeDtypeStruct((b, t, D_MODEL), _F32),
            jax.ShapeDtypeStruct((b, CONV_TAPS - 1, D_MODEL), _F32),
        ],
        scratch_shapes=[] if has_state else [pltpu.VMEM((nb, CONV_TAPS - 1, D_MODEL), _F32)],
        live_bytes=_tile_f32_bytes(nb * nt, 8),
        name="conv_branch",
    )


def _mem_gate(xn, attn, acc_in, w_ref, wg_ref, wb_ref):
    gm = _dot(xn, w_ref[:, D_MODEL:2 * D_MODEL])
    y = attn * _silu(gm)
    mm = _dot(xn, wg_ref[...])
    return acc_in + _sigmoid(mm) * _dot(y.astype(_BF16), wb_ref[...])


def _mem_prompt_kernel(x_ref, k_ref, v_ref, acc_in_ref, g_ref, w_ref, wg_ref, wb_ref, acc_ref):
    xn = _rmsnorm(x_ref[...], g_ref[...]).astype(_BF16)
    q = _dot(xn, w_ref[:, 0:D_MODEL]) * (MEM_HEAD_DIM ** -0.5)
    heads = []
    for hh in range(MEM_HEADS):
        sl = slice(hh * MEM_HEAD_DIM, (hh + 1) * MEM_HEAD_DIM)
        qh = q[:, sl].astype(_BF16)
        kh = k_ref[:, sl].astype(_BF16)
        vh = v_ref[:, sl].astype(_BF16)
        s = lax.dot_general(qh, kh, (((1,), (1,)), ((), ())), preferred_element_type=_F32)
        e = jnp.exp(s - jnp.max(s, axis=-1, keepdims=True))
        l = jnp.sum(e, axis=-1, keepdims=True)
        heads.append(_dot(e.astype(_BF16), vh) / l)
    attn = jnp.concatenate(heads, axis=-1)
    acc_ref[...] = _mem_gate(xn, attn, acc_in_ref[...], w_ref, wg_ref, wb_ref)


def _mem_sample_kernel(x_ref, k_ref, v_ref, acc_in_ref, g_ref, w_ref, wg_ref, wb_ref, acc_ref):
    tb, nt, _ = x_ref.shape
    rows = tb * nt
    xn = _rmsnorm(x_ref[...].reshape(rows, D_MODEL), g_ref[...]).astype(_BF16)
    q = (_dot(xn, w_ref[:, 0:D_MODEL]) * (MEM_HEAD_DIM ** -0.5)).reshape(tb, nt, D_MODEL)
    nkh = N_MEM * MEM_HEADS
    q4 = jnp.concatenate(
        [q[:, :, hh * MEM_HEAD_DIM:(hh + 1) * MEM_HEAD_DIM] for hh in range(MEM_HEADS)], axis=1).astype(_BF16)
    k_all = k_ref[...].reshape(tb, nkh, MEM_HEAD_DIM).astype(_BF16)
    v_all = v_ref[...].reshape(tb, nkh, MEM_HEAD_DIM).astype(_BF16)
    s = jnp.einsum("bqd,bkd->bqk", q4, k_all, preferred_element_type=_F32)
    q_head = lax.broadcasted_iota(jnp.int32, s.shape, 1) // nt
    k_head = lax.broadcasted_iota(jnp.int32, s.shape, 2) % MEM_HEADS
    s = jnp.where(q_head == k_head, s, MASKED_SCORE)
    e = jnp.exp(s - jnp.max(s, axis=-1, keepdims=True))
    l = jnp.sum(e, axis=-1, keepdims=True)
    o = jnp.einsum("bqk,bkd->bqd", e.astype(_BF16), v_all, preferred_element_type=_F32) / l
    attn = jnp.concatenate([o[:, hh * nt:(hh + 1) * nt, :] for hh in range(MEM_HEADS)], axis=-1)
    attn = attn.reshape(rows, D_MODEL)
    acc = _mem_gate(xn, attn, acc_in_ref[...].reshape(rows, D_MODEL), w_ref, wg_ref, wb_ref)
    acc_ref[...] = acc.reshape(tb, nt, D_MODEL)


def _mem_prompt(x, mk, mv, acc_in, g, w_bf, wb_bf):
    b, t, _ = x.shape
    tm = min(WIDE_TOKEN_TILE, t)
    tok = pl.BlockSpec((None, tm, D_MODEL), lambda i, j: (i, j, 0))
    kv = pl.BlockSpec((None, N_MEM, D_MODEL), lambda i, j: (i, 0, 0))
    return _call(
        _mem_prompt_kernel, (x, mk, mv, acc_in, g, w_bf, w_bf, wb_bf),
        grid=(b, t // tm),
        in_specs=[tok, kv, kv, tok, _const_spec((1, D_MODEL), 2), _w_spec(_OFF_QMEM, 2 * D_MODEL, 2),
                  _w_spec(_OFF_MMEM, D_MODEL, 2), _const_spec((D_MODEL, D_MODEL), 2)],
        out_specs=tok,
        out_shape=jax.ShapeDtypeStruct((b, t, D_MODEL), _F32),
        live_bytes=_tile_f32_bytes(tm, 8),
        name="mem_prompt",
    )


def _mem_sample(x, mk, mv, acc_in, g, w_bf, wb_bf):
    b, t, _ = x.shape
    tb = SAMPLE_MEM_BATCH_TILE
    tok = pl.BlockSpec((tb, t, D_MODEL), lambda i: (i, 0, 0))
    kv = pl.BlockSpec((None, tb, N_MEM, MEM_HEADS, MEM_HEAD_DIM), lambda i: (0, i, 0, 0, 0))
    kv_bf16 = 2 * _vmem_bytes((tb, N_MEM * MEM_HEADS, MEM_HEAD_DIM), _BF16)
    return _call(
        _mem_sample_kernel, (x, mk, mv, acc_in, g, w_bf, w_bf, wb_bf),
        grid=(b // tb,),
        in_specs=[tok, kv, kv, tok, _const_spec((1, D_MODEL), 1), _w_spec(_OFF_QMEM, 2 * D_MODEL, 1),
                  _w_spec(_OFF_MMEM, D_MODEL, 1), _const_spec((D_MODEL, D_MODEL), 1)],
        out_specs=tok,
        out_shape=jax.ShapeDtypeStruct((b, t, D_MODEL), _F32),
        live_bytes=kv_bf16 + _tile_f32_bytes(tb * t * MEM_HEADS, 4),
        name="mem_sample",
    )


def _prep_tile(x_ref, prev_rows, scr_ref, g_ref, w_ref, wg_ref, mu_ref, w0_ref, wup_ref, a0_ref, aup_ref, kk_ref,
               ka_ref, rk_ref, ones_ref, ops_ref, bonus_ref, sg_ref, sm_ref):
    nb, nt, _ = x_ref.shape
    rows = nb * nt
    xn = _rmsnorm(x_ref[...].reshape(rows, D_MODEL), g_ref[...]).astype(_BF16)
    last = {}

    def shifted(lo, n):
        p = _dot(xn, w_ref[:, lo:lo + n])
        last[lo] = p.reshape(nb, nt, n)[:, nt - 1:, :]
        prev = jnp.where(_time_index(nb, nt, n) == 0, _bcast_rows(prev_rows[:, :, lo:lo + n], nt),
                         pltpu.roll(p, 1, axis=0))
        return p + mu_ref[:, lo:lo + n] * (prev - p)

    lora_in = shifted(3 * D_MODEL, 2 * LORA)
    lora_tanh = jnp.tanh(lora_in).astype(_BF16)
    lora_lin = lora_in.astype(_BF16)
    ones = ones_ref[...]
    seg = lambda x: _dot(x.astype(_BF16), ones)
    for c in range(D_MODEL // SEG_TILE):
        sl = slice(c * SEG_TILE, (c + 1) * SEG_TILE)
        hp0 = c * (SEG_TILE // V7X_LANES)
        r = shifted(c * SEG_TILE, SEG_TILE)
        k = shifted(D_MODEL + c * SEG_TILE, SEG_TILE)
        v = shifted(2 * D_MODEL + c * SEG_TILE, SEG_TILE)
        z = w0_ref[:, sl] + _dot(lora_tanh, wup_ref[:, sl])
        store = lambda n, val: _store_step_pairs(ops_ref, n, val, nb, nt, hp0, scr_ref)
        store(OP_W, jnp.exp(-DECAY_RATE * _sigmoid(z)))
        a = _sigmoid(a0_ref[:, sl] + _dot(lora_lin, aup_ref[:, sl]))
        kk = k * kk_ref[:, sl]
        kk = kk * lax.rsqrt(seg(kk * kk) + KK_EPS)
        kx = k * (1.0 + (a - 1.0) * ka_ref[:, sl])
        store(OP_R, r)
        store(OP_K, kx)
        store(OP_V, v)
        store(OP_KK, kk)
        store(OP_B, kk * a)
        bonus_ref[:, :, sl] = (seg(r * kx * rk_ref[:, sl]) * v).reshape(nb, nt, SEG_TILE)
        lo = D_SHIFT + c * SEG_TILE
        sg_ref[:, :, sl] = _silu(_dot(xn, w_ref[:, lo:lo + SEG_TILE])).reshape(nb, nt, SEG_TILE)
        sm_ref[:, :, sl] = _sigmoid(_dot(xn, wg_ref[:, sl])).reshape(nb, nt, SEG_TILE)
    return jnp.concatenate([last[lo] for lo in sorted(last)], axis=-1)


def _prep_kernel(x_ref, st_ref, *rest):
    new_ref, scr_ref = rest[-2:]
    new_ref[...] = _prep_tile(x_ref, st_ref[...], scr_ref, *rest[:-2])


def _prep_carry_kernel(x_ref, *rest):
    new_ref, scr_ref, st_ref = rest[-3:]

    @pl.when(pl.program_id(1) == 0)
    def _():
        st_ref[...] = jnp.zeros_like(st_ref)

    new = _prep_tile(x_ref, st_ref[...], scr_ref, *rest[:-3])
    st_ref[...] = new
    new_ref[...] = new


def _ops_spec(steps, index):
    return pl.BlockSpec((N_OPS, HEAD_PAIRS, 2, steps // 2 * GROUP, V7X_LANES),
                        lambda i, j: (0, 0, 0, index(i, j), 0))


def _prep(x, st, weights):
    b, t, _ = x.shape
    nb, nt, grid = _tile_dims(b, t)
    has_state = st is not None
    tok = pl.BlockSpec((nb, nt, D_MODEL), lambda i, j: (i, j, 0))
    st_spec = pl.BlockSpec((nb, 1, D_SHIFT), lambda i, j: (i, 0, 0))
    slab = _ops_spec(nb // GROUP * nt, lambda i, j: i * grid[1] + j)
    vec = _const_spec((1, D_MODEL), 2)
    w_specs = [
        vec,
        _w_spec(_OFF_SHIFT, D_SHIFT + D_MODEL, 2),
        _w_spec(_OFF_MRWKV, D_MODEL, 2),
        _const_spec((1, D_SHIFT), 2),
        vec,
        _const_spec((2 * LORA, D_MODEL), 2),
        vec,
        _const_spec((2 * LORA, D_MODEL), 2),
        vec, vec, vec,
        _const_spec((SEG_TILE, SEG_TILE), 2),
    ]
    slab_shape = jax.ShapeDtypeStruct((N_OPS, HEAD_PAIRS, 2, b * t // 2, V7X_LANES), _F32)
    tok_shape = jax.ShapeDtypeStruct((b, t, D_MODEL), _F32)
    stage = pltpu.VMEM((nb * nt, V7X_LANES), _F32)
    return _call(
        _prep_kernel if has_state else _prep_carry_kernel,
        [x] + ([st] if has_state else []) + list(weights),
        grid=grid,
        in_specs=[tok] + ([st_spec] if has_state else []) + w_specs,
        out_specs=[slab] + [tok] * 3 + [st_spec],
        out_shape=[slab_shape] + [tok_shape] * 3 + [jax.ShapeDtypeStruct((b, 1, D_SHIFT), _F32)],
        scratch_shapes=[stage] + ([] if has_state else [pltpu.VMEM((nb, 1, D_SHIFT), _F32)]),
        live_bytes=_tile_f32_bytes(nb * nt, 10),
        name="rwkv_prep",
    )


def _scan_kernel(ops_ref, *rest, zero_init):
    if zero_init:
        o_ref, sT_ref, s_ref, yt_ref, dec_ref, sc_ref = rest
    else:
        s0_ref, o_ref, sT_ref, s_ref, yt_ref, dec_ref, sc_ref = rest
    nk = RWKV_HEAD
    tt = o_ref.shape[0]
    kb = SCAN_UPDATE_BLOCK

    @pl.when(pl.program_id(1) == 0)
    def _():
        if zero_init:
            s_ref[...] = jnp.zeros_like(s_ref)
        else:
            s_ref[...] = s0_ref[...]

    def to_lanes(n, pair):
        src = jnp.minimum(pair, tt // 2 - 1)
        rows = pl.ds(pl.multiple_of(src * GROUP, GROUP), GROUP)
        d = jnp.concatenate(
            [ops_ref[n, hp, h2, rows, :] for h2 in range(2) for hp in range(HEAD_PAIRS)], axis=0)
        tr = d.T
        yt_ref[n, 2 * pair] = tr[0:nk]
        yt_ref[n, 2 * pair + 1] = tr[nk:2 * nk]

    for n in range(N_OPS):
        to_lanes(n, 0)

    dec_ref[...] = jnp.ones_like(dec_ref)

    def scaled_row(n, k):
        return sc_ref[n, pl.ds(k, 1), :]

    def step(pair, tau):
        t = 2 * pair + tau
        p_prev = dec_ref[...]
        p_now = p_prev * yt_ref[OP_W, t]
        inv = 1.0 / p_now
        sc_ref[SC_KK] = yt_ref[OP_KK, t] * p_prev
        sc_ref[SC_B] = yt_ref[OP_B, t] * inv
        sc_ref[SC_K] = yt_ref[OP_K, t] * inv
        sc_ref[SC_R] = yt_ref[OP_R, t] * p_now
        dec_ref[...] = p_now

        def p1(k, acc):
            return acc + s_ref[k] * scaled_row(SC_KK, k)

        sa = lax.fori_loop(0, nk, p1, jnp.zeros((nk, V7X_LANES), _F32), unroll=SCAN_UNROLL)
        vv = yt_ref[OP_V, t]

        def p2(c, o):
            for kk in range(kb):
                k = c * kb + kk
                s = s_ref[k] - sa * scaled_row(SC_B, k) + vv * scaled_row(SC_K, k)
                s_ref[k] = s
                o = o + s * scaled_row(SC_R, k)
            for n in ((2 * c, 2 * c + 1) if tau == 0 else (4 + c,)):
                to_lanes(n, pair + 1)
            return o

        o_ref[t] = lax.fori_loop(0, nk // kb, p2, jnp.zeros((nk, V7X_LANES), _F32))

    def step_pair(pair, carry):
        step(pair, 0)
        step(pair, 1)
        return carry

    lax.fori_loop(0, tt // 2, step_pair, 0)

    def restore(k, carry):
        s_ref[k] = s_ref[k] * dec_ref[pl.ds(k, 1), :]
        return carry

    lax.fori_loop(0, nk, restore, 0, unroll=8)

    @pl.when(pl.program_id(1) == pl.num_programs(1) - 1)
    def _():
        sT_ref[...] = s_ref[...]


def _scan(ops, s0, t):
    groups = ops.shape[3] // (t // 2 * GROUP)
    tt = min(SCAN_TIME_TILE, t)
    nt = t // tt
    assert tt % 2 == 0 and RWKV_HEAD // SCAN_UPDATE_BLOCK == 2 and N_OPS == 6
    st_spec = pl.BlockSpec((RWKV_HEAD, RWKV_HEAD, V7X_LANES), lambda g, j: (0, 0, g))
    zero_init = s0 is None
    return _call(
        functools.partial(_scan_kernel, zero_init=zero_init), [ops] + ([] if zero_init else [s0]),
        grid=(groups, nt),
        in_specs=[_ops_spec(tt, lambda g, j: g * nt + j)] + ([] if zero_init else [st_spec]),
        out_specs=[pl.BlockSpec((tt, RWKV_HEAD, V7X_LANES), lambda g, j: (j, 0, g)), st_spec],
        out_shape=[
            jax.ShapeDtypeStruct((t, RWKV_HEAD, groups * V7X_LANES), _F32),
            jax.ShapeDtypeStruct((RWKV_HEAD, RWKV_HEAD, groups * V7X_LANES), _F32),
        ],
        scratch_shapes=[
            pltpu.VMEM((RWKV_HEAD, RWKV_HEAD, V7X_LANES), _F32),
            pltpu.VMEM((N_OPS, tt + 2, RWKV_HEAD, V7X_LANES), _F32),
            pltpu.VMEM((RWKV_HEAD, V7X_LANES), _F32),
            pltpu.VMEM((N_SCALED, RWKV_HEAD, V7X_LANES), _F32),
        ],
        live_bytes=N_OPS * _vmem_bytes((2 * RWKV_HEAD, V7X_LANES), _F32),
        name="wkv_scan",
    )


def _state_to_lanes(s):
    b = s.shape[0]
    s = s.reshape(b // GROUP, GROUP, HEAD_PAIRS, 2, RWKV_HEAD, RWKV_HEAD)
    return s.transpose(5, 4, 0, 3, 2, 1).reshape(RWKV_HEAD, RWKV_HEAD, b * RWKV_HEADS)


def _state_from_lanes(s, b):
    s = s.reshape(RWKV_HEAD, RWKV_HEAD, b // GROUP, 2, HEAD_PAIRS, GROUP)
    return s.transpose(2, 5, 4, 3, 1, 0).reshape(b, RWKV_HEADS, RWKV_HEAD, RWKV_HEAD)


def _lanes_to_slab(o_ref, slab_ref, nb, nt):
    nk = RWKV_HEAD
    half = V7X_LANES // 2
    low = lax.broadcasted_iota(jnp.int32, (nk, V7X_LANES), 1) < half
    for g in range(nb // GROUP):
        lanes = slice(g * V7X_LANES, (g + 1) * V7X_LANES)
        for i in range(nt // 2):
            y0, y1 = o_ref[2 * i, :, lanes], o_ref[2 * i + 1, :, lanes]
            top = jnp.where(low, y0, pltpu.roll(y1, half, axis=1))
            bot = jnp.where(low, pltpu.roll(y0, half, axis=1), y1)
            d = jnp.concatenate([top, bot], axis=0).T
            for tau in (0, 1):
                for hp in range(HEAD_PAIRS):
                    blk = (tau * HEAD_PAIRS + hp) * GROUP
                    row0 = g * nt * GROUP + (2 * i + tau) * GROUP
                    slab_ref[hp, row0:row0 + GROUP, :] = d[blk:blk + GROUP]


def _out_kernel(o_ref, bonus_ref, sg_ref, sm_ref, acc_ref, x_ref, lng_ref, lnb_ref, wb_ref, wo_ref, nf_ref,
                ones_ref, y_ref, slab_ref):
    nb, nt, _ = x_ref.shape
    rows = nb * nt
    tok = lambda ref: ref[...].reshape(rows, D_MODEL)
    ones = ones_ref[...]
    _lanes_to_slab(o_ref, slab_ref, nb, nt)
    o = _load_slab(slab_ref, nb, nt)
    inv_n = 1.0 / RWKV_HEAD
    d = o - _seg_sum(o, ones) * inv_n
    var = _seg_sum(d * d, ones) * inv_n
    on = d * lax.rsqrt(var + GN_EPS) * lng_ref[...] + lnb_ref[...]
    y = (on + tok(bonus_ref)) * tok(sg_ref)
    merged = tok(acc_ref) + tok(sm_ref) * _dot(y.astype(_BF16), wb_ref[...])
    h = tok(x_ref) + _dot(merged.astype(_BF16), wo_ref[...])
    y_ref[...] = _rmsnorm(h, nf_ref[...]).reshape(nb, nt, D_MODEL)


def _out(o, bonus, sg, sm, acc, x, lng, lnb, wb_bf, wo_bf, nf, ones_bd):
    b, t, _ = x.shape
    nb, nt, grid = _tile_dims(b, t, WIDE_TOKEN_TILE)
    tok = pl.BlockSpec((nb, nt, D_MODEL), lambda i, j: (i, j, 0))
    vec = _const_spec((1, D_MODEL), 2)
    mat = _const_spec((D_MODEL, D_MODEL), 2)
    return _call(
        _out_kernel, (o, bonus, sg, sm, acc, x, lng, lnb, wb_bf, wo_bf, nf, ones_bd),
        grid=grid,
        in_specs=[pl.BlockSpec((nt, RWKV_HEAD, nb // GROUP * V7X_LANES), lambda i, j: (j, 0, i))] + [tok] * 5 + [
            vec, vec, mat, mat, vec, _const_spec((SEG_TILE, SEG_TILE), 2)],
        out_specs=tok,
        out_shape=jax.ShapeDtypeStruct((b, t, D_MODEL), _F32),
        scratch_shapes=[pltpu.VMEM((HEAD_PAIRS, nb * nt, V7X_LANES), _F32)],
        live_bytes=_tile_f32_bytes(nb * nt, 6),
        name="merge_out",
    )


def _group(x, conv_st, shift_st, wkv_st, mk, mv, mem_fn, g_in, w_all, conv_w, conv_b, wb, prep_w, out_w):
    b, t, _ = x.shape
    acc, conv_new = _conv(x, conv_st, g_in, w_all, conv_w, conv_b, wb[0])
    acc = mem_fn(x, mk, mv, acc, g_in, w_all, wb[2])
    ops, bonus, sg, sm, shift_new = _prep(x, shift_st, prep_w)
    s0 = None if wkv_st is None else _state_to_lanes(wkv_st)
    o, s_fin = _scan(ops, s0, t)
    y = _out(o, bonus, sg, sm, acc, x, *out_w)
    return y, conv_new[None], shift_new.reshape(b, D_SHIFT)[None], _state_from_lanes(s_fin, b)[None]


def kernel(x_prompt, x_sample, mem_prompt, cache_mem_k, cache_mem_v, state_conv, state_shift, state_wkv, norm_in, w_in, conv_w, conv_b, shift_mu, decay_w0, decay_up, icl_a0, icl_up, k_k, k_a, r_k, ln_x_g, ln_x_b, norm_mem, w_mem_kv, w_branch, w_out, norm_final):
    assert norm_in.shape[0] == 1, "single-layer step"
    bp = x_prompt.shape[0]
    bs = x_sample.shape[0]

    row = lambda a: a.reshape(1, -1)
    w_all = w_in[0].astype(_BF16)
    wb = w_branch[0].astype(_BF16)
    wo = w_out[0].astype(_BF16)
    zpad = jnp.zeros((LORA, D_MODEL), _F32)
    wup = jnp.concatenate([decay_up[0], zpad], axis=0).astype(_BF16)
    aup = jnp.concatenate([zpad, icl_up[0]], axis=0).astype(_BF16)
    ones_bd = jnp.kron(jnp.eye(SEG_TILE // RWKV_HEAD, dtype=_F32),
                       jnp.ones((RWKV_HEAD, RWKV_HEAD), _F32)).astype(_BF16)
    g_in = row(norm_in[0])
    prep_w = (g_in, w_all, w_all, row(shift_mu[0]), row(decay_w0[0]), wup, row(icl_a0[0]), aup, row(k_k[0]),
              row(k_a[0]), row(r_k[0]), ones_bd)
    out_w = (row(ln_x_g[0]), row(ln_x_b[0]), wb[1], wo, row(norm_final), ones_bd)
    shared = (g_in, w_all, conv_w[0], row(conv_b[0]), wb, prep_w, out_w)

    mk, mv, mk_heads, mv_heads = _mem_kv(mem_prompt, row(norm_mem[0]), w_mem_kv[0].astype(_BF16))
    y_p, conv_p, shift_p, wkv_p = _group(x_prompt, None, None, None, mk, mv, _mem_prompt, *shared)
    y_s, conv_s, shift_s, wkv_s = _group(
        x_sample, state_conv[0], state_shift[0].reshape(bs, 1, D_SHIFT), state_wkv[0],
        cache_mem_k, cache_mem_v, _mem_sample, *shared)

    return (y_p, y_s, mk_heads[None], mv_heads[None], conv_p, shift_p, wkv_p, conv_s, shift_s, wkv_s)
```

```python
import functools
import math

import jax
import jax.numpy as jnp
from jax import lax
from jax.experimental import pallas as pl
from jax.experimental.pallas import tpu as pltpu

D_MODEL = 1024
N_MEM = 256
RWKV_HEAD = 64
RWKV_HEADS = D_MODEL // RWKV_HEAD
LORA = 64
MEM_HEADS = 4
MEM_HEAD_DIM = D_MODEL // MEM_HEADS
D_SHIFT = 3 * D_MODEL + 2 * LORA
CONV_TAPS = 3
NORM_EPS = 1e-6
GN_EPS = RWKV_HEAD * 1e-5
KK_EPS = 1e-12
DECAY_RATE = math.exp(-0.5)
MASKED_SCORE = -1e30

_OFF_CONV = 0
_OFF_SHIFT = 4 * D_MODEL
_OFF_GRWKV = _OFF_SHIFT + D_SHIFT
_OFF_QMEM = _OFF_GRWKV + D_MODEL
_OFF_GMEM = _OFF_QMEM + D_MODEL
_OFF_MCONV = _OFF_GMEM + D_MODEL
_OFF_MRWKV = _OFF_MCONV + D_MODEL
_OFF_MMEM = _OFF_MRWKV + D_MODEL

V7X_SUBLANES = 8
V7X_LANES = 128
GROUP = V7X_SUBLANES
HEAD_PAIRS = D_MODEL // V7X_LANES
SEG_TILE = 256
TOKEN_TILE = 256
WIDE_TOKEN_TILE = 512
SCAN_TIME_TILE = 64
SAMPLE_MEM_BATCH_TILE = 8
SCAN_UNROLL = 64
SCAN_UPDATE_BLOCK = 32
N_OPS = 6
OP_R, OP_W, OP_K, OP_V, OP_KK, OP_B = range(N_OPS)
N_SCALED = 4
SC_KK, SC_B, SC_K, SC_R = range(N_SCALED)
V7X_VMEM_BYTES = 64 * 1024 * 1024
VMEM_LIMIT = V7X_VMEM_BYTES * 7 // 8

_F32 = jnp.float32
_BF16 = jnp.bfloat16


def _dot(a, b):
    return jnp.dot(a, b, preferred_element_type=_F32)


def _rmsnorm(x, g):
    ms = jnp.mean(x * x, axis=-1, keepdims=True)
    return x * lax.rsqrt(ms + NORM_EPS) * g


def _sigmoid(x):
    return 0.5 * jnp.tanh(0.5 * x) + 0.5


def _silu(x):
    return x * _sigmoid(x)


def _seg_sum(x, ones_bd):
    xb = x.astype(_BF16)
    return jnp.concatenate(
        [_dot(xb[:, c * SEG_TILE:(c + 1) * SEG_TILE], ones_bd) for c in range(D_MODEL // SEG_TILE)], axis=-1)


def _bcast_rows(st, nt):
    nb, _, c = st.shape
    return jnp.broadcast_to(st, (nb, nt, c)).reshape(nb * nt, c)


def _time_index(nb, nt, c):
    return lax.broadcasted_iota(jnp.int32, (nb * nt, c), 0) % nt


def _slab_start(bl, nt):
    return (bl // GROUP) * (nt * GROUP) + bl % GROUP


def _store_step_pairs(ref, n, val, nb, nt, hp0, scr_ref):
    rows = nb * nt
    nt2 = nt // 2
    half = V7X_LANES // 2
    for i in range(val.shape[1] // V7X_LANES):
        scr_ref[...] = val[:, i * V7X_LANES:(i + 1) * V7X_LANES]
        if nt2 % V7X_SUBLANES == 0:
            even = scr_ref[pl.ds(0, rows // 2, stride=2), :]
            odd = scr_ref[pl.ds(1, rows // 2, stride=2), :]
            low = lax.broadcasted_iota(jnp.int32, even.shape, 1) < half
            heads = (jnp.where(low, even, pltpu.roll(odd, half, axis=1)),
                     jnp.where(low, pltpu.roll(even, half, axis=1), odd))
            for h2 in range(2):
                for bl in range(nb):
                    ref[n, hp0 + i, h2, pl.ds(_slab_start(bl, nt2), nt2, stride=GROUP), :] = (
                        heads[h2][bl * nt2:(bl + 1) * nt2])
        else:
            low = lax.broadcasted_iota(jnp.int32, (GROUP, V7X_LANES), 1) < half
            for g in range(nb // GROUP):
                for t2 in range(nt2):
                    src = g * GROUP * nt + 2 * t2
                    even = scr_ref[pl.ds(src, GROUP, stride=nt), :]
                    odd = scr_ref[pl.ds(src + 1, GROUP, stride=nt), :]
                    row0 = g * nt2 * GROUP + t2 * GROUP
                    ref[n, hp0 + i, 0, row0:row0 + GROUP, :] = jnp.where(low, even, pltpu.roll(odd, half, axis=1))
                    ref[n, hp0 + i, 1, row0:row0 + GROUP, :] = jnp.where(low, pltpu.roll(even, half, axis=1), odd)


def _load_slab(ref, nb, nt):
    per_batch = []
    for bl in range(nb):
        per_batch.append(jnp.concatenate(
            [ref[hp, pl.ds(_slab_start(bl, nt), nt, stride=GROUP), :] for hp in range(HEAD_PAIRS)],
            axis=-1))
    return jnp.concatenate(per_batch, axis=0)


def _const_spec(shape, ngrid):
    zeros = (0,) * len(shape)
    if ngrid == 1:
        return pl.BlockSpec(shape, lambda i: zeros, pipeline_mode=pl.Buffered(1))
    return pl.BlockSpec(shape, lambda i, j: zeros, pipeline_mode=pl.Buffered(1))


def _w_spec(offset, width, ngrid):
    shape = (pl.Element(D_MODEL), pl.Element(width))
    if ngrid == 1:
        return pl.BlockSpec(shape, lambda i: (0, offset), pipeline_mode=pl.Buffered(1))
    return pl.BlockSpec(shape, lambda i, j: (0, offset), pipeline_mode=pl.Buffered(1))


def _tile_dims(b, t, long_rows=TOKEN_TILE):
    if t >= long_rows // GROUP:
        nb, nt = GROUP, long_rows // GROUP
    else:
        nb, nt = TOKEN_TILE // t, t
    assert b % nb == 0 and t % nt == 0 and (nb == GROUP or nt == t)
    return nb, nt, (b // nb, t // nt)


def _vmem_bytes(shape, dtype):
    itemsize = jnp.dtype(dtype).itemsize
    dims = [1 if d is None else getattr(d, "block_size", d) for d in shape]
    dims[-1] = -(-dims[-1] // V7X_LANES) * V7X_LANES
    if len(dims) > 1:
        tile_rows = V7X_SUBLANES * 4 // itemsize
        rows = dims[-2]
        if rows < tile_rows:
            tile_rows = 1 << (rows - 1).bit_length()
        dims[-2] = -(-rows // tile_rows) * tile_rows
    return math.prod(dims) * itemsize


def _tile_f32_bytes(rows, n_live):
    return n_live * rows * D_MODEL * 4


def _call(kernel, args, *, grid, in_specs, out_specs, out_shape, scratch_shapes=(), live_bytes=0, name):
    outs = list(zip(out_specs, out_shape)) if isinstance(out_specs, (list, tuple)) else [(out_specs, out_shape)]
    blocks = [(s, a.dtype) for s, a in zip(in_specs, args)] + [(s, o.dtype) for s, o in outs]
    need = sum((1 if s.pipeline_mode is not None else 2) * _vmem_bytes(s.block_shape, dt) for s, dt in blocks)
    need += sum(_vmem_bytes(s.shape, s.dtype) for s in scratch_shapes) + live_bytes
    assert need <= VMEM_LIMIT, (name, need)
    return pl.pallas_call(
        kernel, grid=grid, in_specs=in_specs, out_specs=out_specs, out_shape=out_shape,
        scratch_shapes=list(scratch_shapes),
        compiler_params=pltpu.CompilerParams(
            dimension_semantics=("arbitrary",) * len(grid), vmem_limit_bytes=VMEM_LIMIT),
        name=name,
    )(*args)


def _mem_kv_kernel(mem_ref, g_ref, w_ref, k_ref, v_ref, k4_ref, v4_ref):
    xn = _rmsnorm(mem_ref[...], g_ref[...]).astype(_BF16)
    for col0, flat_ref, head_ref in ((0, k_ref, k4_ref), (D_MODEL, v_ref, v4_ref)):
        val = _dot(xn, w_ref[:, col0:col0 + D_MODEL])
        flat_ref[...] = val
        for hh in range(MEM_HEADS):
            head_ref[:, hh, :] = val[:, hh * MEM_HEAD_DIM:(hh + 1) * MEM_HEAD_DIM]


def _mem_kv(mem, g, w_bf):
    b = mem.shape[0]
    flat = pl.BlockSpec((None, N_MEM, D_MODEL), lambda i: (i, 0, 0))
    heads = pl.BlockSpec((None, N_MEM, MEM_HEADS, MEM_HEAD_DIM), lambda i: (i, 0, 0, 0))
    return _call(
        _mem_kv_kernel, (mem, g, w_bf),
        grid=(b,),
        in_specs=[flat, _const_spec((1, D_MODEL), 1), _const_spec((D_MODEL, 2 * D_MODEL), 1)],
        out_specs=[flat, flat, heads, heads],
        out_shape=[jax.ShapeDtypeStruct((b, N_MEM, D_MODEL), _F32)] * 2 + [
            jax.ShapeDtypeStruct((b, N_MEM, MEM_HEADS, MEM_HEAD_DIM), _F32)] * 2,
        live_bytes=_tile_f32_bytes(N_MEM, 4),
        name="mem_kv",
    )


def _conv_kernel(x_ref, *rest, has_state):
    if has_state:
        st_ref, g_ref, w_ref, wg_ref, cw_ref, cb_ref, wb_ref, acc_ref, new_ref = rest
    else:
        g_ref, w_ref, wg_ref, cw_ref, cb_ref, wb_ref, acc_ref, new_ref, st_ref = rest

        @pl.when(pl.program_id(1) == 0)
        def _():
            st_ref[...] = jnp.zeros_like(st_ref)

    nb, nt, _ = x_ref.shape
    rows = nb * nt
    xn = _rmsnorm(x_ref[...].reshape(rows, D_MODEL), g_ref[...]).astype(_BF16)
    h = _dot(xn, w_ref[:, 0:D_MODEL])
    cg = _dot(xn, w_ref[:, 2 * D_MODEL:3 * D_MODEL])
    u = cg * h
    s0 = _bcast_rows(st_ref[:, 0:1, :], nt)
    s1 = _bcast_rows(st_ref[:, 1:2, :], nt)
    t = _time_index(nb, nt, D_MODEL)
    prev1 = jnp.where(t == 0, s1, pltpu.roll(u, 1, axis=0))
    prev2 = jnp.where(t == 0, s0, jnp.where(t == 1, s1, pltpu.roll(u, 2, axis=0)))
    conv = cb_ref[...] + prev2 * cw_ref[0:1, :] + prev1 * cw_ref[1:2, :] + u * cw_ref[2:3, :]
    bg = _dot(xn, w_ref[:, D_MODEL:2 * D_MODEL])
    gc = _dot(xn, w_ref[:, 3 * D_MODEL:4 * D_MODEL])
    y = bg * conv * _silu(gc)
    mc = _dot(xn, wg_ref[...])
    acc_ref[...] = (_sigmoid(mc) * _dot(y.astype(_BF16), wb_ref[...])).reshape(nb, nt, D_MODEL)
    new = u.reshape(nb, nt, D_MODEL)[:, nt - (CONV_TAPS - 1):, :]
    new_ref[...] = new
    if not has_state:
        st_ref[...] = new


def _conv(x, st, g, w_bf, cw, cb, wb_bf):
    b, t, _ = x.shape
    nb, nt, grid = _tile_dims(b, t, WIDE_TOKEN_TILE)
    has_state = st is not None
    tok = pl.BlockSpec((nb, nt, D_MODEL), lambda i, j: (i, j, 0))
    st_spec = pl.BlockSpec((nb, CONV_TAPS - 1, D_MODEL), lambda i, j: (i, 0, 0))
    return _call(
        functools.partial(_conv_kernel, has_state=has_state),
        [x] + ([st] if has_state else []) + [g, w_bf, w_bf, cw, cb, wb_bf],
        grid=grid,
        in_specs=[tok] + ([st_spec] if has_state else []) + [
            _const_spec((1, D_MODEL), 2),
            _w_spec(_OFF_CONV, 4 * D_MODEL, 2),
            _w_spec(_OFF_MCONV, D_MODEL, 2),
            _const_spec((CONV_TAPS, D_MODEL), 2),
            _const_spec((1, D_MODEL), 2),
            _const_spec((D_MODEL, D_MODEL), 2),
        ],
        out_specs=[tok, st_spec],
        out_shape=[
            jax.ShapeDtypeStruct((b, t, D_MODEL), _F32),
            jax.ShapeDtypeStruct((b, CONV_TAPS - 1, D_MODEL), _F32),
        ],
        scratch_shapes=[] if has_state else [pltpu.VMEM((nb, CONV_TAPS - 1, D_MODEL), _F32)],
        live_bytes=_tile_f32_bytes(nb * nt, 8),
        name="conv_branch",
    )


def _mem_gate(xn, attn, acc_in, w_ref, wg_ref, wb_ref):
    gm = _dot(xn, w_ref[:, D_MODEL:2 * D_MODEL])
    y = attn * _silu(gm)
    mm = _dot(xn, wg_ref[...])
    return acc_in + _sigmoid(mm) * _dot(y.astype(_BF16), wb_ref[...])


def _mem_prompt_kernel(x_ref, k_ref, v_ref, acc_in_ref, g_ref, w_ref, wg_ref, wb_ref, acc_ref):
    xn = _rmsnorm(x_ref[...], g_ref[...]).astype(_BF16)
    q = _dot(xn, w_ref[:, 0:D_MODEL]) * (MEM_HEAD_DIM ** -0.5)
    heads = []
    for hh in range(MEM_HEADS):
        sl = slice(hh * MEM_HEAD_DIM, (hh + 1) * MEM_HEAD_DIM)
        qh = q[:, sl].astype(_BF16)
        kh = k_ref[:, sl].astype(_BF16)
        vh = v_ref[:, sl].astype(_BF16)
        s = lax.dot_general(qh, kh, (((1,), (1,)), ((), ())), preferred_element_type=_F32)
        e = jnp.exp(s - jnp.max(s, axis=-1, keepdims=True))
        l = jnp.sum(e, axis=-1, keepdims=True)
        heads.append(_dot(e.astype(_BF16), vh) / l)
    attn = jnp.concatenate(heads, axis=-1)
    acc_ref[...] = _mem_gate(xn, attn, acc_in_ref[...], w_ref, wg_ref, wb_ref)


def _mem_sample_kernel(x_ref, k_ref, v_ref, acc_in_ref, g_ref, w_ref, wg_ref, wb_ref, acc_ref):
    tb, nt, _ = x_ref.shape
    rows = tb * nt
    xn = _rmsnorm(x_ref[...].reshape(rows, D_MODEL), g_ref[...]).astype(_BF16)
    q = (_dot(xn, w_ref[:, 0:D_MODEL]) * (MEM_HEAD_DIM ** -0.5)).reshape(tb, nt, D_MODEL)
    nkh = N_MEM * MEM_HEADS
    q4 = jnp.concatenate(
        [q[:, :, hh * MEM_HEAD_DIM:(hh + 1) * MEM_HEAD_DIM] for hh in range(MEM_HEADS)], axis=1).astype(_BF16)
    k_all = k_ref[...].reshape(tb, nkh, MEM_HEAD_DIM).astype(_BF16)
    v_all = v_ref[...].reshape(tb, nkh, MEM_HEAD_DIM).astype(_BF16)
    s = jnp.einsum("bqd,bkd->bqk", q4, k_all, preferred_element_type=_F32)
    q_head = lax.broadcasted_iota(jnp.int32, s.shape, 1) // nt
    k_head = lax.broadcasted_iota(jnp.int32, s.shape, 2) % MEM_HEADS
    s = jnp.where(q_head == k_head, s, MASKED_SCORE)
    e = jnp.exp(s - jnp.max(s, axis=-1, keepdims=True))
    l = jnp.sum(e, axis=-1, keepdims=True)
    o = jnp.einsum("bqk,bkd->bqd", e.astype(_BF16), v_all, preferred_element_type=_F32) / l
    attn = jnp.concatenate([o[:, hh * nt:(hh + 1) * nt, :] for hh in range(MEM_HEADS)], axis=-1)
    attn = attn.reshape(rows, D_MODEL)
    acc = _mem_gate(xn, attn, acc_in_ref[...].reshape(rows, D_MODEL), w_ref, wg_ref, wb_ref)
    acc_ref[...] = acc.reshape(tb, nt, D_MODEL)


def _mem_prompt(x, mk, mv, acc_in, g, w_bf, wb_bf):
    b, t, _ = x.shape
    tm = min(WIDE_TOKEN_TILE, t)
    tok = pl.BlockSpec((None, tm, D_MODEL), lambda i, j: (i, j, 0))
    kv = pl.BlockSpec((None, N_MEM, D_MODEL), lambda i, j: (i, 0, 0))
    return _call(
        _mem_prompt_kernel, (x, mk, mv, acc_in, g, w_bf, w_bf, wb_bf),
        grid=(b, t // tm),
        in_specs=[tok, kv, kv, tok, _const_spec((1, D_MODEL), 2), _w_spec(_OFF_QMEM, 2 * D_MODEL, 2),
                  _w_spec(_OFF_MMEM, D_MODEL, 2), _const_spec((D_MODEL, D_MODEL), 2)],
        out_specs=tok,
        out_shape=jax.ShapeDtypeStruct((b, t, D_MODEL), _F32),
        live_bytes=_tile_f32_bytes(tm, 8),
        name="mem_prompt",
    )


def _mem_sample(x, mk, mv, acc_in, g, w_bf, wb_bf):
    b, t, _ = x.shape
    tb = SAMPLE_MEM_BATCH_TILE
    tok = pl.BlockSpec((tb, t, D_MODEL), lambda i: (i, 0, 0))
    kv = pl.BlockSpec((None, tb, N_MEM, MEM_HEADS, MEM_HEAD_DIM), lambda i: (0, i, 0, 0, 0))
    kv_bf16 = 2 * _vmem_bytes((tb, N_MEM * MEM_HEADS, MEM_HEAD_DIM), _BF16)
    return _call(
        _mem_sample_kernel, (x, mk, mv, acc_in, g, w_bf, w_bf, wb_bf),
        grid=(b // tb,),
        in_specs=[tok, kv, kv, tok, _const_spec((1, D_MODEL), 1), _w_spec(_OFF_QMEM, 2 * D_MODEL, 1),
                  _w_spec(_OFF_MMEM, D_MODEL, 1), _const_spec((D_MODEL, D_MODEL), 1)],
        out_specs=tok,
        out_shape=jax.ShapeDtypeStruct((b, t, D_MODEL), _F32),
        live_bytes=kv_bf16 + _tile_f32_bytes(tb * t * MEM_HEADS, 4),
        name="mem_sample",
    )


def _prep_tile(x_ref, prev_rows, scr_ref, g_ref, w_ref, wg_ref, mu_ref, w0_ref, wup_ref, a0_ref, aup_ref, kk_ref,
               ka_ref, rk_ref, ones_ref, ops_ref, bonus_ref, sg_ref, sm_ref):
    nb, nt, _ = x_ref.shape
    rows = nb * nt
    xn = _rmsnorm(x_ref[...].reshape(rows, D_MODEL), g_ref[...]).astype(_BF16)
    last = {}

    def shifted(lo, n):
        p = _dot(xn, w_ref[:, lo:lo + n])
        last[lo] = p.reshape(nb, nt, n)[:, nt - 1:, :]
        prev = jnp.where(_time_index(nb, nt, n) == 0, _bcast_rows(prev_rows[:, :, lo:lo + n], nt),
                         pltpu.roll(p, 1, axis=0))
        return p + mu_ref[:, lo:lo + n] * (prev - p)

    lora_in = shifted(3 * D_MODEL, 2 * LORA)
    lora_tanh = jnp.tanh(lora_in).astype(_BF16)
    lora_lin = lora_in.astype(_BF16)
    ones = ones_ref[...]
    seg = lambda x: _dot(x.astype(_BF16), ones)
    for c in range(D_MODEL // SEG_TILE):
        sl = slice(c * SEG_TILE, (c + 1) * SEG_TILE)
        hp0 = c * (SEG_TILE // V7X_LANES)
        r = shifted(c * SEG_TILE, SEG_TILE)
        k = shifted(D_MODEL + c * SEG_TILE, SEG_TILE)
        v = shifted(2 * D_MODEL + c * SEG_TILE, SEG_TILE)
        z = w0_ref[:, sl] + _dot(lora_tanh, wup_ref[:, sl])
        store = lambda n, val: _store_step_pairs(ops_ref, n, val, nb, nt, hp0, scr_ref)
        store(OP_W, jnp.exp(-DECAY_RATE * _sigmoid(z)))
        a = _sigmoid(a0_ref[:, sl] + _dot(lora_lin, aup_ref[:, sl]))
        kk = k * kk_ref[:, sl]
        kk = kk * lax.rsqrt(seg(kk * kk) + KK_EPS)
        kx = k * (1.0 + (a - 1.0) * ka_ref[:, sl])
        store(OP_R, r)
        store(OP_K, kx)
        store(OP_V, v)
        store(OP_KK, kk)
        store(OP_B, kk * a)
        bonus_ref[:, :, sl] = (seg(r * kx * rk_ref[:, sl]) * v).reshape(nb, nt, SEG_TILE)
        lo = D_SHIFT + c * SEG_TILE
        sg_ref[:, :, sl] = _silu(_dot(xn, w_ref[:, lo:lo + SEG_TILE])).reshape(nb, nt, SEG_TILE)
        sm_ref[:, :, sl] = _sigmoid(_dot(xn, wg_ref[:, sl])).reshape(nb, nt, SEG_TILE)
    return jnp.concatenate([last[lo] for lo in sorted(last)], axis=-1)


def _prep_kernel(x_ref, st_ref, *rest):
    new_ref, scr_ref = rest[-2:]
    new_ref[...] = _prep_tile(x_ref, st_ref[...], scr_ref, *rest[:-2])


def _prep_carry_kernel(x_ref, *rest):
    new_ref, scr_ref, st_ref = rest[-3:]

    @pl.when(pl.program_id(1) == 0)
    def _():
        st_ref[...] = jnp.zeros_like(st_ref)

    new = _prep_tile(x_ref, st_ref[...], scr_ref, *rest[:-3])
    st_ref[...] = new
    new_ref[...] = new


def _ops_spec(steps, index):
    return pl.BlockSpec((N_OPS, HEAD_PAIRS, 2, steps // 2 * GROUP, V7X_LANES),
                        lambda i, j: (0, 0, 0, index(i, j), 0))


def _prep(x, st, weights):
    b, t, _ = x.shape
    nb, nt, grid = _tile_dims(b, t)
    has_state = st is not None
    tok = pl.BlockSpec((nb, nt, D_MODEL), lambda i, j: (i, j, 0))
    st_spec = pl.BlockSpec((nb, 1, D_SHIFT), lambda i, j: (i, 0, 0))
    slab = _ops_spec(nb // GROUP * nt, lambda i, j: i * grid[1] + j)
    vec = _const_spec((1, D_MODEL), 2)
    w_specs = [
        vec,
        _w_spec(_OFF_SHIFT, D_SHIFT + D_MODEL, 2),
        _w_spec(_OFF_MRWKV, D_MODEL, 2),
        _const_spec((1, D_SHIFT), 2),
        vec,
        _const_spec((2 * LORA, D_MODEL), 2),
        vec,
        _const_spec((2 * LORA, D_MODEL), 2),
        vec, vec, vec,
        _const_spec((SEG_TILE, SEG_TILE), 2),
    ]
    slab_shape = jax.ShapeDtypeStruct((N_OPS, HEAD_PAIRS, 2, b * t // 2, V7X_LANES), _F32)
    tok_shape = jax.ShapeDtypeStruct((b, t, D_MODEL), _F32)
    stage = pltpu.VMEM((nb * nt, V7X_LANES), _F32)
    return _call(
        _prep_kernel if has_state else _prep_carry_kernel,
        [x] + ([st] if has_state else []) + list(weights),
        grid=grid,
        in_specs=[tok] + ([st_spec] if has_state else []) + w_specs,
        out_specs=[slab] + [tok] * 3 + [st_spec],
        out_shape=[slab_shape] + [tok_shape] * 3 + [jax.ShapeDtypeStruct((b, 1, D_SHIFT), _F32)],
        scratch_shapes=[stage] + ([] if has_state else [pltpu.VMEM((nb, 1, D_SHIFT), _F32)]),
        live_bytes=_tile_f32_bytes(nb * nt, 10),
        name="rwkv_prep",
    )


def _scan_kernel(ops_ref, *rest, zero_init):
    if zero_init:
        o_ref, sT_ref, s_ref, yt_ref, dec_ref, sc_ref = rest
    else:
        s0_ref, o_ref, sT_ref, s_ref, yt_ref, dec_ref, sc_ref = rest
    nk = RWKV_HEAD
    tt = o_ref.shape[0]
    kb = SCAN_UPDATE_BLOCK

    @pl.when(pl.program_id(1) == 0)
    def _():
        if zero_init:
            s_ref[...] = jnp.zeros_like(s_ref)
        else:
            s_ref[...] = s0_ref[...]

    def to_lanes(n, pair):
        src = jnp.minimum(pair, tt // 2 - 1)
        rows = pl.ds(pl.multiple_of(src * GROUP, GROUP), GROUP)
        d = jnp.concatenate(
            [ops_ref[n, hp, h2, rows, :] for h2 in range(2) for hp in range(HEAD_PAIRS)], axis=0)
        tr = d.T
        yt_ref[n, 2 * pair] = tr[0:nk]
        yt_ref[n, 2 * pair + 1] = tr[nk:2 * nk]

    for n in range(N_OPS):
        to_lanes(n, 0)

    dec_ref[...] = jnp.ones_like(dec_ref)

    def scaled_row(n, k):
        return sc_ref[n, pl.ds(k, 1), :]

    def step(pair, tau):
        t = 2 * pair + tau
        p_prev = dec_ref[...]
        p_now = p_prev * yt_ref[OP_W, t]
        inv = 1.0 / p_now
        sc_ref[SC_KK] = yt_ref[OP_KK, t] * p_prev
        sc_ref[SC_B] = yt_ref[OP_B, t] * inv
        sc_ref[SC_K] = yt_ref[OP_K, t] * inv
        sc_ref[SC_R] = yt_ref[OP_R, t] * p_now
        dec_ref[...] = p_now

        def p1(k, acc):
            return acc + s_ref[k] * scaled_row(SC_KK, k)

        sa = lax.fori_loop(0, nk, p1, jnp.zeros((nk, V7X_LANES), _F32), unroll=SCAN_UNROLL)
        vv = yt_ref[OP_V, t]

        def p2(c, o):
            for kk in range(kb):
                k = c * kb + kk
                s = s_ref[k] - sa * scaled_row(SC_B, k) + vv * scaled_row(SC_K, k)
                s_ref[k] = s
                o = o + s * scaled_row(SC_R, k)
            for n in ((2 * c, 2 * c + 1) if tau == 0 else (4 + c,)):
                to_lanes(n, pair + 1)
            return o

        o_ref[t] = lax.fori_loop(0, nk // kb, p2, jnp.zeros((nk, V7X_LANES), _F32))

    def step_pair(pair, carry):
        step(pair, 0)
        step(pair, 1)
        return carry

    lax.fori_loop(0, tt // 2, step_pair, 0)

    def restore(k, carry):
        s_ref[k] = s_ref[k] * dec_ref[pl.ds(k, 1), :]
        return carry

    lax.fori_loop(0, nk, restore, 0, unroll=8)

    @pl.when(pl.program_id(1) == pl.num_programs(1) - 1)
    def _():
        sT_ref[...] = s_ref[...]


def _scan(ops, s0, t):
    groups = ops.shape[3] // (t // 2 * GROUP)
    tt = min(SCAN_TIME_TILE, t)
    nt = t // tt
    assert tt % 2 == 0 and RWKV_HEAD // SCAN_UPDATE_BLOCK == 2 and N_OPS == 6
    st_spec = pl.BlockSpec((RWKV_HEAD, RWKV_HEAD, V7X_LANES), lambda g, j: (0, 0, g))
    zero_init = s0 is None
    return _call(
        functools.partial(_scan_kernel, zero_init=zero_init), [ops] + ([] if zero_init else [s0]),
        grid=(groups, nt),
        in_specs=[_ops_spec(tt, lambda g, j: g * nt + j)] + ([] if zero_init else [st_spec]),
        out_specs=[pl.BlockSpec((tt, RWKV_HEAD, V7X_LANES), lambda g, j: (j, 0, g)), st_spec],
        out_shape=[
            jax.ShapeDtypeStruct((t, RWKV_HEAD, groups * V7X_LANES), _F32),
            jax.ShapeDtypeStruct((RWKV_HEAD, RWKV_HEAD, groups * V7X_LANES), _F32),
        ],
        scratch_shapes=[
            pltpu.VMEM((RWKV_HEAD, RWKV_HEAD, V7X_LANES), _F32),
            pltpu.VMEM((N_OPS, tt + 2, RWKV_HEAD, V7X_LANES), _F32),
            pltpu.VMEM((RWKV_HEAD, V7X_LANES), _F32),
            pltpu.VMEM((N_SCALED, RWKV_HEAD, V7X_LANES), _F32),
        ],
        live_bytes=N_OPS * _vmem_bytes((2 * RWKV_HEAD, V7X_LANES), _F32),
        name="wkv_scan",
    )


def _state_to_lanes(s):
    b = s.shape[0]
    s = s.reshape(b // GROUP, GROUP, HEAD_PAIRS, 2, RWKV_HEAD, RWKV_HEAD)
    return s.transpose(5, 4, 0, 3, 2, 1).reshape(RWKV_HEAD, RWKV_HEAD, b * RWKV_HEADS)


def _state_from_lanes(s, b):
    s = s.reshape(RWKV_HEAD, RWKV_HEAD, b // GROUP, 2, HEAD_PAIRS, GROUP)
    return s.transpose(2, 5, 4, 3, 1, 0).reshape(b, RWKV_HEADS, RWKV_HEAD, RWKV_HEAD)


def _lanes_to_slab(o_ref, slab_ref, nb, nt):
    nk = RWKV_HEAD
    half = V7X_LANES // 2
    low = lax.broadcasted_iota(jnp.int32, (nk, V7X_LANES), 1) < half
    for g in range(nb // GROUP):
        lanes = slice(g * V7X_LANES, (g + 1) * V7X_LANES)
        for i in range(nt // 2):
            y0, y1 = o_ref[2 * i, :, lanes], o_ref[2 * i + 1, :, lanes]
            top = jnp.where(low, y0, pltpu.roll(y1, half, axis=1))
            bot = jnp.where(low, pltpu.roll(y0, half, axis=1), y1)
            d = jnp.concatenate([top, bot], axis=0).T
            for tau in (0, 1):
                for hp in range(HEAD_PAIRS):
                    blk = (tau * HEAD_PAIRS + hp) * GROUP
                    row0 = g * nt * GROUP + (2 * i + tau) * GROUP
                    slab_ref[hp, row0:row0 + GROUP, :] = d[blk:blk + GROUP]


def _out_kernel(o_ref, bonus_ref, sg_ref, sm_ref, acc_ref, x_ref, lng_ref, lnb_ref, wb_ref, wo_ref, nf_ref,
                ones_ref, y_ref, slab_ref):
    nb, nt, _ = x_ref.shape
    rows = nb * nt
    tok = lambda ref: ref[...].reshape(rows, D_MODEL)
    ones = ones_ref[...]
    _lanes_to_slab(o_ref, slab_ref, nb, nt)
    o = _load_slab(slab_ref, nb, nt)
    inv_n = 1.0 / RWKV_HEAD
    d = o - _seg_sum(o, ones) * inv_n
    var = _seg_sum(d * d, ones) * inv_n
    on = d * lax.rsqrt(var + GN_EPS) * lng_ref[...] + lnb_ref[...]
    y = (on + tok(bonus_ref)) * tok(sg_ref)
    merged = tok(acc_ref) + tok(sm_ref) * _dot(y.astype(_BF16), wb_ref[...])
    h = tok(x_ref) + _dot(merged.astype(_BF16), wo_ref[...])
    y_ref[...] = _rmsnorm(h, nf_ref[...]).reshape(nb, nt, D_MODEL)


def _out(o, bonus, sg, sm, acc, x, lng, lnb, wb_bf, wo_bf, nf, ones_bd):
    b, t, _ = x.shape
    nb, nt, grid = _tile_dims(b, t, WIDE_TOKEN_TILE)
    tok = pl.BlockSpec((nb, nt, D_MODEL), lambda i, j: (i, j, 0))
    vec = _const_spec((1, D_MODEL), 2)
    mat = _const_spec((D_MODEL, D_MODEL), 2)
    return _call(
        _out_kernel, (o, bonus, sg, sm, acc, x, lng, lnb, wb_bf, wo_bf, nf, ones_bd),
        grid=grid,
        in_specs=[pl.BlockSpec((nt, RWKV_HEAD, nb // GROUP * V7X_LANES), lambda i, j: (j, 0, i))] + [tok] * 5 + [
            vec, vec, mat, mat, vec, _const_spec((SEG_TILE, SEG_TILE), 2)],
        out_specs=tok,
        out_shape=jax.ShapeDtypeStruct((b, t, D_MODEL), _F32),
        scratch_shapes=[pltpu.VMEM((HEAD_PAIRS, nb * nt, V7X_LANES), _F32)],
        live_bytes=_tile_f32_bytes(nb * nt, 6),
        name="merge_out",
    )


def _group(x, conv_st, shift_st, wkv_st, mk, mv, mem_fn, g_in, w_all, conv_w, conv_b, wb, prep_w, out_w):
    b, t, _ = x.shape
    acc, conv_new = _conv(x, conv_st, g_in, w_all, conv_w, conv_b, wb[0])
    acc = mem_fn(x, mk, mv, acc, g_in, w_all, wb[2])
    ops, bonus, sg, sm, shift_new = _prep(x, shift_st, prep_w)
    s0 = None if wkv_st is None else _state_to_lanes(wkv_st)
    o, s_fin = _scan(ops, s0, t)
    y = _out(o, bonus, sg, sm, acc, x, *out_w)
    return y, conv_new[None], shift_new.reshape(b, D_SHIFT)[None], _state_from_lanes(s_fin, b)[None]


def kernel(x_prompt, x_sample, mem_prompt, cache_mem_k, cache_mem_v, state_conv, state_shift, state_wkv, norm_in, w_in, conv_w, conv_b, shift_mu, decay_w0, decay_up, icl_a0, icl_up, k_k, k_a, r_k, ln_x_g, ln_x_b, norm_mem, w_mem_kv, w_branch, w_out, norm_final):
    assert norm_in.shape[0] == 1, "single-layer step"
    bp = x_prompt.shape[0]
    bs = x_sample.shape[0]

    row = lambda a: a.reshape(1, -1)
    w_all = w_in[0].astype(_BF16)
    wb = w_branch[0].astype(_BF16)
    wo = w_out[0].astype(_BF16)
    zpad = jnp.zeros((LORA, D_MODEL), _F32)
    wup = jnp.concatenate([decay_up[0], zpad], axis=0).astype(_BF16)
    aup = jnp.concatenate([zpad, icl_up[0]], axis=0).astype(_BF16)
    ones_bd = jnp.kron(jnp.eye(SEG_TILE // RWKV_HEAD, dtype=_F32),
                       jnp.ones((RWKV_HEAD, RWKV_HEAD), _F32)).astype(_BF16)
    g_in = row(norm_in[0])
    prep_w = (g_in, w_all, w_all, row(shift_mu[0]), row(decay_w0[0]), wup, row(icl_a0[0]), aup, row(k_k[0]),
              row(k_a[0]), row(r_k[0]), ones_bd)
    out_w = (row(ln_x_g[0]), row(ln_x_b[0]), wb[1], wo, row(norm_final), ones_bd)
    shared = (g_in, w_all, conv_w[0], row(conv_b[0]), wb, prep_w, out_w)

    mk, mv, mk_heads, mv_heads = _mem_kv(mem_prompt, row(norm_mem[0]), w_mem_kv[0].astype(_BF16))
    y_p, conv_p, shift_p, wkv_p = _group(x_prompt, None, None, None, mk, mv, _mem_prompt, *shared)
    y_s, conv_s, shift_s, wkv_s = _group(
        x_sample, state_conv[0], state_shift[0].reshape(bs, 1, D_SHIFT), state_wkv[0],
        cache_mem_k, cache_mem_v, _mem_sample, *shared)

    return (y_p, y_s, mk_heads[None], mv_heads[None], conv_p, shift_p, wkv_p, conv_s, shift_s, wkv_s)
```

```python
import functools
import math

import jax
import jax.numpy as jnp
from jax import lax
from jax.experimental import pallas as pl
from jax.experimental.pallas import tpu as pltpu

D_MODEL = 1024
N_MEM = 256
RWKV_HEAD = 64
RWKV_HEADS = D_MODEL // RWKV_HEAD
LORA = 64
MEM_HEADS = 4
MEM_HEAD_DIM = D_MODEL // MEM_HEADS
D_SHIFT = 3 * D_MODEL + 2 * LORA
CONV_TAPS = 3
NORM_EPS = 1e-6
GN_EPS = RWKV_HEAD * 1e-5
KK_EPS = 1e-12
DECAY_RATE = math.exp(-0.5)
MASKED_SCORE = -1e30

_OFF_CONV = 0
_OFF_SHIFT = 4 * D_MODEL
_OFF_GRWKV = _OFF_SHIFT + D_SHIFT
_OFF_QMEM = _OFF_GRWKV + D_MODEL
_OFF_GMEM = _OFF_QMEM + D_MODEL
_OFF_MCONV = _OFF_GMEM + D_MODEL
_OFF_MRWKV = _OFF_MCONV + D_MODEL
_OFF_MMEM = _OFF_MRWKV + D_MODEL

V7X_SUBLANES = 8
V7X_LANES = 128
GROUP = V7X_SUBLANES
HEAD_PAIRS = D_MODEL // V7X_LANES
SEG_TILE = 256
TOKEN_TILE = 256
WIDE_TOKEN_TILE = 512
SCAN_TIME_TILE = 64
SAMPLE_MEM_BATCH_TILE = 8
SCAN_UNROLL = 64
SCAN_UPDATE_BLOCK = 32
N_OPS = 6
OP_R, OP_W, OP_K, OP_V, OP_KK, OP_B = range(N_OPS)
N_SCALED = 4
SC_KK, SC_B, SC_K, SC_R = range(N_SCALED)
V7X_VMEM_BYTES = 64 * 1024 * 1024
VMEM_LIMIT = V7X_VMEM_BYTES * 7 // 8

_F32 = jnp.float32
_BF16 = jnp.bfloat16


def _dot(a, b):
    return jnp.dot(a, b, preferred_element_type=_F32)


def _rmsnorm(x, g):
    ms = jnp.mean(x * x, axis=-1, keepdims=True)
    return x * lax.rsqrt(ms + NORM_EPS) * g


def _sigmoid(x):
    return 0.5 * jnp.tanh(0.5 * x) + 0.5


def _silu(x):
    return x * _sigmoid(x)


def _seg_sum(x, ones_bd):
    xb = x.astype(_BF16)
    return jnp.concatenate(
        [_dot(xb[:, c * SEG_TILE:(c + 1) * SEG_TILE], ones_bd) for c in range(D_MODEL // SEG_TILE)], axis=-1)


def _bcast_rows(st, nt):
    nb, _, c = st.shape
    return jnp.broadcast_to(st, (nb, nt, c)).reshape(nb * nt, c)


def _time_index(nb, nt, c):
    return lax.broadcasted_iota(jnp.int32, (nb * nt, c), 0) % nt


def _slab_start(bl, nt):
    return (bl // GROUP) * (nt * GROUP) + bl % GROUP


def _store_step_pairs(ref, n, val, nb, nt, hp0, scr_ref):
    rows = nb * nt
    nt2 = nt // 2
    half = V7X_LANES // 2
    for i in range(val.shape[1] // V7X_LANES):
        scr_ref[...] = val[:, i * V7X_LANES:(i + 1) * V7X_LANES]
        if nt2 % V7X_SUBLANES == 0:
            even = scr_ref[pl.ds(0, rows // 2, stride=2), :]
            odd = scr_ref[pl.ds(1, rows // 2, stride=2), :]
            low = lax.broadcasted_iota(jnp.int32, even.shape, 1) < half
            heads = (jnp.where(low, even, pltpu.roll(odd, half, axis=1)),
                     jnp.where(low, pltpu.roll(even, half, axis=1), odd))
            for h2 in range(2):
                for bl in range(nb):
                    ref[n, hp0 + i, h2, pl.ds(_slab_start(bl, nt2), nt2, stride=GROUP), :] = (
                        heads[h2][bl * nt2:(bl + 1) * nt2])
        else:
            low = lax.broadcasted_iota(jnp.int32, (GROUP, V7X_LANES), 1) < half
            for g in range(nb // GROUP):
                for t2 in range(nt2):
                    src = g * GROUP * nt + 2 * t2
                    even = scr_ref[pl.ds(src, GROUP, stride=nt), :]
                    odd = scr_ref[pl.ds(src + 1, GROUP, stride=nt), :]
                    row0 = g * nt2 * GROUP + t2 * GROUP
                    ref[n, hp0 + i, 0, row0:row0 + GROUP, :] = jnp.where(low, even, pltpu.roll(odd, half, axis=1))
                    ref[n, hp0 + i, 1, row0:row0 + GROUP, :] = jnp.where(low, pltpu.roll(even, half, axis=1), odd)


def _load_slab(ref, nb, nt):
    per_batch = []
    for bl in range(nb):
        per_batch.append(jnp.concatenate(
            [ref[hp, pl.ds(_slab_start(bl, nt), nt, stride=GROUP), :] for hp in range(HEAD_PAIRS)],
            axis=-1))
    return jnp.concatenate(per_batch, axis=0)


def _const_spec(shape, ngrid):
    zeros = (0,) * len(shape)
    if ngrid == 1:
        return pl.BlockSpec(shape, lambda i: zeros, pipeline_mode=pl.Buffered(1))
    return pl.BlockSpec(shape, lambda i, j: zeros, pipeline_mode=pl.Buffered(1))


def _w_spec(offset, width, ngrid):
    shape = (pl.Element(D_MODEL), pl.Element(width))
    if ngrid == 1:
        return pl.BlockSpec(shape, lambda i: (0, offset), pipeline_mode=pl.Buffered(1))
    return pl.BlockSpec(shape, lambda i, j: (0, offset), pipeline_mode=pl.Buffered(1))


def _tile_dims(b, t, long_rows=TOKEN_TILE):
    if t >= long_rows // GROUP:
        nb, nt = GROUP, long_rows // GROUP
    else:
        nb, nt = TOKEN_TILE // t, t
    assert b % nb == 0 and t % nt == 0 and (nb == GROUP or nt == t)
    return nb, nt, (b // nb, t // nt)


def _vmem_bytes(shape, dtype):
    itemsize = jnp.dtype(dtype).itemsize
    dims = [1 if d is None else getattr(d, "block_size", d) for d in shape]
    dims[-1] = -(-dims[-1] // V7X_LANES) * V7X_LANES
    if len(dims) > 1:
        tile_rows = V7X_SUBLANES * 4 // itemsize
        rows = dims[-2]
        if rows < tile_rows:
            tile_rows = 1 << (rows - 1).bit_length()
        dims[-2] = -(-rows // tile_rows) * tile_rows
    return math.prod(dims) * itemsize


def _tile_f32_bytes(rows, n_live):
    return n_live * rows * D_MODEL * 4


def _call(kernel, args, *, grid, in_specs, out_specs, out_shape, scratch_shapes=(), live_bytes=0, name):
    outs = list(zip(out_specs, out_shape)) if isinstance(out_specs, (list, tuple)) else [(out_specs, out_shape)]
    blocks = [(s, a.dtype) for s, a in zip(in_specs, args)] + [(s, o.dtype) for s, o in outs]
    need = sum((1 if s.pipeline_mode is not None else 2) * _vmem_bytes(s.block_shape, dt) for s, dt in blocks)
    need += sum(_vmem_bytes(s.shape, s.dtype) for s in scratch_shapes) + live_bytes
    assert need <= VMEM_LIMIT, (name, need)
    return pl.pallas_call(
        kernel, grid=grid, in_specs=in_specs, out_specs=out_specs, out_shape=out_shape,
        scratch_shapes=list(scratch_shapes),
        compiler_params=pltpu.CompilerParams(
            dimension_semantics=("arbitrary",) * len(grid), vmem_limit_bytes=VMEM_LIMIT),
        name=name,
    )(*args)


def _mem_kv_kernel(mem_ref, g_ref, w_ref, k_ref, v_ref, k4_ref, v4_ref):
    xn = _rmsnorm(mem_ref[...], g_ref[...]).astype(_BF16)
    for col0, flat_ref, head_ref in ((0, k_ref, k4_ref), (D_MODEL, v_ref, v4_ref)):
        val = _dot(xn, w_ref[:, col0:col0 + D_MODEL])
        flat_ref[...] = val
        for hh in range(MEM_HEADS):
            head_ref[:, hh, :] = val[:, hh * MEM_HEAD_DIM:(hh + 1) * MEM_HEAD_DIM]


def _mem_kv(mem, g, w_bf):
    b = mem.shape[0]
    flat = pl.BlockSpec((None, N_MEM, D_MODEL), lambda i: (i, 0, 0))
    heads = pl.BlockSpec((None, N_MEM, MEM_HEADS, MEM_HEAD_DIM), lambda i: (i, 0, 0, 0))
    return _call(
        _mem_kv_kernel, (mem, g, w_bf),
        grid=(b,),
        in_specs=[flat, _const_spec((1, D_MODEL), 1), _const_spec((D_MODEL, 2 * D_MODEL), 1)],
        out_specs=[flat, flat, heads, heads],
        out_shape=[jax.ShapeDtypeStruct((b, N_MEM, D_MODEL), _F32)] * 2 + [
            jax.ShapeDtypeStruct((b, N_MEM, MEM_HEADS, MEM_HEAD_DIM), _F32)] * 2,
        live_bytes=_tile_f32_bytes(N_MEM, 4),
        name="mem_kv",
    )


def _conv_kernel(x_ref, *rest, has_state):
    if has_state:
        st_ref, g_ref, w_ref, wg_ref, cw_ref, cb_ref, wb_ref, acc_ref, new_ref = rest
    else:
        g_ref, w_ref, wg_ref, cw_ref, cb_ref, wb_ref, acc_ref, new_ref, st_ref = rest

        @pl.when(pl.program_id(1) == 0)
        def _():
            st_ref[...] = jnp.zeros_like(st_ref)

    nb, nt, _ = x_ref.shape
    rows = nb * nt
    xn = _rmsnorm(x_ref[...].reshape(rows, D_MODEL), g_ref[...]).astype(_BF16)
    h = _dot(xn, w_ref[:, 0:D_MODEL])
    cg = _dot(xn, w_ref[:, 2 * D_MODEL:3 * D_MODEL])
    u = cg * h
    s0 = _bcast_rows(st_ref[:, 0:1, :], nt)
    s1 = _bcast_rows(st_ref[:, 1:2, :], nt)
    t = _time_index(nb, nt, D_MODEL)
    prev1 = jnp.where(t == 0, s1, pltpu.roll(u, 1, axis=0))
    prev2 = jnp.where(t == 0, s0, jnp.where(t == 1, s1, pltpu.roll(u, 2, axis=0)))
    conv = cb_ref[...] + prev2 * cw_ref[0:1, :] + prev1 * cw_ref[1:2, :] + u * cw_ref[2:3, :]
    bg = _dot(xn, w_ref[:, D_MODEL:2 * D_MODEL])
    gc = _dot(xn, w_ref[:, 3 * D_MODEL:4 * D_MODEL])
    y = bg * conv * _silu(gc)
    mc = _dot(xn, wg_ref[...])
    acc_ref[...] = (_sigmoid(mc) * _dot(y.astype(_BF16), wb_ref[...])).reshape(nb, nt, D_MODEL)
    new = u.reshape(nb, nt, D_MODEL)[:, nt - (CONV_TAPS - 1):, :]
    new_ref[...] = new
    if not has_state:
        st_ref[...] = new


def _conv(x, st, g, w_bf, cw, cb, wb_bf):
    b, t, _ = x.shape
    nb, nt, grid = _tile_dims(b, t, WIDE_TOKEN_TILE)
    has_state = st is not None
    tok = pl.BlockSpec((nb, nt, D_MODEL), lambda i, j: (i, j, 0))
    st_spec = pl.BlockSpec((nb, CONV_TAPS - 1, D_MODEL), lambda i, j: (i, 0, 0))
    return _call(
        functools.partial(_conv_kernel, has_state=has_state),
        [x] + ([st] if has_state else []) + [g, w_bf, w_bf, cw, cb, wb_bf],
        grid=grid,
        in_specs=[tok] + ([st_spec] if has_state else []) + [
            _const_spec((1, D_MODEL), 2),
            _w_spec(_OFF_CONV, 4 * D_MODEL, 2),
            _w_spec(_OFF_MCONV, D_MODEL, 2),
            _const_spec((CONV_TAPS, D_MODEL), 2),
            _const_spec((1, D_MODEL), 2),
            _const_spec((D_MODEL, D_MODEL), 2),
        ],
        out_specs=[tok, st_spec],
        out_shape=[
            jax.ShapeDtypeStruct((b, t, D_MODEL), _F32),
            jax.ShapeDtypeStruct((b, CONV_TAPS - 1, D_MODEL), _F32),
        ],
        scratch_shapes=[] if has_state else [pltpu.VMEM((nb, CONV_TAPS - 1, D_MODEL), _F32)],
        live_bytes=_tile_f32_bytes(nb * nt, 8),
        name="conv_branch",
    )


def _mem_gate(xn, attn, acc_in, w_ref, wg_ref, wb_ref):
    gm = _dot(xn, w_ref[:, D_MODEL:2 * D_MODEL])
    y = attn * _silu(gm)
    mm = _dot(xn, wg_ref[...])
    return acc_in + _sigmoid(mm) * _dot(y.astype(_BF16), wb_ref[...])


def _mem_prompt_kernel(x_ref, k_ref, v_ref, acc_in_ref, g_ref, w_ref, wg_ref, wb_ref, acc_ref):
    xn = _rmsnorm(x_ref[...], g_ref[...]).astype(_BF16)
    q = _dot(xn, w_ref[:, 0:D_MODEL]) * (MEM_HEAD_DIM ** -0.5)
    heads = []
    for hh in range(MEM_HEADS):
        sl = slice(hh * MEM_HEAD_DIM, (hh + 1) * MEM_HEAD_DIM)
        qh = q[:, sl].astype(_BF16)
        kh = k_ref[:, sl].astype(_BF16)
        vh = v_ref[:, sl].astype(_BF16)
        s = lax.dot_general(qh, kh, (((1,), (1,)), ((), ())), preferred_element_type=_F32)
        e = jnp.exp(s - jnp.max(s, axis=-1, keepdims=True))
        l = jnp.sum(e, axis=-1, keepdims=True)
        heads.append(_dot(e.astype(_BF16), vh) / l)
    attn = jnp.concatenate(heads, axis=-1)
    acc_ref[...] = _mem_gate(xn, attn, acc_in_ref[...], w_ref, wg_ref, wb_ref)


def _mem_sample_kernel(x_ref, k_ref, v_ref, acc_in_ref, g_ref, w_ref, wg_ref, wb_ref, acc_ref):
    tb, nt, _ = x_ref.shape
    rows = tb * nt
    xn = _rmsnorm(x_ref[...].reshape(rows, D_MODEL), g_ref[...]).astype(_BF16)
    q = (_dot(xn, w_ref[:, 0:D_MODEL]) * (MEM_HEAD_DIM ** -0.5)).reshape(tb, nt, D_MODEL)
    nkh = N_MEM * MEM_HEADS
    q4 = jnp.concatenate(
        [q[:, :, hh * MEM_HEAD_DIM:(hh + 1) * MEM_HEAD_DIM] for hh in range(MEM_HEADS)], axis=1).astype(_BF16)
    k_all = k_ref[...].reshape(tb, nkh, MEM_HEAD_DIM).astype(_BF16)
    v_all = v_ref[...].reshape(tb, nkh, MEM_HEAD_DIM).astype(_BF16)
    s = jnp.einsum("bqd,bkd->bqk", q4, k_all, preferred_element_type=_F32)
    q_head = lax.broadcasted_iota(jnp.int32, s.shape, 1) // nt
    k_head = lax.broadcasted_iota(jnp.int32, s.shape, 2) % MEM_HEADS
    s = jnp.where(q_head == k_head, s, MASKED_SCORE)
    e = jnp.exp(s - jnp.max(s, axis=-1, keepdims=True))
    l = jnp.sum(e, axis=-1, keepdims=True)
    o = jnp.einsum("bqk,bkd->bqd", e.astype(_BF16), v_all, preferred_element_type=_F32) / l
    attn = jnp.concatenate([o[:, hh * nt:(hh + 1) * nt, :] for hh in range(MEM_HEADS)], axis=-1)
    attn = attn.reshape(rows, D_MODEL)
    acc = _mem_gate(xn, attn, acc_in_ref[...].reshape(rows, D_MODEL), w_ref, wg_ref, wb_ref)
    acc_ref[...] = acc.reshape(tb, nt, D_MODEL)


def _mem_prompt(x, mk, mv, acc_in, g, w_bf, wb_bf):
    b, t, _ = x.shape
    tm = min(WIDE_TOKEN_TILE, t)
    tok = pl.BlockSpec((None, tm, D_MODEL), lambda i, j: (i, j, 0))
    kv = pl.BlockSpec((None, N_MEM, D_MODEL), lambda i, j: (i, 0, 0))
    return _call(
        _mem_prompt_kernel, (x, mk, mv, acc_in, g, w_bf, w_bf, wb_bf),
        grid=(b, t // tm),
        in_specs=[tok, kv, kv, tok, _const_spec((1, D_MODEL), 2), _w_spec(_OFF_QMEM, 2 * D_MODEL, 2),
                  _w_spec(_OFF_MMEM, D_MODEL, 2), _const_spec((D_MODEL, D_MODEL), 2)],
        out_specs=tok,
        out_shape=jax.ShapeDtypeStruct((b, t, D_MODEL), _F32),
        live_bytes=_tile_f32_bytes(tm, 8),
        name="mem_prompt",
    )


def _mem_sample(x, mk, mv, acc_in, g, w_bf, wb_bf):
    b, t, _ = x.shape
    tb = SAMPLE_MEM_BATCH_TILE
    tok = pl.BlockSpec((tb, t, D_MODEL), lambda i: (i, 0, 0))
    kv = pl.BlockSpec((None, tb, N_MEM, MEM_HEADS, MEM_HEAD_DIM), lambda i: (0, i, 0, 0, 0))
    kv_bf16 = 2 * _vmem_bytes((tb, N_MEM * MEM_HEADS, MEM_HEAD_DIM), _BF16)
    return _call(
        _mem_sample_kernel, (x, mk, mv, acc_in, g, w_bf, w_bf, wb_bf),
        grid=(b // tb,),
        in_specs=[tok, kv, kv, tok, _const_spec((1, D_MODEL), 1), _w_spec(_OFF_QMEM, 2 * D_MODEL, 1),
                  _w_spec(_OFF_MMEM, D_MODEL, 1), _const_spec((D_MODEL, D_MODEL), 1)],
        out_specs=tok,
        out_shape=jax.ShapeDtypeStruct((b, t, D_MODEL), _F32),
        live_bytes=kv_bf16 + _tile_f32_bytes(tb * t * MEM_HEADS, 4),
        name="mem_sample",
    )


def _prep_tile(x_ref, prev_rows, scr_ref, g_ref, w_ref, wg_ref, mu_ref, w0_ref, wup_ref, a0_ref, aup_ref, kk_ref,
               ka_ref, rk_ref, ones_ref, ops_ref, bonus_ref, sg_ref, sm_ref):
    nb, nt, _ = x_ref.shape
    rows = nb * nt
    xn = _rmsnorm(x_ref[...].reshape(rows, D_MODEL), g_ref[...]).astype(_BF16)
    last = {}

    def shifted(lo, n):
        p = _dot(xn, w_ref[:, lo:lo + n])
        last[lo] = p.reshape(nb, nt, n)[:, nt - 1:, :]
        prev = jnp.where(_time_index(nb, nt, n) == 0, _bcast_rows(prev_rows[:, :, lo:lo + n], nt),
                         pltpu.roll(p, 1, axis=0))
        return p + mu_ref[:, lo:lo + n] * (prev - p)

    lora_in = shifted(3 * D_MODEL, 2 * LORA)
    lora_tanh = jnp.tanh(lora_in).astype(_BF16)
    lora_lin = lora_in.astype(_BF16)
    ones = ones_ref[...]
    seg = lambda x: _dot(x.astype(_BF16), ones)
    for c in range(D_MODEL // SEG_TILE):
        sl = slice(c * SEG_TILE, (c + 1) * SEG_TILE)
        hp0 = c * (SEG_TILE // V7X_LANES)
        r = shifted(c * SEG_TILE, SEG_TILE)
        k = shifted(D_MODEL + c * SEG_TILE, SEG_TILE)
        v = shifted(2 * D_MODEL + c * SEG_TILE, SEG_TILE)
        z = w0_ref[:, sl] + _dot(lora_tanh, wup_ref[:, sl])
        store = lambda n, val: _store_step_pairs(ops_ref, n, val, nb, nt, hp0, scr_ref)
        store(OP_W, jnp.exp(-DECAY_RATE * _sigmoid(z)))
        a = _sigmoid(a0_ref[:, sl] + _dot(lora_lin, aup_ref[:, sl]))
        kk = k * kk_ref[:, sl]
        kk = kk * lax.rsqrt(seg(kk * kk) + KK_EPS)
        kx = k * (1.0 + (a - 1.0) * ka_ref[:, sl])
        store(OP_R, r)
        store(OP_K, kx)
        store(OP_V, v)
        store(OP_KK, kk)
        store(OP_B, kk * a)
        bonus_ref[:, :, sl] = (seg(r * kx * rk_ref[:, sl]) * v).reshape(nb, nt, SEG_TILE)
        lo = D_SHIFT + c * SEG_TILE
        sg_ref[:, :, sl] = _silu(_dot(xn, w_ref[:, lo:lo + SEG_TILE])).reshape(nb, nt, SEG_TILE)
        sm_ref[:, :, sl] = _sigmoid(_dot(xn, wg_ref[:, sl])).reshape(nb, nt, SEG_TILE)
    return jnp.concatenate([last[lo] for lo in sorted(last)], axis=-1)


def _prep_kernel(x_ref, st_ref, *rest):
    new_ref, scr_ref = rest[-2:]
    new_ref[...] = _prep_tile(x_ref, st_ref[...], scr_ref, *rest[:-2])


def _prep_carry_kernel(x_ref, *rest):
    new_ref, scr_ref, st_ref = rest[-3:]

    @pl.when(pl.program_id(1) == 0)
    def _():
        st_ref[...] = jnp.zeros_like(st_ref)

    new = _prep_tile(x_ref, st_ref[...], scr_ref, *rest[:-3])
    st_ref[...] = new
    new_ref[...] = new


def _ops_spec(steps, index):
    return pl.BlockSpec((N_OPS, HEAD_PAIRS, 2, steps // 2 * GROUP, V7X_LANES),
                        lambda i, j: (0, 0, 0, index(i, j), 0))


def _prep(x, st, weights):
    b, t, _ = x.shape
    nb, nt, grid = _tile_dims(b, t)
    has_state = st is not None
    tok = pl.BlockSpec((nb, nt, D_MODEL), lambda i, j: (i, j, 0))
    st_spec = pl.BlockSpec((nb, 1, D_SHIFT), lambda i, j: (i, 0, 0))
    slab = _ops_spec(nb // GROUP * nt, lambda i, j: i * grid[1] + j)
    vec = _const_spec((1, D_MODEL), 2)
    w_specs = [
        vec,
        _w_spec(_OFF_SHIFT, D_SHIFT + D_MODEL, 2),
        _w_spec(_OFF_MRWKV, D_MODEL, 2),
        _const_spec((1, D_SHIFT), 2),
        vec,
        _const_spec((2 * LORA, D_MODEL), 2),
        vec,
        _const_spec((2 * LORA, D_MODEL), 2),
        vec, vec, vec,
        _const_spec((SEG_TILE, SEG_TILE), 2),
    ]
    slab_shape = jax.ShapeDtypeStruct((N_OPS, HEAD_PAIRS, 2, b * t // 2, V7X_LANES), _F32)
    tok_shape = jax.ShapeDtypeStruct((b, t, D_MODEL), _F32)
    stage = pltpu.VMEM((nb * nt, V7X_LANES), _F32)
    return _call(
        _prep_kernel if has_state else _prep_carry_kernel,
        [x] + ([st] if has_state else []) + list(weights),
        grid=grid,
        in_specs=[tok] + ([st_spec] if has_state else []) + w_specs,
        out_specs=[slab] + [tok] * 3 + [st_spec],
        out_shape=[slab_shape] + [tok_shape] * 3 + [jax.ShapeDtypeStruct((b, 1, D_SHIFT), _F32)],
        scratch_shapes=[stage] + ([] if has_state else [pltpu.VMEM((nb, 1, D_SHIFT), _F32)]),
        live_bytes=_tile_f32_bytes(nb * nt, 10),
        name="rwkv_prep",
    )


def _scan_kernel(ops_ref, *rest, zero_init):
    if zero_init:
        o_ref, sT_ref, s_ref, yt_ref, dec_ref, sc_ref = rest
    else:
        s0_ref, o_ref, sT_ref, s_ref, yt_ref, dec_ref, sc_ref = rest
    nk = RWKV_HEAD
    tt = o_ref.shape[0]
    kb = SCAN_UPDATE_BLOCK

    @pl.when(pl.program_id(1) == 0)
    def _():
        if zero_init:
            s_ref[...] = jnp.zeros_like(s_ref)
        else:
            s_ref[...] = s0_ref[...]

    def to_lanes(n, pair):
        src = jnp.minimum(pair, tt // 2 - 1)
        rows = pl.ds(pl.multiple_of(src * GROUP, GROUP), GROUP)
        d = jnp.concatenate(
            [ops_ref[n, hp, h2, rows, :] for h2 in range(2) for hp in range(HEAD_PAIRS)], axis=0)
        tr = d.T
        yt_ref[n, 2 * pair] = tr[0:nk]
        yt_ref[n, 2 * pair + 1] = tr[nk:2 * nk]

    for n in range(N_OPS):
        to_lanes(n, 0)

    dec_ref[...] = jnp.ones_like(dec_ref)

    def scaled_row(n, k):
        return sc_ref[n, pl.ds(k, 1), :]

    def step(pair, tau):
        t = 2 * pair + tau
        p_prev = dec_ref[...]
        p_now = p_prev * yt_ref[OP_W, t]
        inv = 1.0 / p_now
        sc_ref[SC_KK] = yt_ref[OP_KK, t] * p_prev
        sc_ref[SC_B] = yt_ref[OP_B, t] * inv
        sc_ref[SC_K] = yt_ref[OP_K, t] * inv
        sc_ref[SC_R] = yt_ref[OP_R, t] * p_now
        dec_ref[...] = p_now

        def p1(k, acc):
            return acc + s_ref[k] * scaled_row(SC_KK, k)

        sa = lax.fori_loop(0, nk, p1, jnp.zeros((nk, V7X_LANES), _F32), unroll=SCAN_UNROLL)
        vv = yt_ref[OP_V, t]

        def p2(c, o):
            for kk in range(kb):
                k = c * kb + kk
                s = s_ref[k] - sa * scaled_row(SC_B, k) + vv * scaled_row(SC_K, k)
                s_ref[k] = s
                o = o + s * scaled_row(SC_R, k)
            for n in ((2 * c, 2 * c + 1) if tau == 0 else (4 + c,)):
                to_lanes(n, pair + 1)
            return o

        o_ref[t] = lax.fori_loop(0, nk // kb, p2, jnp.zeros((nk, V7X_LANES), _F32))

    def step_pair(pair, carry):
        step(pair, 0)
        step(pair, 1)
        return carry

    lax.fori_loop(0, tt // 2, step_pair, 0)

    def restore(k, carry):
        s_ref[k] = s_ref[k] * dec_ref[pl.ds(k, 1), :]
        return carry

    lax.fori_loop(0, nk, restore, 0, unroll=8)

    @pl.when(pl.program_id(1) == pl.num_programs(1) - 1)
    def _():
        sT_ref[...] = s_ref[...]


def _scan(ops, s0, t):
    groups = ops.shape[3] // (t // 2 * GROUP)
    tt = min(SCAN_TIME_TILE, t)
    nt = t // tt
    assert tt % 2 == 0 and RWKV_HEAD // SCAN_UPDATE_BLOCK == 2 and N_OPS == 6
    st_spec = pl.BlockSpec((RWKV_HEAD, RWKV_HEAD, V7X_LANES), lambda g, j: (0, 0, g))
    zero_init = s0 is None
    return _call(
        functools.partial(_scan_kernel, zero_init=zero_init), [ops] + ([] if zero_init else [s0]),
        grid=(groups, nt),
        in_specs=[_ops_spec(tt, lambda g, j: g * nt + j)] + ([] if zero_init else [st_spec]),
        out_specs=[pl.BlockSpec((tt, RWKV_HEAD, V7X_LANES), lambda g, j: (j, 0, g)), st_spec],
        out_shape=[
            jax.ShapeDtypeStruct((t, RWKV_HEAD, groups * V7X_LANES), _F32),
            jax.ShapeDtypeStruct((RWKV_HEAD, RWKV_HEAD, groups * V7X_LANES), _F32),
        ],
        scratch_shapes=[
            pltpu.VMEM((RWKV_HEAD, RWKV_HEAD, V7X_LANES), _F32),
            pltpu.VMEM((N_OPS, tt + 2, RWKV_HEAD, V7X_LANES), _F32),
            pltpu.VMEM((RWKV_HEAD, V7X_LANES), _F32),
            pltpu.VMEM((N_SCALED, RWKV_HEAD, V7X_LANES), _F32),
        ],
        live_bytes=N_OPS * _vmem_bytes((2 * RWKV_HEAD, V7X_LANES), _F32),
        name="wkv_scan",
    )


def _transpose_kernel(x_ref, o_ref):
    o_ref[...] = x_ref[...].T


def _pairs_to_lanes(x):
    p, n = x.shape
    return _call(
        _transpose_kernel, (x,),
        grid=(p // V7X_LANES,),
        in_specs=[pl.BlockSpec((V7X_LANES, n), lambda g: (g, 0))],
        out_specs=pl.BlockSpec((n, V7X_LANES), lambda g: (0, g)),
        out_shape=jax.ShapeDtypeStruct((n, p), x.dtype),
        name="state_to_lanes",
    )


def _lanes_to_pairs(x):
    n, p = x.shape
    return _call(
        _transpose_kernel, (x,),
        grid=(p // V7X_LANES,),
        in_specs=[pl.BlockSpec((n, V7X_LANES), lambda g: (0, g))],
        out_specs=pl.BlockSpec((V7X_LANES, n), lambda g: (g, 0)),
        out_shape=jax.ShapeDtypeStruct((p, n), x.dtype),
        name="state_from_lanes",
    )


def _state_to_lanes(s):
    b = s.shape[0]
    s = s.reshape(b // GROUP, GROUP, HEAD_PAIRS, 2, RWKV_HEAD, RWKV_HEAD).transpose(0, 3, 2, 1, 5, 4)
    s = _pairs_to_lanes(s.reshape(b * RWKV_HEADS, RWKV_HEAD * RWKV_HEAD))
    return s.reshape(RWKV_HEAD, RWKV_HEAD, b * RWKV_HEADS)


def _state_from_lanes(s, b):
    s = _lanes_to_pairs(s.reshape(RWKV_HEAD * RWKV_HEAD, b * RWKV_HEADS))
    s = s.reshape(b // GROUP, 2, HEAD_PAIRS, GROUP, RWKV_HEAD, RWKV_HEAD).transpose(0, 3, 2, 1, 5, 4)
    return s.reshape(b, RWKV_HEADS, RWKV_HEAD, RWKV_HEAD)


def _lanes_to_slab(o_ref, slab_ref, nb, nt):
    nk = RWKV_HEAD
    half = V7X_LANES // 2
    low = lax.broadcasted_iota(jnp.int32, (nk, V7X_LANES), 1) < half
    for g in range(nb // GROUP):
        lanes = slice(g * V7X_LANES, (g + 1) * V7X_LANES)
        for i in range(nt // 2):
            y0, y1 = o_ref[2 * i, :, lanes], o_ref[2 * i + 1, :, lanes]
            top = jnp.where(low, y0, pltpu.roll(y1, half, axis=1))
            bot = jnp.where(low, pltpu.roll(y0, half, axis=1), y1)
            d = jnp.concatenate([top, bot], axis=0).T
            for tau in (0, 1):
                for hp in range(HEAD_PAIRS):
                    blk = (tau * HEAD_PAIRS + hp) * GROUP
                    row0 = g * nt * GROUP + (2 * i + tau) * GROUP
                    slab_ref[hp, row0:row0 + GROUP, :] = d[blk:blk + GROUP]


def _out_kernel(o_ref, bonus_ref, sg_ref, sm_ref, acc_ref, x_ref, lng_ref, lnb_ref, wb_ref, wo_ref, nf_ref,
                ones_ref, y_ref, slab_ref):
    nb, nt, _ = x_ref.shape
    rows = nb * nt
    tok = lambda ref: ref[...].reshape(rows, D_MODEL)
    ones = ones_ref[...]
    _lanes_to_slab(o_ref, slab_ref, nb, nt)
    o = _load_slab(slab_ref, nb, nt)
    inv_n = 1.0 / RWKV_HEAD
    d = o - _seg_sum(o, ones) * inv_n
    var = _seg_sum(d * d, ones) * inv_n
    on = d * lax.rsqrt(var + GN_EPS) * lng_ref[...] + lnb_ref[...]
    y = (on + tok(bonus_ref)) * tok(sg_ref)
    merged = tok(acc_ref) + tok(sm_ref) * _dot(y.astype(_BF16), wb_ref[...])
    h = tok(x_ref) + _dot(merged.astype(_BF16), wo_ref[...])
    y_ref[...] = _rmsnorm(h, nf_ref[...]).reshape(nb, nt, D_MODEL)


def _out(o, bonus, sg, sm, acc, x, lng, lnb, wb_bf, wo_bf, nf, ones_bd):
    b, t, _ = x.shape
    nb, nt, grid = _tile_dims(b, t, WIDE_TOKEN_TILE)
    tok = pl.BlockSpec((nb, nt, D_MODEL), lambda i, j: (i, j, 0))
    vec = _const_spec((1, D_MODEL), 2)
    mat = _const_spec((D_MODEL, D_MODEL), 2)
    return _call(
        _out_kernel, (o, bonus, sg, sm, acc, x, lng, lnb, wb_bf, wo_bf, nf, ones_bd),
        grid=grid,
        in_specs=[pl.BlockSpec((nt, RWKV_HEAD, nb // GROUP * V7X_LANES), lambda i, j: (j, 0, i))] + [tok] * 5 + [
            vec, vec, mat, mat, vec, _const_spec((SEG_TILE, SEG_TILE), 2)],
        out_specs=tok,
        out_shape=jax.ShapeDtypeStruct((b, t, D_MODEL), _F32),
        scratch_shapes=[pltpu.VMEM((HEAD_PAIRS, nb * nt, V7X_LANES), _F32)],
        live_bytes=_tile_f32_bytes(nb * nt, 6),
        name="merge_out",
    )


def _group(x, conv_st, shift_st, wkv_st, mk, mv, mem_fn, g_in, w_all, conv_w, conv_b, wb, prep_w, out_w):
    b, t, _ = x.shape
    acc, conv_new = _conv(x, conv_st, g_in, w_all, conv_w, conv_b, wb[0])
    acc = mem_fn(x, mk, mv, acc, g_in, w_all, wb[2])
    ops, bonus, sg, sm, shift_new = _prep(x, shift_st, prep_w)
    s0 = None if wkv_st is None else _state_to_lanes(wkv_st)
    o, s_fin = _scan(ops, s0, t)
    y = _out(o, bonus, sg, sm, acc, x, *out_w)
    return y, conv_new[None], shift_new.reshape(b, D_SHIFT)[None], _state_from_lanes(s_fin, b)[None]


def kernel(x_prompt, x_sample, mem_prompt, cache_mem_k, cache_mem_v, state_conv, state_shift, state_wkv, norm_in, w_in, conv_w, conv_b, shift_mu, decay_w0, decay_up, icl_a0, icl_up, k_k, k_a, r_k, ln_x_g, ln_x_b, norm_mem, w_mem_kv, w_branch, w_out, norm_final):
    assert norm_in.shape[0] == 1, "single-layer step"
    bp = x_prompt.shape[0]
    bs = x_sample.shape[0]

    row = lambda a: a.reshape(1, -1)
    w_all = w_in[0].astype(_BF16)
    wb = w_branch[0].astype(_BF16)
    wo = w_out[0].astype(_BF16)
    zpad = jnp.zeros((LORA, D_MODEL), _F32)
    wup = jnp.concatenate([decay_up[0], zpad], axis=0).astype(_BF16)
    aup = jnp.concatenate([zpad, icl_up[0]], axis=0).astype(_BF16)
    ones_bd = jnp.kron(jnp.eye(SEG_TILE // RWKV_HEAD, dtype=_F32),
                       jnp.ones((RWKV_HEAD, RWKV_HEAD), _F32)).astype(_BF16)
    g_in = row(norm_in[0])
    prep_w = (g_in, w_all, w_all, row(shift_mu[0]), row(decay_w0[0]), wup, row(icl_a0[0]), aup, row(k_k[0]),
              row(k_a[0]), row(r_k[0]), ones_bd)
    out_w = (row(ln_x_g[0]), row(ln_x_b[0]), wb[1], wo, row(norm_final), ones_bd)
    shared = (g_in, w_all, conv_w[0], row(conv_b[0]), wb, prep_w, out_w)

    mk, mv, mk_heads, mv_heads = _mem_kv(mem_prompt, row(norm_mem[0]), w_mem_kv[0].astype(_BF16))
    y_p, conv_p, shift_p, wkv_p = _group(x_prompt, None, None, None, mk, mv, _mem_prompt, *shared)
    y_s, conv_s, shift_s, wkv_s = _group(
        x_sample, state_conv[0], state_shift[0].reshape(bs, 1, D_SHIFT), state_wkv[0],
        cache_mem_k, cache_mem_v, _mem_sample, *shared)

    return (y_p, y_s, mk_heads[None], mv_heads[None], conv_p, shift_p, wkv_p, conv_s, shift_s, wkv_s)
```

```python
import functools
import math

import jax
import jax.numpy as jnp
from jax import lax
from jax.experimental import pallas as pl
from jax.experimental.pallas import tpu as pltpu

D_MODEL = 1024
N_MEM = 256
RWKV_HEAD = 64
RWKV_HEADS = D_MODEL // RWKV_HEAD
LORA = 64
MEM_HEADS = 4
MEM_HEAD_DIM = D_MODEL // MEM_HEADS
D_SHIFT = 3 * D_MODEL + 2 * LORA
CONV_TAPS = 3
NORM_EPS = 1e-6
GN_EPS = RWKV_HEAD * 1e-5
KK_EPS = 1e-12
DECAY_RATE = math.exp(-0.5)
MASKED_SCORE = -1e30

_OFF_CONV = 0
_OFF_SHIFT = 4 * D_MODEL
_OFF_GRWKV = _OFF_SHIFT + D_SHIFT
_OFF_QMEM = _OFF_GRWKV + D_MODEL
_OFF_GMEM = _OFF_QMEM + D_MODEL
_OFF_MCONV = _OFF_GMEM + D_MODEL
_OFF_MRWKV = _OFF_MCONV + D_MODEL
_OFF_MMEM = _OFF_MRWKV + D_MODEL

V7X_SUBLANES = 8
V7X_LANES = 128
GROUP = V7X_SUBLANES
HEAD_PAIRS = D_MODEL // V7X_LANES
SEG_TILE = 256
TOKEN_TILE = 256
WIDE_TOKEN_TILE = 512
SCAN_TIME_TILE = 64
SAMPLE_MEM_BATCH_TILE = 8
SCAN_UNROLL = 64
SCAN_UPDATE_BLOCK = 32
N_OPS = 6
OP_R, OP_W, OP_K, OP_V, OP_KK, OP_B = range(N_OPS)
N_SCALED = 4
SC_KK, SC_B, SC_K, SC_R = range(N_SCALED)
V7X_VMEM_BYTES = 64 * 1024 * 1024
VMEM_LIMIT = V7X_VMEM_BYTES * 7 // 8

_F32 = jnp.float32
_BF16 = jnp.bfloat16


def _dot(a, b):
    return jnp.dot(a, b, preferred_element_type=_F32)


def _rmsnorm(x, g):
    ms = jnp.mean(x * x, axis=-1, keepdims=True)
    return x * lax.rsqrt(ms + NORM_EPS) * g


def _sigmoid(x):
    return 0.5 * jnp.tanh(0.5 * x) + 0.5


def _silu(x):
    return x * _sigmoid(x)


def _seg_sum(x, ones_bd):
    xb = x.astype(_BF16)
    return jnp.concatenate(
        [_dot(xb[:, c * SEG_TILE:(c + 1) * SEG_TILE], ones_bd) for c in range(D_MODEL // SEG_TILE)], axis=-1)


def _bcast_rows(st, nt):
    nb, _, c = st.shape
    return jnp.broadcast_to(st, (nb, nt, c)).reshape(nb * nt, c)


def _time_index(nb, nt, c):
    return lax.broadcasted_iota(jnp.int32, (nb * nt, c), 0) % nt


def _slab_start(bl, nt):
    return (bl // GROUP) * (nt * GROUP) + bl % GROUP


def _store_step_pairs(ref, n, val, nb, nt, hp0, scr_ref):
    rows = nb * nt
    nt2 = nt // 2
    half = V7X_LANES // 2
    for i in range(val.shape[1] // V7X_LANES):
        scr_ref[...] = val[:, i * V7X_LANES:(i + 1) * V7X_LANES]
        if nt2 % V7X_SUBLANES == 0:
            even = scr_ref[pl.ds(0, rows // 2, stride=2), :]
            odd = scr_ref[pl.ds(1, rows // 2, stride=2), :]
            low = lax.broadcasted_iota(jnp.int32, even.shape, 1) < half
            heads = (jnp.where(low, even, pltpu.roll(odd, half, axis=1)),
                     jnp.where(low, pltpu.roll(even, half, axis=1), odd))
            for h2 in range(2):
                for bl in range(nb):
                    ref[n, hp0 + i, h2, pl.ds(_slab_start(bl, nt2), nt2, stride=GROUP), :] = (
                        heads[h2][bl * nt2:(bl + 1) * nt2])
        else:
            low = lax.broadcasted_iota(jnp.int32, (GROUP, V7X_LANES), 1) < half
            for g in range(nb // GROUP):
                for t2 in range(nt2):
                    src = g * GROUP * nt + 2 * t2
                    even = scr_ref[pl.ds(src, GROUP, stride=nt), :]
                    odd = scr_ref[pl.ds(src + 1, GROUP, stride=nt), :]
                    row0 = g * nt2 * GROUP + t2 * GROUP
                    ref[n, hp0 + i, 0, row0:row0 + GROUP, :] = jnp.where(low, even, pltpu.roll(odd, half, axis=1))
                    ref[n, hp0 + i, 1, row0:row0 + GROUP, :] = jnp.where(low, pltpu.roll(even, half, axis=1), odd)


def _load_slab(ref, nb, nt):
    per_batch = []
    for bl in range(nb):
        per_batch.append(jnp.concatenate(
            [ref[hp, pl.ds(_slab_start(bl, nt), nt, stride=GROUP), :] for hp in range(HEAD_PAIRS)],
            axis=-1))
    return jnp.concatenate(per_batch, axis=0)


def _const_spec(shape, ngrid):
    zeros = (0,) * len(shape)
    if ngrid == 1:
        return pl.BlockSpec(shape, lambda i: zeros, pipeline_mode=pl.Buffered(1))
    return pl.BlockSpec(shape, lambda i, j: zeros, pipeline_mode=pl.Buffered(1))


def _w_spec(offset, width, ngrid):
    shape = (pl.Element(D_MODEL), pl.Element(width))
    if ngrid == 1:
        return pl.BlockSpec(shape, lambda i: (0, offset), pipeline_mode=pl.Buffered(1))
    return pl.BlockSpec(shape, lambda i, j: (0, offset), pipeline_mode=pl.Buffered(1))


def _tile_dims(b, t, long_rows=TOKEN_TILE):
    if t >= long_rows // GROUP:
        nb, nt = GROUP, long_rows // GROUP
    else:
        nb, nt = TOKEN_TILE // t, t
    assert b % nb == 0 and t % nt == 0 and (nb == GROUP or nt == t)
    return nb, nt, (b // nb, t // nt)


def _vmem_bytes(shape, dtype):
    itemsize = jnp.dtype(dtype).itemsize
    dims = [1 if d is None else getattr(d, "block_size", d) for d in shape]
    dims[-1] = -(-dims[-1] // V7X_LANES) * V7X_LANES
    if len(dims) > 1:
        tile_rows = V7X_SUBLANES * 4 // itemsize
        rows = dims[-2]
        if rows < tile_rows:
            tile_rows = 1 << (rows - 1).bit_length()
        dims[-2] = -(-rows // tile_rows) * tile_rows
    return math.prod(dims) * itemsize


def _tile_f32_bytes(rows, n_live):
    return n_live * rows * D_MODEL * 4


def _call(kernel, args, *, grid, in_specs, out_specs, out_shape, scratch_shapes=(), live_bytes=0, name):
    outs = list(zip(out_specs, out_shape)) if isinstance(out_specs, (list, tuple)) else [(out_specs, out_shape)]
    blocks = [(s, a.dtype) for s, a in zip(in_specs, args)] + [(s, o.dtype) for s, o in outs]
    need = sum((1 if s.pipeline_mode is not None else 2) * _vmem_bytes(s.block_shape, dt) for s, dt in blocks)
    need += sum(_vmem_bytes(s.shape, s.dtype) for s in scratch_shapes) + live_bytes
    assert need <= VMEM_LIMIT, (name, need)
    return pl.pallas_call(
        kernel, grid=grid, in_specs=in_specs, out_specs=out_specs, out_shape=out_shape,
        scratch_shapes=list(scratch_shapes),
        compiler_params=pltpu.CompilerParams(
            dimension_semantics=("arbitrary",) * len(grid), vmem_limit_bytes=VMEM_LIMIT),
        name=name,
    )(*args)


def _conv_kernel(x_ref, *rest, has_state):
    if has_state:
        st_ref, g_ref, w_ref, wg_ref, cw_ref, cb_ref, wb_ref, acc_ref, new_ref = rest
    else:
        g_ref, w_ref, wg_ref, cw_ref, cb_ref, wb_ref, acc_ref, new_ref, st_ref = rest

        @pl.when(pl.program_id(1) == 0)
        def _():
            st_ref[...] = jnp.zeros_like(st_ref)

    nb, nt, _ = x_ref.shape
    rows = nb * nt
    xn = _rmsnorm(x_ref[...].reshape(rows, D_MODEL), g_ref[...]).astype(_BF16)
    h = _dot(xn, w_ref[:, 0:D_MODEL])
    cg = _dot(xn, w_ref[:, 2 * D_MODEL:3 * D_MODEL])
    u = cg * h
    s0 = _bcast_rows(st_ref[:, 0:1, :], nt)
    s1 = _bcast_rows(st_ref[:, 1:2, :], nt)
    t = _time_index(nb, nt, D_MODEL)
    prev1 = jnp.where(t == 0, s1, pltpu.roll(u, 1, axis=0))
    prev2 = jnp.where(t == 0, s0, jnp.where(t == 1, s1, pltpu.roll(u, 2, axis=0)))
    conv = cb_ref[...] + prev2 * cw_ref[0:1, :] + prev1 * cw_ref[1:2, :] + u * cw_ref[2:3, :]
    bg = _dot(xn, w_ref[:, D_MODEL:2 * D_MODEL])
    gc = _dot(xn, w_ref[:, 3 * D_MODEL:4 * D_MODEL])
    y = bg * conv * _silu(gc)
    mc = _dot(xn, wg_ref[...])
    acc_ref[...] = (_sigmoid(mc) * _dot(y.astype(_BF16), wb_ref[...])).reshape(nb, nt, D_MODEL)
    new = u.reshape(nb, nt, D_MODEL)[:, nt - (CONV_TAPS - 1):, :]
    new_ref[...] = new
    if not has_state:
        st_ref[...] = new


def _conv(x, st, g, w_bf, cw, cb, wb_bf):
    b, t, _ = x.shape
    nb, nt, grid = _tile_dims(b, t, WIDE_TOKEN_TILE)
    has_state = st is not None
    tok = pl.BlockSpec((nb, nt, D_MODEL), lambda i, j: (i, j, 0))
    st_spec = pl.BlockSpec((nb, CONV_TAPS - 1, D_MODEL), lambda i, j: (i, 0, 0))
    return _call(
        functools.partial(_conv_kernel, has_state=has_state),
        [x] + ([st] if has_state else []) + [g, w_bf, w_bf, cw, cb, wb_bf],
        grid=grid,
        in_specs=[tok] + ([st_spec] if has_state else []) + [
            _const_spec((1, D_MODEL), 2),
            _w_spec(_OFF_CONV, 4 * D_MODEL, 2),
            _w_spec(_OFF_MCONV, D_MODEL, 2),
            _const_spec((CONV_TAPS, D_MODEL), 2),
            _const_spec((1, D_MODEL), 2),
            _const_spec((D_MODEL, D_MODEL), 2),
        ],
        out_specs=[tok, st_spec],
        out_shape=[
            jax.ShapeDtypeStruct((b, t, D_MODEL), _F32),
            jax.ShapeDtypeStruct((b, CONV_TAPS - 1, D_MODEL), _F32),
        ],
        scratch_shapes=[] if has_state else [pltpu.VMEM((nb, CONV_TAPS - 1, D_MODEL), _F32)],
        live_bytes=_tile_f32_bytes(nb * nt, 8),
        name="conv_branch",
    )


def _mem_gate(xn, attn, acc_in, w_ref, wg_ref, wb_ref):
    gm = _dot(xn, w_ref[:, D_MODEL:2 * D_MODEL])
    y = attn * _silu(gm)
    mm = _dot(xn, wg_ref[...])
    return acc_in + _sigmoid(mm) * _dot(y.astype(_BF16), wb_ref[...])


def _mem_prompt_kernel(x_ref, mem_ref, gm_ref, wkv_ref, acc_in_ref, g_ref, w_ref, wg_ref, wb_ref,
                       acc_ref, k4_ref, v4_ref, k_scr, v_scr):
    @pl.when(pl.program_id(1) == 0)
    def _():
        mn = _rmsnorm(mem_ref[...], gm_ref[...]).astype(_BF16)
        for col0, head_ref, scr in ((0, k4_ref, k_scr), (D_MODEL, v4_ref, v_scr)):
            val = _dot(mn, wkv_ref[:, col0:col0 + D_MODEL])
            scr[...] = val.astype(_BF16)
            for hh in range(MEM_HEADS):
                head_ref[:, hh, :] = val[:, hh * MEM_HEAD_DIM:(hh + 1) * MEM_HEAD_DIM]

    xn = _rmsnorm(x_ref[...], g_ref[...]).astype(_BF16)
    q = _dot(xn, w_ref[:, 0:D_MODEL]) * (MEM_HEAD_DIM ** -0.5)
    heads = []
    for hh in range(MEM_HEADS):
        sl = slice(hh * MEM_HEAD_DIM, (hh + 1) * MEM_HEAD_DIM)
        qh = q[:, sl].astype(_BF16)
        kh = k_scr[:, sl]
        vh = v_scr[:, sl]
        s = lax.dot_general(qh, kh, (((1,), (1,)), ((), ())), preferred_element_type=_F32)
        e = jnp.exp(s - jnp.max(s, axis=-1, keepdims=True))
        l = jnp.sum(e, axis=-1, keepdims=True)
        heads.append(_dot(e.astype(_BF16), vh) / l)
    attn = jnp.concatenate(heads, axis=-1)
    acc_ref[...] = _mem_gate(xn, attn, acc_in_ref[...], w_ref, wg_ref, wb_ref)


def _mem_sample_kernel(x_ref, k_ref, v_ref, acc_in_ref, g_ref, w_ref, wg_ref, wb_ref, acc_ref):
    tb, nt, _ = x_ref.shape
    rows = tb * nt
    xn = _rmsnorm(x_ref[...].reshape(rows, D_MODEL), g_ref[...]).astype(_BF16)
    q = (_dot(xn, w_ref[:, 0:D_MODEL]) * (MEM_HEAD_DIM ** -0.5)).reshape(tb, nt, D_MODEL)
    nkh = N_MEM * MEM_HEADS
    q4 = jnp.concatenate(
        [q[:, :, hh * MEM_HEAD_DIM:(hh + 1) * MEM_HEAD_DIM] for hh in range(MEM_HEADS)], axis=1).astype(_BF16)
    k_all = k_ref[...].reshape(tb, nkh, MEM_HEAD_DIM).astype(_BF16)
    v_all = v_ref[...].reshape(tb, nkh, MEM_HEAD_DIM).astype(_BF16)
    s = jnp.einsum("bqd,bkd->bqk", q4, k_all, preferred_element_type=_F32)
    q_head = lax.broadcasted_iota(jnp.int32, s.shape, 1) // nt
    k_head = lax.broadcasted_iota(jnp.int32, s.shape, 2) % MEM_HEADS
    s = jnp.where(q_head == k_head, s, MASKED_SCORE)
    e = jnp.exp(s - jnp.max(s, axis=-1, keepdims=True))
    l = jnp.sum(e, axis=-1, keepdims=True)
    o = jnp.einsum("bqk,bkd->bqd", e.astype(_BF16), v_all, preferred_element_type=_F32) / l
    attn = jnp.concatenate([o[:, hh * nt:(hh + 1) * nt, :] for hh in range(MEM_HEADS)], axis=-1)
    attn = attn.reshape(rows, D_MODEL)
    acc = _mem_gate(xn, attn, acc_in_ref[...].reshape(rows, D_MODEL), w_ref, wg_ref, wb_ref)
    acc_ref[...] = acc.reshape(tb, nt, D_MODEL)


def _mem_prompt(x, mem, kv_w, acc_in, g, w_bf, wb_bf):
    b, t, _ = x.shape
    tm = min(WIDE_TOKEN_TILE, t)
    tok = pl.BlockSpec((None, tm, D_MODEL), lambda i, j: (i, j, 0))
    mem_spec = pl.BlockSpec((None, N_MEM, D_MODEL), lambda i, j: (i, 0, 0))
    heads = pl.BlockSpec((None, N_MEM, MEM_HEADS, MEM_HEAD_DIM), lambda i, j: (i, 0, 0, 0))
    kv_shape = jax.ShapeDtypeStruct((b, N_MEM, MEM_HEADS, MEM_HEAD_DIM), _F32)
    kv_scr = pltpu.VMEM((N_MEM, D_MODEL), _BF16)
    return _call(
        _mem_prompt_kernel, (x, mem, kv_w[0], kv_w[1], acc_in, g, w_bf, w_bf, wb_bf),
        grid=(b, t // tm),
        in_specs=[tok, mem_spec, _const_spec((1, D_MODEL), 2), _const_spec((D_MODEL, 2 * D_MODEL), 2), tok,
                  _const_spec((1, D_MODEL), 2), _w_spec(_OFF_QMEM, 2 * D_MODEL, 2),
                  _w_spec(_OFF_MMEM, D_MODEL, 2), _const_spec((D_MODEL, D_MODEL), 2)],
        out_specs=[tok, heads, heads],
        out_shape=[jax.ShapeDtypeStruct((b, t, D_MODEL), _F32), kv_shape, kv_shape],
        scratch_shapes=[kv_scr, kv_scr],
        live_bytes=_tile_f32_bytes(tm, 8),
        name="mem_prompt",
    )


def _mem_sample(x, mk, mv, acc_in, g, w_bf, wb_bf):
    b, t, _ = x.shape
    tb = SAMPLE_MEM_BATCH_TILE
    tok = pl.BlockSpec((tb, t, D_MODEL), lambda i: (i, 0, 0))
    kv = pl.BlockSpec((None, tb, N_MEM, MEM_HEADS, MEM_HEAD_DIM), lambda i: (0, i, 0, 0, 0))
    kv_bf16 = 2 * _vmem_bytes((tb, N_MEM * MEM_HEADS, MEM_HEAD_DIM), _BF16)
    return _call(
        _mem_sample_kernel, (x, mk, mv, acc_in, g, w_bf, w_bf, wb_bf),
        grid=(b // tb,),
        in_specs=[tok, kv, kv, tok, _const_spec((1, D_MODEL), 1), _w_spec(_OFF_QMEM, 2 * D_MODEL, 1),
                  _w_spec(_OFF_MMEM, D_MODEL, 1), _const_spec((D_MODEL, D_MODEL), 1)],
        out_specs=tok,
        out_shape=jax.ShapeDtypeStruct((b, t, D_MODEL), _F32),
        live_bytes=kv_bf16 + _tile_f32_bytes(tb * t * MEM_HEADS, 4),
        name="mem_sample",
    )


def _prep_tile(x_ref, prev_rows, scr_ref, g_ref, w_ref, wg_ref, mu_ref, w0_ref, wup_ref, a0_ref, aup_ref, kk_ref,
               ka_ref, rk_ref, ones_ref, ops_ref, bonus_ref, sg_ref, sm_ref):
    nb, nt, _ = x_ref.shape
    rows = nb * nt
    xn = _rmsnorm(x_ref[...].reshape(rows, D_MODEL), g_ref[...]).astype(_BF16)
    last = {}

    def shifted(lo, n):
        p = _dot(xn, w_ref[:, lo:lo + n])
        last[lo] = p.reshape(nb, nt, n)[:, nt - 1:, :]
        prev = jnp.where(_time_index(nb, nt, n) == 0, _bcast_rows(prev_rows[:, :, lo:lo + n], nt),
                         pltpu.roll(p, 1, axis=0))
        return p + mu_ref[:, lo:lo + n] * (prev - p)

    lora_in = shifted(3 * D_MODEL, 2 * LORA)
    lora_tanh = jnp.tanh(lora_in).astype(_BF16)
    lora_lin = lora_in.astype(_BF16)
    ones = ones_ref[...]
    seg = lambda x: _dot(x.astype(_BF16), ones)
    for c in range(D_MODEL // SEG_TILE):
        sl = slice(c * SEG_TILE, (c + 1) * SEG_TILE)
        hp0 = c * (SEG_TILE // V7X_LANES)
        r = shifted(c * SEG_TILE, SEG_TILE)
        k = shifted(D_MODEL + c * SEG_TILE, SEG_TILE)
        v = shifted(2 * D_MODEL + c * SEG_TILE, SEG_TILE)
        z = w0_ref[:, sl] + _dot(lora_tanh, wup_ref[:, sl])
        store = lambda n, val: _store_step_pairs(ops_ref, n, val, nb, nt, hp0, scr_ref)
        store(OP_W, jnp.exp(-DECAY_RATE * _sigmoid(z)))
        a = _sigmoid(a0_ref[:, sl] + _dot(lora_lin, aup_ref[:, sl]))
        kk = k * kk_ref[:, sl]
        kk = kk * lax.rsqrt(seg(kk * kk) + KK_EPS)
        kx = k * (1.0 + (a - 1.0) * ka_ref[:, sl])
        store(OP_R, r)
        store(OP_K, kx)
        store(OP_V, v)
        store(OP_KK, kk)
        store(OP_B, kk * a)
        bonus_ref[:, :, sl] = (seg(r * kx * rk_ref[:, sl]) * v).reshape(nb, nt, SEG_TILE)
        lo = D_SHIFT + c * SEG_TILE
        sg_ref[:, :, sl] = _silu(_dot(xn, w_ref[:, lo:lo + SEG_TILE])).reshape(nb, nt, SEG_TILE)
        sm_ref[:, :, sl] = _sigmoid(_dot(xn, wg_ref[:, sl])).reshape(nb, nt, SEG_TILE)
    return jnp.concatenate([last[lo] for lo in sorted(last)], axis=-1)


def _prep_kernel(x_ref, st_ref, *rest):
    new_ref, scr_ref = rest[-2:]
    new_ref[...] = _prep_tile(x_ref, st_ref[...], scr_ref, *rest[:-2])


def _prep_carry_kernel(x_ref, *rest):
    new_ref, scr_ref, st_ref = rest[-3:]

    @pl.when(pl.program_id(1) == 0)
    def _():
        st_ref[...] = jnp.zeros_like(st_ref)

    new = _prep_tile(x_ref, st_ref[...], scr_ref, *rest[:-3])
    st_ref[...] = new
    new_ref[...] = new


def _ops_spec(steps, index):
    return pl.BlockSpec((N_OPS, HEAD_PAIRS, 2, steps // 2 * GROUP, V7X_LANES),
                        lambda i, j: (0, 0, 0, index(i, j), 0))


def _prep(x, st, weights):
    b, t, _ = x.shape
    nb, nt, grid = _tile_dims(b, t)
    has_state = st is not None
    tok = pl.BlockSpec((nb, nt, D_MODEL), lambda i, j: (i, j, 0))
    st_spec = pl.BlockSpec((nb, 1, D_SHIFT), lambda i, j: (i, 0, 0))
    slab = _ops_spec(nb // GROUP * nt, lambda i, j: i * grid[1] + j)
    vec = _const_spec((1, D_MODEL), 2)
    w_specs = [
        vec,
        _w_spec(_OFF_SHIFT, D_SHIFT + D_MODEL, 2),
        _w_spec(_OFF_MRWKV, D_MODEL, 2),
        _const_spec((1, D_SHIFT), 2),
        vec,
        _const_spec((2 * LORA, D_MODEL), 2),
        vec,
        _const_spec((2 * LORA, D_MODEL), 2),
        vec, vec, vec,
        _const_spec((SEG_TILE, SEG_TILE), 2),
    ]
    slab_shape = jax.ShapeDtypeStruct((N_OPS, HEAD_PAIRS, 2, b * t // 2, V7X_LANES), _F32)
    tok_shape = jax.ShapeDtypeStruct((b, t, D_MODEL), _F32)
    stage = pltpu.VMEM((nb * nt, V7X_LANES), _F32)
    return _call(
        _prep_kernel if has_state else _prep_carry_kernel,
        [x] + ([st] if has_state else []) + list(weights),
        grid=grid,
        in_specs=[tok] + ([st_spec] if has_state else []) + w_specs,
        out_specs=[slab] + [tok] * 3 + [st_spec],
        out_shape=[slab_shape] + [tok_shape] * 3 + [jax.ShapeDtypeStruct((b, 1, D_SHIFT), _F32)],
        scratch_shapes=[stage] + ([] if has_state else [pltpu.VMEM((nb, 1, D_SHIFT), _F32)]),
        live_bytes=_tile_f32_bytes(nb * nt, 10),
        name="rwkv_prep",
    )


def _scan_kernel(ops_ref, *rest, zero_init):
    if zero_init:
        o_ref, sT_ref, s_ref, yt_ref, dec_ref, sc_ref = rest
    else:
        s0_ref, o_ref, sT_ref, s_ref, yt_ref, dec_ref, sc_ref = rest
    nk = RWKV_HEAD
    tt = o_ref.shape[0]
    kb = SCAN_UPDATE_BLOCK

    @pl.when(pl.program_id(1) == 0)
    def _():
        if zero_init:
            s_ref[...] = jnp.zeros_like(s_ref)
        else:
            s_ref[...] = s0_ref[...]

    def to_lanes(n, pair):
        src = jnp.minimum(pair, tt // 2 - 1)
        rows = pl.ds(pl.multiple_of(src * GROUP, GROUP), GROUP)
        d = jnp.concatenate(
            [ops_ref[n, hp, h2, rows, :] for h2 in range(2) for hp in range(HEAD_PAIRS)], axis=0)
        tr = d.T
        yt_ref[n, 2 * pair] = tr[0:nk]
        yt_ref[n, 2 * pair + 1] = tr[nk:2 * nk]

    for n in range(N_OPS):
        to_lanes(n, 0)

    dec_ref[...] = jnp.ones_like(dec_ref)

    def scaled_row(n, k):
        return sc_ref[n, pl.ds(k, 1), :]

    def step(pair, tau):
        t = 2 * pair + tau
        p_prev = dec_ref[...]
        p_now = p_prev * yt_ref[OP_W, t]
        inv = 1.0 / p_now
        sc_ref[SC_KK] = yt_ref[OP_KK, t] * p_prev
        sc_ref[SC_B] = yt_ref[OP_B, t] * inv
        sc_ref[SC_K] = yt_ref[OP_K, t] * inv
        sc_ref[SC_R] = yt_ref[OP_R, t] * p_now
        dec_ref[...] = p_now

        def p1(k, acc):
            return acc + s_ref[k] * scaled_row(SC_KK, k)

        sa = lax.fori_loop(0, nk, p1, jnp.zeros((nk, V7X_LANES), _F32), unroll=SCAN_UNROLL)
        vv = yt_ref[OP_V, t]

        def p2(c, o):
            for kk in range(kb):
                k = c * kb + kk
                s = s_ref[k] - sa * scaled_row(SC_B, k) + vv * scaled_row(SC_K, k)
                s_ref[k] = s
                o = o + s * scaled_row(SC_R, k)
            for n in ((2 * c, 2 * c + 1) if tau == 0 else (4 + c,)):
                to_lanes(n, pair + 1)
            return o

        o_ref[t] = lax.fori_loop(0, nk // kb, p2, jnp.zeros((nk, V7X_LANES), _F32))

    def step_pair(pair, carry):
        step(pair, 0)
        step(pair, 1)
        return carry

    lax.fori_loop(0, tt // 2, step_pair, 0)

    def restore(k, carry):
        s_ref[k] = s_ref[k] * dec_ref[pl.ds(k, 1), :]
        return carry

    lax.fori_loop(0, nk, restore, 0, unroll=8)

    @pl.when(pl.program_id(1) == pl.num_programs(1) - 1)
    def _():
        sT_ref[...] = s_ref[...]


def _scan(ops, s0, t):
    groups = ops.shape[3] // (t // 2 * GROUP)
    tt = min(SCAN_TIME_TILE, t)
    nt = t // tt
    assert tt % 2 == 0 and RWKV_HEAD // SCAN_UPDATE_BLOCK == 2 and N_OPS == 6
    st_spec = pl.BlockSpec((RWKV_HEAD, RWKV_HEAD, V7X_LANES), lambda g, j: (0, 0, g))
    zero_init = s0 is None
    return _call(
        functools.partial(_scan_kernel, zero_init=zero_init), [ops] + ([] if zero_init else [s0]),
        grid=(groups, nt),
        in_specs=[_ops_spec(tt, lambda g, j: g * nt + j)] + ([] if zero_init else [st_spec]),
        out_specs=[pl.BlockSpec((tt, RWKV_HEAD, V7X_LANES), lambda g, j: (j, 0, g)), st_spec],
        out_shape=[
            jax.ShapeDtypeStruct((t, RWKV_HEAD, groups * V7X_LANES), _F32),
            jax.ShapeDtypeStruct((RWKV_HEAD, RWKV_HEAD, groups * V7X_LANES), _F32),
        ],
        scratch_shapes=[
            pltpu.VMEM((RWKV_HEAD, RWKV_HEAD, V7X_LANES), _F32),
            pltpu.VMEM((N_OPS, tt + 2, RWKV_HEAD, V7X_LANES), _F32),
            pltpu.VMEM((RWKV_HEAD, V7X_LANES), _F32),
            pltpu.VMEM((N_SCALED, RWKV_HEAD, V7X_LANES), _F32),
        ],
        live_bytes=N_OPS * _vmem_bytes((2 * RWKV_HEAD, V7X_LANES), _F32),
        name="wkv_scan",
    )


def _state_to_lanes(s):
    b = s.shape[0]
    s = s.reshape(b // GROUP, GROUP, HEAD_PAIRS, 2, RWKV_HEAD, RWKV_HEAD)
    return s.transpose(5, 4, 0, 3, 2, 1).reshape(RWKV_HEAD, RWKV_HEAD, b * RWKV_HEADS)


def _state_from_lanes(s, b):
    s = s.reshape(RWKV_HEAD, RWKV_HEAD, b // GROUP, 2, HEAD_PAIRS, GROUP)
    return s.transpose(2, 5, 4, 3, 1, 0).reshape(b, RWKV_HEADS, RWKV_HEAD, RWKV_HEAD)


def _lanes_to_slab(o_ref, slab_ref, nb, nt):
    nk = RWKV_HEAD
    half = V7X_LANES // 2
    low = lax.broadcasted_iota(jnp.int32, (nk, V7X_LANES), 1) < half
    for g in range(nb // GROUP):
        lanes = slice(g * V7X_LANES, (g + 1) * V7X_LANES)
        for i in range(nt // 2):
            y0, y1 = o_ref[2 * i, :, lanes], o_ref[2 * i + 1, :, lanes]
            top = jnp.where(low, y0, pltpu.roll(y1, half, axis=1))
            bot = jnp.where(low, pltpu.roll(y0, half, axis=1), y1)
            d = jnp.concatenate([top, bot], axis=0).T
            for tau in (0, 1):
                for hp in range(HEAD_PAIRS):
                    blk = (tau * HEAD_PAIRS + hp) * GROUP
                    row0 = g * nt * GROUP + (2 * i + tau) * GROUP
                    slab_ref[hp, row0:row0 + GROUP, :] = d[blk:blk + GROUP]


def _out_kernel(o_ref, bonus_ref, sg_ref, sm_ref, acc_ref, x_ref, lng_ref, lnb_ref, wb_ref, wo_ref, nf_ref,
                ones_ref, y_ref, slab_ref):
    nb, nt, _ = x_ref.shape
    rows = nb * nt
    tok = lambda ref: ref[...].reshape(rows, D_MODEL)
    ones = ones_ref[...]
    _lanes_to_slab(o_ref, slab_ref, nb, nt)
    o = _load_slab(slab_ref, nb, nt)
    inv_n = 1.0 / RWKV_HEAD
    d = o - _seg_sum(o, ones) * inv_n
    var = _seg_sum(d * d, ones) * inv_n
    on = d * lax.rsqrt(var + GN_EPS) * lng_ref[...] + lnb_ref[...]
    y = (on + tok(bonus_ref)) * tok(sg_ref)
    merged = tok(acc_ref) + tok(sm_ref) * _dot(y.astype(_BF16), wb_ref[...])
    h = tok(x_ref) + _dot(merged.astype(_BF16), wo_ref[...])
    y_ref[...] = _rmsnorm(h, nf_ref[...]).reshape(nb, nt, D_MODEL)


def _out(o, bonus, sg, sm, acc, x, lng, lnb, wb_bf, wo_bf, nf, ones_bd):
    b, t, _ = x.shape
    nb, nt, grid = _tile_dims(b, t, WIDE_TOKEN_TILE)
    tok = pl.BlockSpec((nb, nt, D_MODEL), lambda i, j: (i, j, 0))
    vec = _const_spec((1, D_MODEL), 2)
    mat = _const_spec((D_MODEL, D_MODEL), 2)
    return _call(
        _out_kernel, (o, bonus, sg, sm, acc, x, lng, lnb, wb_bf, wo_bf, nf, ones_bd),
        grid=grid,
        in_specs=[pl.BlockSpec((nt, RWKV_HEAD, nb // GROUP * V7X_LANES), lambda i, j: (j, 0, i))] + [tok] * 5 + [
            vec, vec, mat, mat, vec, _const_spec((SEG_TILE, SEG_TILE), 2)],
        out_specs=tok,
        out_shape=jax.ShapeDtypeStruct((b, t, D_MODEL), _F32),
        scratch_shapes=[pltpu.VMEM((HEAD_PAIRS, nb * nt, V7X_LANES), _F32)],
        live_bytes=_tile_f32_bytes(nb * nt, 6),
        name="merge_out",
    )


def _group(x, conv_st, shift_st, wkv_st, mk, mv, mem_fn, g_in, w_all, conv_w, conv_b, wb, prep_w, out_w):
    b, t, _ = x.shape
    acc, conv_new = _conv(x, conv_st, g_in, w_all, conv_w, conv_b, wb[0])
    res = mem_fn(x, mk, mv, acc, g_in, w_all, wb[2])
    acc, mem_kv = (res[0], res[1:]) if isinstance(res, (list, tuple)) else (res, [])
    ops, bonus, sg, sm, shift_new = _prep(x, shift_st, prep_w)
    s0 = None if wkv_st is None else _state_to_lanes(wkv_st)
    o, s_fin = _scan(ops, s0, t)
    y = _out(o, bonus, sg, sm, acc, x, *out_w)
    return (y, conv_new[None], shift_new.reshape(b, D_SHIFT)[None], _state_from_lanes(s_fin, b)[None],
            [a[None] for a in mem_kv])


def kernel(x_prompt, x_sample, mem_prompt, cache_mem_k, cache_mem_v, state_conv, state_shift, state_wkv, norm_in, w_in, conv_w, conv_b, shift_mu, decay_w0, decay_up, icl_a0, icl_up, k_k, k_a, r_k, ln_x_g, ln_x_b, norm_mem, w_mem_kv, w_branch, w_out, norm_final):
    assert norm_in.shape[0] == 1, "single-layer step"
    bp = x_prompt.shape[0]
    bs = x_sample.shape[0]

    row = lambda a: a.reshape(1, -1)
    w_all = w_in[0].astype(_BF16)
    wb = w_branch[0].astype(_BF16)
    wo = w_out[0].astype(_BF16)
    zpad = jnp.zeros((LORA, D_MODEL), _F32)
    wup = jnp.concatenate([decay_up[0], zpad], axis=0).astype(_BF16)
    aup = jnp.concatenate([zpad, icl_up[0]], axis=0).astype(_BF16)
    ones_bd = jnp.kron(jnp.eye(SEG_TILE // RWKV_HEAD, dtype=_F32),
                       jnp.ones((RWKV_HEAD, RWKV_HEAD), _F32)).astype(_BF16)
    g_in = row(norm_in[0])
    prep_w = (g_in, w_all, w_all, row(shift_mu[0]), row(decay_w0[0]), wup, row(icl_a0[0]), aup, row(k_k[0]),
              row(k_a[0]), row(r_k[0]), ones_bd)
    out_w = (row(ln_x_g[0]), row(ln_x_b[0]), wb[1], wo, row(norm_final), ones_bd)
    shared = (g_in, w_all, conv_w[0], row(conv_b[0]), wb, prep_w, out_w)

    kv_w = (row(norm_mem[0]), w_mem_kv[0].astype(_BF16))
    y_p, conv_p, shift_p, wkv_p, (mk_p, mv_p) = _group(
        x_prompt, None, None, None, mem_prompt, kv_w, _mem_prompt, *shared)
    y_s, conv_s, shift_s, wkv_s, _ = _group(
        x_sample, state_conv[0], state_shift[0].reshape(bs, 1, D_SHIFT), state_wkv[0],
        cache_mem_k, cache_mem_v, _mem_sample, *shared)

    return (y_p, y_s, mk_p, mv_p, conv_p, shift_p, wkv_p, conv_s, shift_s, wkv_s)
```
